```python
import jax, jax.numpy as jnp
from jax import lax
import numpy as np

D_MODEL = 1024
BATCH = 16
SEQ = 4096
DEPTH = 4

N_META = 16
D_MIX = 1024
FOURIER_HEADS = 4
FOURIER_HEAD_DIM = 64
D_FOURIER = FOURIER_HEADS * FOURIER_HEAD_DIM
POOL_WINDOWS = (2, 4, 8, 16)
N_POOL_GROUPS = 4
POOL_GROUP_DIM = 64
D_POOL = N_POOL_GROUPS * POOL_GROUP_DIM
MLA_HEADS = 4
QK_NOPE_DIM = 128
QK_ROPE_DIM = 64
V_HEAD_DIM = 128
D_ATTN = MLA_HEADS * V_HEAD_DIM
Q_LORA_RANK = 384
KV_LORA_RANK = 256
ROPE_THETA = 10000.0
QUERY_BLOCK = 128
NORM_EPS = 1e-6
IN_SIZES = (D_FOURIER, D_FOURIER, D_POOL, D_POOL, Q_LORA_RANK, KV_LORA_RANK, QK_ROPE_DIM, D_ATTN)
D_IN = 2 * D_FOURIER + 2 * D_POOL + Q_LORA_RANK + KV_LORA_RANK + QK_ROPE_DIM + D_ATTN

kernel_name = "hymba_fnet_pool_mla_encoder"


def rms_norm(x, w):
    xf = x.astype(jnp.float32)
    y = xf * lax.rsqrt(jnp.mean(xf * xf, axis=-1, keepdims=True) + NORM_EPS)
    return (y * w.astype(jnp.float32)).astype(x.dtype)


def rope_tables(length):
    inv = 1.0 / (ROPE_THETA ** (jnp.arange(0, QK_ROPE_DIM, 2, dtype=jnp.float32) / QK_ROPE_DIM))
    ang = jnp.arange(length, dtype=jnp.float32)[:, None] * inv[None, :]
    return jnp.cos(ang), jnp.sin(ang)


def apply_rope(t, cos, sin):
    tf = t.astype(jnp.float32)
    half = QK_ROPE_DIM // 2
    t1, t2 = tf[..., :half], tf[..., half:]
    out = jnp.concatenate([t1 * cos - t2 * sin, t2 * cos + t1 * sin], axis=-1)
    return out.astype(t.dtype)


def fourier_mix(f_in, fourier_w):
    B, L, _ = f_in.shape
    xf = f_in.reshape(B, L, FOURIER_HEADS, FOURIER_HEAD_DIM).astype(jnp.float32)
    y = jnp.fft.fft2(xf, axes=(1, 3), norm="ortho").real.astype(f_in.dtype)
    y = jnp.einsum('blhc,hcd->blhd', y, fourier_w)
    return y.reshape(B, L, D_FOURIER)


def pool_mix(p_in, pool_w, pool_scale):
    B, L, _ = p_in.shape
    xg = p_in.reshape(B, L, N_POOL_GROUPS, POOL_GROUP_DIM).astype(jnp.float32)
    cs = jnp.concatenate([jnp.zeros((B, 1, N_POOL_GROUPS, POOL_GROUP_DIM), jnp.float32),
                          jnp.cumsum(xg, axis=1)], axis=1)
    idx = jnp.arange(L)
    outs = []
    for g, w in enumerate(POOL_WINDOWS):
        lo = jnp.clip(idx - w // 2, 0, L)
        hi = jnp.clip(idx + w // 2, 0, L)
        s = cs[:, hi, g] - cs[:, lo, g]
        cnt = (hi - lo).astype(jnp.float32)[None, :, None]
        outs.append(s / cnt - xg[:, :, g])
    y = jnp.stack(outs, axis=2).astype(p_in.dtype)
    y = jnp.einsum('blgc,gcd->blgd', y, pool_w).reshape(B, L, D_POOL)
    return y * pool_scale


def mla_mix(c_q, c_kv, k_r, q_norm_w, w_uq, kv_norm_w, w_ukv, cos, sin):
    B, L, _ = c_q.shape
    dq = QK_NOPE_DIM + QK_ROPE_DIM
    q = (rms_norm(c_q, q_norm_w) @ w_uq).reshape(B, L, MLA_HEADS, dq)
    q = jnp.concatenate([q[..., :QK_NOPE_DIM],
                         apply_rope(q[..., QK_NOPE_DIM:], cos[None, :, None], sin[None, :, None])], axis=-1)
    kv = (rms_norm(c_kv, kv_norm_w) @ w_ukv).reshape(B, L, MLA_HEADS, QK_NOPE_DIM + V_HEAD_DIM)
    k_nope, v = kv[..., :QK_NOPE_DIM], kv[..., QK_NOPE_DIM:]
    k_rope = apply_rope(k_r, cos[None], sin[None])
    k = jnp.concatenate([k_nope, jnp.broadcast_to(k_rope[:, :, None, :], (B, L, MLA_HEADS, QK_ROPE_DIM))], axis=-1)
    scale = dq ** -0.5

    def attend(qb):
        s = jnp.einsum('bqhd,bkhd->bhqk', qb, k).astype(jnp.float32) * scale
        p = jax.nn.softmax(s, axis=-1).astype(v.dtype)
        return jnp.einsum('bhqk,bkhd->bqhd', p, v)

    o_meta = attend(q[:, :N_META])
    n_blk = (L - N_META) // QUERY_BLOCK
    qb = jnp.moveaxis(q[:, N_META:].reshape(B, n_blk, QUERY_BLOCK, MLA_HEADS, dq), 1, 0)
    o = lax.map(attend, qb)
    o = jnp.moveaxis(o, 0, 1).reshape(B, L - N_META, MLA_HEADS, V_HEAD_DIM)
    return jnp.concatenate([o_meta, o], axis=1).reshape(B, L, D_ATTN)


def setup_inputs(seed: int = 0) -> dict:
    key = jax.random.key(seed)
    ks = jax.random.split(key, 13)
    f32 = jnp.float32
    nrm = lambda k, shape, fan: jax.random.normal(k, shape, f32) * (fan ** -0.5)
    return {
        "x": jax.random.normal(ks[0], (BATCH, SEQ, D_MODEL), f32),
        "meta_tokens": jax.random.normal(ks[1], (N_META, D_MODEL), f32),
        "norm_w": 1.0 + 0.02 * jax.random.normal(ks[2], (DEPTH, D_MODEL), f32),
        "w_in": nrm(ks[3], (DEPTH, D_MODEL, D_IN), D_MODEL),
        "fourier_w": nrm(ks[4], (DEPTH, FOURIER_HEADS, FOURIER_HEAD_DIM, FOURIER_HEAD_DIM), FOURIER_HEAD_DIM),
        "pool_w": nrm(ks[5], (DEPTH, N_POOL_GROUPS, POOL_GROUP_DIM, POOL_GROUP_DIM), POOL_GROUP_DIM),
        "pool_scale": 1.0 + 0.02 * jax.random.normal(ks[6], (DEPTH, D_POOL), f32),
        "q_norm_w": 1.0 + 0.02 * jax.random.normal(ks[7], (DEPTH, Q_LORA_RANK), f32),
        "w_uq": nrm(ks[8], (DEPTH, Q_LORA_RANK, MLA_HEADS * (QK_NOPE_DIM + QK_ROPE_DIM)), Q_LORA_RANK),
        "kv_norm_w": 1.0 + 0.02 * jax.random.normal(ks[9], (DEPTH, KV_LORA_RANK), f32),
        "w_ukv": nrm(ks[10], (DEPTH, KV_LORA_RANK, MLA_HEADS * (QK_NOPE_DIM + V_HEAD_DIM)), KV_LORA_RANK),
        "w_out": nrm(ks[11], (DEPTH, D_MIX, D_MODEL), D_MIX),
        "final_norm_w": 1.0 + 0.02 * jax.random.normal(ks[12], (D_MODEL,), f32),
    }


def reference(x, meta_tokens, norm_w, w_in, fourier_w, pool_w, pool_scale, q_norm_w, w_uq,
              kv_norm_w, w_ukv, w_out, final_norm_w):
    B = x.shape[0]
    h = jnp.concatenate([jnp.broadcast_to(meta_tokens[None].astype(x.dtype), (B, N_META, D_MODEL)), x], axis=1)
    L = h.shape[1]
    cos, sin = rope_tables(L)
    split_idx = np.cumsum(IN_SIZES)[:-1].tolist()
    for l in range(DEPTH):
        u = rms_norm(h, norm_w[l]) @ w_in[l]
        f_in, f_gate, p_in, p_gate, c_q, c_kv, k_r, a_gate = jnp.split(u, split_idx, axis=-1)
        y_f = fourier_mix(f_in, fourier_w[l])
        y_p = pool_mix(p_in, pool_w[l], pool_scale[l])
        y_a = mla_mix(c_q, c_kv, k_r, q_norm_w[l], w_uq[l], kv_norm_w[l], w_ukv[l], cos, sin)
        mix = jnp.concatenate([y_f * jax.nn.silu(f_gate),
                               y_p * jax.nn.silu(p_gate),
                               y_a * jax.nn.silu(a_gate)], axis=-1)
        h = h + mix @ w_out[l]
    return rms_norm(h, final_norm_w)[:, N_META:]
```

```python
import functools
import math

import numpy as np
import jax
import jax.numpy as jnp
from jax import lax
from jax.experimental import pallas as pl
from jax.experimental.pallas import tpu as pltpu

N_META = 16
FOURIER_HEADS = 4
FOURIER_HEAD_DIM = 64
POOL_WINDOWS = (2, 4, 8, 16)
POOL_GROUP_DIM = 64
MLA_HEADS = 4
QK_NOPE_DIM = 128
QK_ROPE_DIM = 64
V_HEAD_DIM = 128
ROPE_THETA = 10000.0
NORM_EPS = 1e-6

LANES = 128
POOL_PAD = 32
VMEM_LIMIT = 56 * 1024 * 1024

F32 = jnp.float32
BF16 = jnp.bfloat16


def _round_up(a, m):
    return (a + m - 1) // m * m


def _dot(a, b):
    return jnp.dot(a, b, preferred_element_type=F32)


def _dot_nt(a, b):
    return lax.dot_general(a, b, (((1,), (1,)), ((), ())), preferred_element_type=F32)


def _rms(x, w):
    return x * lax.rsqrt(jnp.mean(x * x, axis=-1, keepdims=True) + NORM_EPS) * w


def _inproj_kernel(x_ref, nw_ref, w1_ref, qnw_ref, wuq_ref, kvnw_ref, wukv_ref, cos_ref, sin_ref,
                   fp_ref, sg_ref, q_ref, k_ref, v_ref, *, tm, seq_len, scale, d_q, d_kv):
    t = pl.program_id(0)
    n = _rms(x_ref[...], nw_ref[...]).astype(BF16)

    fp_ref[...] = _dot(n, w1_ref[:, 0:512])
    g = _dot(n, w1_ref[:, 512:1536])
    sg_ref[...] = (g * pl.reciprocal(1.0 + jnp.exp(-g))).astype(BF16)

    start = pl.multiple_of(lax.rem(t * tm, seq_len), 16)
    cos = cos_ref[pl.ds(start, tm), :]
    sin = sin_ref[pl.ds(start, tm), :]

    o = 1536
    cq = _rms(_dot(n, w1_ref[:, o:o + d_q]), qnw_ref[...]).astype(BF16)
    o += d_q
    ckv = _rms(_dot(n, w1_ref[:, o:o + d_kv]), kvnw_ref[...]).astype(BF16)
    o += d_kv
    kr = _dot(n, w1_ref[:, o:o + 2 * LANES])

    qf = _dot(cq, wuq_ref[...])
    nq = MLA_HEADS * QK_NOPE_DIM
    nr = MLA_HEADS * QK_ROPE_DIM
    for h in range(MLA_HEADS):
        q_ref[:, 256 * h:256 * h + 128] = (qf[:, 128 * h:128 * h + 128] * scale).astype(BF16)
    for pair in range(MLA_HEADS // 2):
        r = qf[:, nq + 128 * pair:nq + 128 * pair + 128]
        rs = qf[:, nq + nr + 128 * pair:nq + nr + 128 * pair + 128]
        roped = ((r * cos + rs * sin) * scale).astype(BF16)
        for h in (2 * pair, 2 * pair + 1):
            q_ref[:, 256 * h + 128:256 * h + 256] = roped

    kvf = _dot(ckv, wukv_ref[...])
    v_ref[...] = kvf[:, nq:].astype(BF16)
    kroped = kr[:, :LANES] * cos + kr[:, LANES:] * sin
    lane = lax.broadcasted_iota(jnp.int32, kroped.shape, 1)
    k_lo = jnp.where(lane < QK_ROPE_DIM, kroped, 0.0).astype(BF16)
    k_hi = jnp.where(lane >= QK_ROPE_DIM, kroped, 0.0).astype(BF16)
    for h in range(MLA_HEADS):
        k_ref[:, 256 * h:256 * h + 128] = kvf[:, 128 * h:128 * h + 128].astype(BF16)
        k_ref[:, 256 * h + 128:256 * h + 256] = k_lo if h % 2 == 0 else k_hi


def _inproj(h2, nw, w1, qnw, wuq, kvnw, wukv, cos_t, sin_t, *, tm, seq_len):
    n_tok, d = h2.shape
    d_q, d_kv = qnw.shape[1], kvnw.shape[1]
    scale = float((QK_NOPE_DIM + QK_ROPE_DIM) ** -0.5)
    const = lambda a: pl.BlockSpec(a.shape, lambda t: (0,) * a.ndim)
    row = lambda w: pl.BlockSpec((tm, w), lambda t: (t, 0))
    return pl.pallas_call(
        functools.partial(_inproj_kernel, tm=tm, seq_len=seq_len, scale=scale, d_q=d_q, d_kv=d_kv),
        grid=(n_tok // tm,),
        in_specs=[row(d), const(nw), const(w1), const(qnw), const(wuq), const(kvnw), const(wukv),
                  const(cos_t), const(sin_t)],
        out_specs=[row(512), row(1024), row(1024), row(1024), row(512)],
        out_shape=[jax.ShapeDtypeStruct((n_tok, 512), F32),
                   jax.ShapeDtypeStruct((n_tok, 1024), BF16),
                   jax.ShapeDtypeStruct((n_tok, 1024), BF16),
                   jax.ShapeDtypeStruct((n_tok, 1024), BF16),
                   jax.ShapeDtypeStruct((n_tok, 512), BF16)],
        compiler_params=pltpu.CompilerParams(dimension_semantics=("arbitrary",),
                                             vmem_limit_bytes=VMEM_LIMIT),
        name="inproj",
    )(h2, nw, w1, qnw, wuq, kvnw, wukv, cos_t, sin_t)


def _fourier_kernel(x_ref, cs_ref, wbd_ref, mats_ref, o_ref, pq_ref, xs_ref, *, half):
    p = pl.program_id(0)
    b = pl.program_id(1)
    kp = xs_ref.shape[0]

    @pl.when(b == 0)
    def _():
        w = wbd_ref[...]
        pq_ref[:, 0:256] = jnp.dot(cs_ref[0], w, preferred_element_type=F32,
                                   precision=lax.Precision.HIGHEST).astype(BF16)
        pq_ref[:, 256:512] = jnp.dot(cs_ref[1], w, preferred_element_type=F32,
                                     precision=lax.Precision.HIGHEST).astype(BF16)
        xs_ref[half:kp, :] = jnp.zeros((kp - half, xs_ref.shape[1]), F32)

    sign = (1 - 2 * p).astype(F32)
    xs_ref[0:half, :] = x_ref[0, 0:half, :] + sign * x_ref[0, half:2 * half, :]
    ab = _dot(xs_ref[...].astype(BF16), pq_ref[...])
    y = _dot(mats_ref[0, 0], ab[:, 0:256].astype(BF16)) + _dot(mats_ref[0, 1], ab[:, 256:512].astype(BF16))
    o_ref[0] = y[0:half].astype(BF16)


def _fourier(fp3, cs, wbd, mats):
    bsz, seq_len, _ = fp3.shape
    half = seq_len // 2
    hp, kp = mats.shape[2], mats.shape[3]
    return pl.pallas_call(
        functools.partial(_fourier_kernel, half=half),
        grid=(2, bsz),
        in_specs=[pl.BlockSpec((1, seq_len, 256), lambda p, b: (b, 0, 0)),
                  pl.BlockSpec(cs.shape, lambda p, b: (0, 0, 0)),
                  pl.BlockSpec(wbd.shape, lambda p, b: (0, 0)),
                  pl.BlockSpec((1, 2, hp, kp), lambda p, b: (p, 0, 0, 0), pipeline_mode=pl.Buffered(1))],
        out_specs=pl.BlockSpec((1, half, 256), lambda p, b: (b, 0, p)),
        out_shape=jax.ShapeDtypeStruct((bsz, half, 512), BF16),
        scratch_shapes=[pltpu.VMEM((256, 512), BF16), pltpu.VMEM((kp, 256), F32)],
        compiler_params=pltpu.CompilerParams(dimension_semantics=("arbitrary", "arbitrary"),
                                             vmem_limit_bytes=VMEM_LIMIT),
        name="fourier",
    )(fp3, cs, wbd, mats)


def _pool_kernel(x_ref, icnt_ref, wbd_ref, ps_ref, o_ref, xp_ref, s2_ref, s4_ref, s8_ref, *, seq_len):
    pad = POOL_PAD
    tot = seq_len + 2 * pad
    zeros = jnp.zeros((pad, 256), F32)
    xp_ref[0:pad, :] = zeros
    xp_ref[pad + seq_len:tot, :] = zeros
    xp_ref[pad:pad + seq_len, :] = x_ref[0]

    def level(src, dst, lo, hi, d_lo, d_hi):
        chunk = 512
        for c0 in range(lo, hi, chunk):
            c1 = min(c0 + chunk, hi)
            dst[c0:c1, :] = src[c0 + d_lo:c1 + d_lo, :] + src[c0 + d_hi:c1 + d_hi, :]

    level(xp_ref, s2_ref, 8, tot - 8, -1, 0)
    level(s2_ref, s4_ref, 16, tot - 16, -1, 1)
    level(s4_ref, s8_ref, 24, tot - 24, -2, 2)

    chunk = 512
    for c0 in range(0, seq_len, chunk):
        c1 = min(c0 + chunk, seq_len)
        a0, a1 = pad + c0, pad + c1
        s16 = s8_ref[a0 - 4:a1 - 4, :] + s8_ref[a0 + 4:a1 + 4, :]
        lane = lax.broadcasted_iota(jnp.int32, s16.shape, 1)
        hi_sel = jnp.where(lane < 192, s8_ref[a0:a1, :], s16)
        lo_sel = jnp.where(lane < 64, s2_ref[a0:a1, :], s4_ref[a0:a1, :])
        win = jnp.where(lane < 128, lo_sel, hi_sel)
        pooled = (win * icnt_ref[c0:c1, :] - xp_ref[a0:a1, :]).astype(BF16)
        o_ref[0, c0:c1, :] = (_dot(pooled, wbd_ref[...]) * ps_ref[...]).astype(BF16)


def _pool(fp3, icnt, wbd, ps):
    bsz, seq_len, _ = fp3.shape
    tot = seq_len + 2 * POOL_PAD
    return pl.pallas_call(
        functools.partial(_pool_kernel, seq_len=seq_len),
        grid=(bsz,),
        in_specs=[pl.BlockSpec((1, seq_len, 256), lambda b: (b, 0, 1)),
                  pl.BlockSpec(icnt.shape, lambda b: (0, 0)),
                  pl.BlockSpec(wbd.shape, lambda b: (0, 0)),
                  pl.BlockSpec(ps.shape, lambda b: (0, 0))],
        out_specs=pl.BlockSpec((1, seq_len, 256), lambda b: (b, 0, 0)),
        out_shape=jax.ShapeDtypeStruct((bsz, seq_len, 256), BF16),
        scratch_shapes=[pltpu.VMEM((tot, 256), F32)] * 4,
        compiler_params=pltpu.CompilerParams(dimension_semantics=("arbitrary",),
                                             vmem_limit_bytes=VMEM_LIMIT),
        name="pool",
    )(fp3, icnt, wbd, ps)


def _attn_kernel(q_ref, k_ref, v_ref, o_ref, kt_ref, vt_ref, *, seq_len, tq, tk):
    n_kv = seq_len // tk
    rem = seq_len - n_kv * tk
    if rem:
        kt_ref[...] = jnp.zeros(kt_ref.shape, BF16)
        vt_ref[...] = jnp.zeros(vt_ref.shape, BF16)
        kt_ref[0:rem, :] = k_ref[0, n_kv * tk:seq_len, :]
        vt_ref[0:rem, :] = v_ref[0, n_kv * tk:seq_len, :]

    def update(carry, s, v):
        m, l, acc = carry
        m_new = jnp.maximum(m, jnp.max(s, axis=-1, keepdims=True))
        alpha = jnp.exp(m - m_new)
        p = jnp.exp(s - m_new)
        l = alpha * l + jnp.sum(p, axis=-1, keepdims=True)
        acc = alpha * acc + _dot(p.astype(BF16), v)
        return m_new, l, acc

    def attend(q):
        rows = q.shape[0]

        def body(c, carry):
            k0 = pl.multiple_of(c * tk, tk)
            s = _dot_nt(q, k_ref[0, pl.ds(k0, tk), :])
            return update(carry, s, v_ref[0, pl.ds(k0, tk), :])

        carry = (jnp.full((rows, 1), -jnp.inf, F32), jnp.zeros((rows, 1), F32),
                 jnp.zeros((rows, V_HEAD_DIM), F32))
        carry = lax.fori_loop(0, n_kv, body, carry)
        if rem:
            s = _dot_nt(q, kt_ref[...])
            col = lax.broadcasted_iota(jnp.int32, s.shape, 1)
            carry = update(carry, jnp.where(col < rem, s, -jnp.inf), vt_ref[...])
        _, l, acc = carry
        return (acc * pl.reciprocal(l)).astype(BF16)

    n_q = seq_len // tq
    q_rem = seq_len - n_q * tq

    def q_body(i, _):
        r0 = pl.multiple_of(i * tq, tq)
        o_ref[0, pl.ds(r0, tq), :] = attend(q_ref[0, pl.ds(r0, tq), :])
        return 0

    lax.fori_loop(0, n_q, q_body, 0)
    if q_rem:
        o_ref[0, n_q * tq:seq_len, :] = attend(q_ref[0, n_q * tq:seq_len, :])


def _attention(q3, k3, v3, *, tq, tk):
    bsz, seq_len, _ = q3.shape
    tail = max(_round_up(seq_len % tk, LANES), LANES)
    return pl.pallas_call(
        functools.partial(_attn_kernel, seq_len=seq_len, tq=tq, tk=tk),
        grid=(bsz, MLA_HEADS),
        in_specs=[pl.BlockSpec((1, seq_len, 256), lambda b, h: (b, 0, h)),
                  pl.BlockSpec((1, seq_len, 256), lambda b, h: (b, 0, h)),
                  pl.BlockSpec((1, seq_len, V_HEAD_DIM), lambda b, h: (b, 0, h))],
        out_specs=pl.BlockSpec((1, seq_len, V_HEAD_DIM), lambda b, h: (b, 0, h)),
        out_shape=jax.ShapeDtypeStruct((bsz, seq_len, MLA_HEADS * V_HEAD_DIM), BF16),
        scratch_shapes=[pltpu.VMEM((tail, 256), BF16), pltpu.VMEM((tail, V_HEAD_DIM), BF16)],
        compiler_params=pltpu.CompilerParams(dimension_semantics=("arbitrary", "arbitrary"),
                                             vmem_limit_bytes=VMEM_LIMIT),
        name="attention",
    )(q3, k3, v3)


def _outproj_kernel(yf_ref, yp_ref, ya_ref, sg_ref, h_ref, w_ref, fw_ref, o_ref, *, final):
    y = jnp.concatenate([yf_ref[...], yp_ref[...], ya_ref[...]], axis=-1)
    acc = h_ref[...] + _dot(y * sg_ref[...], w_ref[...])
    o_ref[...] = _rms(acc, fw_ref[...]) if final else acc


def _outproj(yf, yp, ya, sg, h2, w, fw, *, tm, final):
    n_tok, d = h2.shape
    row = lambda width: pl.BlockSpec((tm, width), lambda t: (t, 0))
    return pl.pallas_call(
        functools.partial(_outproj_kernel, final=final),
        grid=(n_tok // tm,),
        in_specs=[row(256), row(256), row(512), row(1024), row(d),
                  pl.BlockSpec(w.shape, lambda t: (0, 0)), pl.BlockSpec(fw.shape, lambda t: (0, 0))],
        out_specs=row(d),
        out_shape=jax.ShapeDtypeStruct((n_tok, d), F32),
        input_output_aliases={4: 0},
        compiler_params=pltpu.CompilerParams(dimension_semantics=("arbitrary",),
                                             vmem_limit_bytes=VMEM_LIMIT),
        name="outproj",
    )(yf, yp, ya, sg, h2, w, fw)


def _rope_tables(seq_len, rows):
    inv = 1.0 / (ROPE_THETA ** (jnp.arange(0, QK_ROPE_DIM, 2, dtype=F32) / QK_ROPE_DIM))
    pos = (jnp.arange(rows) % seq_len).astype(F32)
    ang = pos[:, None] * inv[None, :]
    cos, sin = jnp.cos(ang), jnp.sin(ang)
    return jnp.tile(cos, (1, 4)), jnp.tile(jnp.concatenate([-sin, sin], axis=-1), (1, 2))


def _dft_tables(seq_len):
    half = seq_len // 2
    hp, kp = _round_up(half, 16), _round_up(half, LANES)
    k = jnp.arange(hp, dtype=jnp.int32)[:, None]
    l = jnp.arange(kp, dtype=jnp.int32)[None, :]
    valid = (k < half) & (l < half)
    mats = []
    for parity in (0, 1):
        ang = ((2 * k + parity) * l % seq_len).astype(F32) * (2.0 * math.pi / seq_len)
        mats.append(jnp.stack([jnp.where(valid, jnp.cos(ang), 0.0), jnp.where(valid, -jnp.sin(ang), 0.0)]))
    return jnp.stack(mats).astype(BF16)


def _channel_dft(seq_len):
    c = np.arange(FOURIER_HEAD_DIM)
    ang = 2.0 * np.pi * ((c[:, None] * c[None, :]) % FOURIER_HEAD_DIM) / FOURIER_HEAD_DIM
    norm = 1.0 / math.sqrt(seq_len * FOURIER_HEAD_DIM)
    eye = np.eye(FOURIER_HEADS)
    return jnp.asarray(np.stack([np.kron(eye, np.cos(ang) * norm), np.kron(eye, np.sin(ang) * norm)]), F32)


def _block_diag(w):
    g, c, d = w.shape
    out = jnp.zeros((g * c, g * d), w.dtype)
    for i in range(g):
        out = out.at[i * c:(i + 1) * c, i * d:(i + 1) * d].set(w[i])
    return out


def _pool_inv_count(seq_len):
    idx = np.arange(seq_len)
    cols = []
    for w in POOL_WINDOWS:
        cnt = np.clip(idx + w // 2, 0, seq_len) - np.clip(idx - w // 2, 0, seq_len)
        cols.append(np.repeat((1.0 / cnt)[:, None], POOL_GROUP_DIM, axis=1))
    return jnp.asarray(np.concatenate(cols, axis=1), F32)


def _layer_weights(w_in, w_uq, w_ukv):
    sizes = (256, 256, 256, 256, w_uq.shape[0], w_ukv.shape[0], QK_ROPE_DIM, 512)
    offs = np.concatenate([[0], np.cumsum(sizes)])
    f_in, f_gate, p_in, p_gate, c_q, c_kv, k_r, a_gate = (w_in[:, offs[i]:offs[i + 1]] for i in range(8))
    half = QK_ROPE_DIM // 2
    swap = lambda a: jnp.concatenate([a[..., half:], a[..., :half]], axis=-1)
    w1 = jnp.concatenate([f_in, p_in, f_gate, p_gate, a_gate, c_q, c_kv,
                          k_r, k_r, swap(k_r), swap(k_r)], axis=1).astype(BF16)
    uq = w_uq.reshape(w_uq.shape[0], MLA_HEADS, QK_NOPE_DIM + QK_ROPE_DIM)
    q_rope = uq[:, :, QK_NOPE_DIM:]
    wuq = jnp.concatenate([uq[:, :, :QK_NOPE_DIM].reshape(w_uq.shape[0], -1),
                           q_rope.reshape(w_uq.shape[0], -1),
                           swap(q_rope).reshape(w_uq.shape[0], -1)], axis=1).astype(BF16)
    ukv = w_ukv.reshape(w_ukv.shape[0], MLA_HEADS, QK_NOPE_DIM + V_HEAD_DIM)
    wukv = jnp.concatenate([ukv[:, :, :QK_NOPE_DIM].reshape(w_ukv.shape[0], -1),
                            ukv[:, :, QK_NOPE_DIM:].reshape(w_ukv.shape[0], -1)], axis=1).astype(BF16)
    return w1, wuq, wukv


def kernel(x, meta_tokens, norm_w, w_in, fourier_w, pool_w, pool_scale, q_norm_w, w_uq, kv_norm_w, w_ukv,
           w_out, final_norm_w):
    bsz, seq, d = x.shape
    depth = norm_w.shape[0]
    seq_len = seq + N_META
    n_tok = bsz * seq_len
    tm = 256
    assert n_tok % tm == 0 and seq_len % 16 == 0 and (seq_len // 2) % 8 == 0

    h = jnp.concatenate([jnp.broadcast_to(meta_tokens[None].astype(x.dtype), (bsz, N_META, d)), x], axis=1)
    h2 = h.reshape(n_tok, d)

    cos_t, sin_t = _rope_tables(seq_len, seq_len + tm)
    mats = _dft_tables(seq_len)
    cs = _channel_dft(seq_len)
    icnt = _pool_inv_count(seq_len)
    fw = final_norm_w.reshape(1, d)

    for l in range(depth):
        w1, wuq, wukv = _layer_weights(w_in[l], w_uq[l], w_ukv[l])
        fp, sg, q, k, v = _inproj(h2, norm_w[l].reshape(1, d), w1, q_norm_w[l].reshape(1, -1), wuq,
                                  kv_norm_w[l].reshape(1, -1), wukv, cos_t, sin_t, tm=tm, seq_len=seq_len)
        fp3 = fp.reshape(bsz, seq_len, 512)
        yf = _fourier(fp3, cs, _block_diag(fourier_w[l]), mats).reshape(n_tok, 256)
        yp = _pool(fp3, icnt, _block_diag(pool_w[l]).astype(BF16), pool_scale[l].reshape(1, -1)).reshape(n_tok, 256)
        ya = _attention(q.reshape(bsz, seq_len, 1024), k.reshape(bsz, seq_len, 1024),
                        v.reshape(bsz, seq_len, 512), tq=512, tk=512).reshape(n_tok, 512)
        h2 = _outproj(yf, yp, ya, sg, h2, w_out[l].astype(BF16), fw, tm=tm, final=(l == depth - 1))

    return h2.reshape(bsz, seq_len, d)[:, N_META:]
```

```python
import functools
import math

import numpy as np
import jax
import jax.numpy as jnp
from jax import lax
from jax.experimental import pallas as pl
from jax.experimental.pallas import tpu as pltpu

N_META = 16
FOURIER_HEADS = 4
FOURIER_HEAD_DIM = 64
POOL_WINDOWS = (2, 4, 8, 16)
POOL_GROUP_DIM = 64
MLA_HEADS = 4
QK_NOPE_DIM = 128
QK_ROPE_DIM = 64
V_HEAD_DIM = 128
ROPE_THETA = 10000.0
NORM_EPS = 1e-6

LANES = 128
POOL_PAD = 32
VMEM_LIMIT = 56 * 1024 * 1024
ATTN_TQ = 528
ATTN_TK = 1024

F32 = jnp.float32
BF16 = jnp.bfloat16


def _round_up(a, m):
    return (a + m - 1) // m * m


def _dot(a, b):
    return jnp.dot(a, b, preferred_element_type=F32)


def _dot_nt(a, b):
    return lax.dot_general(a, b, (((1,), (1,)), ((), ())), preferred_element_type=F32)


def _rms(x, w):
    return x * lax.rsqrt(jnp.mean(x * x, axis=-1, keepdims=True) + NORM_EPS) * w


def _inproj_kernel(x_ref, nw_ref, w1_ref, qnw_ref, wuq_ref, kvnw_ref, wukv_ref, cos_ref, sin_ref,
                   fp_ref, sg_ref, q_ref, k_ref, v_ref, *, tm, seq_len, scale, d_q, d_kv):
    t = pl.program_id(0)
    n = _rms(x_ref[...], nw_ref[...]).astype(BF16)

    fp_ref[...] = _dot(n, w1_ref[:, 0:512])
    g = _dot(n, w1_ref[:, 512:1536])
    sg_ref[...] = (g * pl.reciprocal(1.0 + jnp.exp(-g))).astype(BF16)

    start = pl.multiple_of(lax.rem(t * tm, seq_len), 16)
    cos = cos_ref[pl.ds(start, tm), :]
    sin = sin_ref[pl.ds(start, tm), :]

    o = 1536
    cq = _rms(_dot(n, w1_ref[:, o:o + d_q]), qnw_ref[...]).astype(BF16)
    o += d_q
    ckv = _rms(_dot(n, w1_ref[:, o:o + d_kv]), kvnw_ref[...]).astype(BF16)
    o += d_kv
    kr = _dot(n, w1_ref[:, o:o + 2 * LANES])

    qf = _dot(cq, wuq_ref[...])
    nq = MLA_HEADS * QK_NOPE_DIM
    nr = MLA_HEADS * QK_ROPE_DIM
    for h in range(MLA_HEADS):
        q_ref[:, 256 * h:256 * h + 128] = (qf[:, 128 * h:128 * h + 128] * scale).astype(BF16)
    for pair in range(MLA_HEADS // 2):
        r = qf[:, nq + 128 * pair:nq + 128 * pair + 128]
        rs = qf[:, nq + nr + 128 * pair:nq + nr + 128 * pair + 128]
        roped = ((r * cos + rs * sin) * scale).astype(BF16)
        for h in (2 * pair, 2 * pair + 1):
            q_ref[:, 256 * h + 128:256 * h + 256] = roped

    kvf = _dot(ckv, wukv_ref[...])
    v_ref[...] = kvf[:, nq:].astype(BF16)
    kroped = kr[:, :LANES] * cos + kr[:, LANES:] * sin
    lane = lax.broadcasted_iota(jnp.int32, kroped.shape, 1)
    k_lo = jnp.where(lane < QK_ROPE_DIM, kroped, 0.0).astype(BF16)
    k_hi = jnp.where(lane >= QK_ROPE_DIM, kroped, 0.0).astype(BF16)
    for h in range(MLA_HEADS):
        k_ref[:, 256 * h:256 * h + 128] = kvf[:, 128 * h:128 * h + 128].astype(BF16)
        k_ref[:, 256 * h + 128:256 * h + 256] = k_lo if h % 2 == 0 else k_hi


def _inproj(h2, nw, w1, qnw, wuq, kvnw, wukv, cos_t, sin_t, *, tm, seq_len):
    n_tok, d = h2.shape
    d_q, d_kv = qnw.shape[1], kvnw.shape[1]
    scale = float((QK_NOPE_DIM + QK_ROPE_DIM) ** -0.5 * math.log2(math.e))
    const = lambda a: pl.BlockSpec(a.shape, lambda t: (0,) * a.ndim)
    row = lambda w: pl.BlockSpec((tm, w), lambda t: (t, 0))
    return pl.pallas_call(
        functools.partial(_inproj_kernel, tm=tm, seq_len=seq_len, scale=scale, d_q=d_q, d_kv=d_kv),
        grid=(n_tok // tm,),
        in_specs=[row(d), const(nw), const(w1), const(qnw), const(wuq), const(kvnw), const(wukv),
                  const(cos_t), const(sin_t)],
        out_specs=[row(512), row(1024), row(1024), row(1024), row(512)],
        out_shape=[jax.ShapeDtypeStruct((n_tok, 512), F32),
                   jax.ShapeDtypeStruct((n_tok, 1024), BF16),
                   jax.ShapeDtypeStruct((n_tok, 1024), BF16),
                   jax.ShapeDtypeStruct((n_tok, 1024), BF16),
                   jax.ShapeDtypeStruct((n_tok, 512), BF16)],
        compiler_params=pltpu.CompilerParams(dimension_semantics=("arbitrary",),
                                             vmem_limit_bytes=VMEM_LIMIT),
        name="inproj",
    )(h2, nw, w1, qnw, wuq, kvnw, wukv, cos_t, sin_t)


def _fourier_kernel(x_ref, cs_ref, wbd_ref, mats_ref, o_ref, pq_ref, xs_ref, *, half):
    p = pl.program_id(0)
    b = pl.program_id(1)
    kp = xs_ref.shape[0]

    @pl.when(b == 0)
    def _():
        w = wbd_ref[...]
        pq_ref[:, 0:256] = jnp.dot(cs_ref[0], w, preferred_element_type=F32,
                                   precision=lax.Precision.HIGHEST).astype(BF16)
        pq_ref[:, 256:512] = jnp.dot(cs_ref[1], w, preferred_element_type=F32,
                                     precision=lax.Precision.HIGHEST).astype(BF16)
        xs_ref[half:kp, :] = jnp.zeros((kp - half, xs_ref.shape[1]), F32)

    sign = (1 - 2 * p).astype(F32)
    xs_ref[0:half, :] = x_ref[0, 0:half, :] + sign * x_ref[0, half:2 * half, :]
    ab = _dot(xs_ref[...].astype(BF16), pq_ref[...])
    y = _dot(mats_ref[0, 0], ab[:, 0:256].astype(BF16)) + _dot(mats_ref[0, 1], ab[:, 256:512].astype(BF16))
    o_ref[0] = y[0:half].astype(BF16)


def _fourier(fp3, cs, wbd, mats):
    bsz, seq_len, _ = fp3.shape
    half = seq_len // 2
    hp, kp = mats.shape[2], mats.shape[3]
    return pl.pallas_call(
        functools.partial(_fourier_kernel, half=half),
        grid=(2, bsz),
        in_specs=[pl.BlockSpec((1, seq_len, 256), lambda p, b: (b, 0, 0)),
                  pl.BlockSpec(cs.shape, lambda p, b: (0, 0, 0)),
                  pl.BlockSpec(wbd.shape, lambda p, b: (0, 0)),
                  pl.BlockSpec((1, 2, hp, kp), lambda p, b: (p, 0, 0, 0), pipeline_mode=pl.Buffered(1))],
        out_specs=pl.BlockSpec((1, half, 256), lambda p, b: (b, 0, p)),
        out_shape=jax.ShapeDtypeStruct((bsz, half, 512), BF16),
        scratch_shapes=[pltpu.VMEM((256, 512), BF16), pltpu.VMEM((kp, 256), F32)],
        compiler_params=pltpu.CompilerParams(dimension_semantics=("arbitrary", "arbitrary"),
                                             vmem_limit_bytes=VMEM_LIMIT),
        name="fourier",
    )(fp3, cs, wbd, mats)


def _pool_kernel(x_ref, icnt_ref, wbd_ref, ps_ref, o_ref, xp_ref, s2_ref, s4_ref, s8_ref, *, seq_len):
    pad = POOL_PAD
    tot = seq_len + 2 * pad
    zeros = jnp.zeros((pad, 256), F32)
    xp_ref[0:pad, :] = zeros
    xp_ref[pad + seq_len:tot, :] = zeros
    xp_ref[pad:pad + seq_len, :] = x_ref[0]

    def level(src, dst, lo, hi, d_lo, d_hi):
        chunk = 512
        for c0 in range(lo, hi, chunk):
            c1 = min(c0 + chunk, hi)
            dst[c0:c1, :] = src[c0 + d_lo:c1 + d_lo, :] + src[c0 + d_hi:c1 + d_hi, :]

    level(xp_ref, s2_ref, 8, tot - 8, -1, 0)
    level(s2_ref, s4_ref, 16, tot - 16, -1, 1)
    level(s4_ref, s8_ref, 24, tot - 24, -2, 2)

    chunk = 512
    for c0 in range(0, seq_len, chunk):
        c1 = min(c0 + chunk, seq_len)
        a0, a1 = pad + c0, pad + c1
        s16 = s8_ref[a0 - 4:a1 - 4, :] + s8_ref[a0 + 4:a1 + 4, :]
        lane = lax.broadcasted_iota(jnp.int32, s16.shape, 1)
        hi_sel = jnp.where(lane < 192, s8_ref[a0:a1, :], s16)
        lo_sel = jnp.where(lane < 64, s2_ref[a0:a1, :], s4_ref[a0:a1, :])
        win = jnp.where(lane < 128, lo_sel, hi_sel)
        pooled = (win * icnt_ref[c0:c1, :] - xp_ref[a0:a1, :]).astype(BF16)
        o_ref[0, c0:c1, :] = (_dot(pooled, wbd_ref[...]) * ps_ref[...]).astype(BF16)


def _pool(fp3, icnt, wbd, ps):
    bsz, seq_len, _ = fp3.shape
    tot = seq_len + 2 * POOL_PAD
    return pl.pallas_call(
        functools.partial(_pool_kernel, seq_len=seq_len),
        grid=(bsz,),
        in_specs=[pl.BlockSpec((1, seq_len, 256), lambda b: (b, 0, 1)),
                  pl.BlockSpec(icnt.shape, lambda b: (0, 0)),
                  pl.BlockSpec(wbd.shape, lambda b: (0, 0)),
                  pl.BlockSpec(ps.shape, lambda b: (0, 0))],
        out_specs=pl.BlockSpec((1, seq_len, 256), lambda b: (b, 0, 0)),
        out_shape=jax.ShapeDtypeStruct((bsz, seq_len, 256), BF16),
        scratch_shapes=[pltpu.VMEM((tot, 256), F32)] * 4,
        compiler_params=pltpu.CompilerParams(dimension_semantics=("arbitrary",),
                                             vmem_limit_bytes=VMEM_LIMIT),
        name="pool",
    )(fp3, icnt, wbd, ps)


def _kv_chunks(lp, tk):
    n = max(lp // tk, 1)
    return [(i * tk, tk if i < n - 1 else lp - i * tk) for i in range(n)]


def _attn_kernel(q_ref, k_ref, v_ref, o_ref, kp_ref, vp_ref, *, seq_len, tq, tk):
    lp = kp_ref.shape[0]
    pad = lp - seq_len
    kp_ref[0:seq_len, :] = k_ref[0]
    vp_ref[0:seq_len, 0:V_HEAD_DIM] = v_ref[0]
    vp_ref[0:seq_len, V_HEAD_DIM:] = jnp.ones((seq_len, V_HEAD_DIM), BF16)
    if pad:
        kp_ref[seq_len:lp, :] = jnp.zeros((pad, kp_ref.shape[1]), BF16)
        vp_ref[seq_len:lp, :] = jnp.zeros((pad, vp_ref.shape[1]), BF16)
    chunks = _kv_chunks(lp, tk)

    def attend(q):
        m = acc = None
        for c0, cw in chunks:
            s = _dot_nt(q, kp_ref[c0:c0 + cw, :])
            if c0 + cw > seq_len:
                col = lax.broadcasted_iota(jnp.int32, (q.shape[0], LANES), 1) + (c0 + cw - LANES)
                s = jnp.concatenate([s[:, :cw - LANES], jnp.where(col < seq_len, s[:, cw - LANES:], -jnp.inf)],
                                    axis=-1)
            m_c = jnp.max(s, axis=-1, keepdims=True)
            m_new = m_c if m is None else jnp.maximum(m, m_c)
            pv = _dot(jnp.exp2(s - m_new).astype(BF16), vp_ref[c0:c0 + cw, :])
            acc = pv if m is None else jnp.exp2(m - m_new) * acc + pv
            m = m_new
        return (acc[:, :V_HEAD_DIM] * pl.reciprocal(acc[:, V_HEAD_DIM:])).astype(BF16)

    def q_body(i, _):
        r0 = pl.multiple_of(jnp.minimum(i * tq, seq_len - tq), 16)
        o_ref[0, pl.ds(r0, tq), :] = attend(q_ref[0, pl.ds(r0, tq), :])
        return 0

    lax.fori_loop(0, pl.cdiv(seq_len, tq), q_body, 0)


def _attention(q3, k3, v3, *, tq, tk):
    bsz, seq_len, _ = q3.shape
    lp = _round_up(seq_len, LANES)
    return pl.pallas_call(
        functools.partial(_attn_kernel, seq_len=seq_len, tq=tq, tk=tk),
        grid=(bsz, MLA_HEADS),
        in_specs=[pl.BlockSpec((1, seq_len, 256), lambda b, h: (b, 0, h)),
                  pl.BlockSpec((1, seq_len, 256), lambda b, h: (b, 0, h)),
                  pl.BlockSpec((1, seq_len, V_HEAD_DIM), lambda b, h: (b, 0, h))],
        out_specs=pl.BlockSpec((1, seq_len, V_HEAD_DIM), lambda b, h: (b, 0, h)),
        out_shape=jax.ShapeDtypeStruct((bsz, seq_len, MLA_HEADS * V_HEAD_DIM), BF16),
        scratch_shapes=[pltpu.VMEM((lp, 256), BF16), pltpu.VMEM((lp, 2 * V_HEAD_DIM), BF16)],
        compiler_params=pltpu.CompilerParams(dimension_semantics=("arbitrary", "arbitrary"),
                                             vmem_limit_bytes=VMEM_LIMIT),
        name="attention",
    )(q3, k3, v3)


def _outproj_kernel(yf_ref, yp_ref, ya_ref, sg_ref, h_ref, w_ref, fw_ref, o_ref, *, final):
    y = jnp.concatenate([yf_ref[...], yp_ref[...], ya_ref[...]], axis=-1)
    acc = h_ref[...] + _dot(y * sg_ref[...], w_ref[...])
    o_ref[...] = _rms(acc, fw_ref[...]) if final else acc


def _outproj(yf, yp, ya, sg, h2, w, fw, *, tm, final):
    n_tok, d = h2.shape
    row = lambda width: pl.BlockSpec((tm, width), lambda t: (t, 0))
    return pl.pallas_call(
        functools.partial(_outproj_kernel, final=final),
        grid=(n_tok // tm,),
        in_specs=[row(256), row(256), row(512), row(1024), row(d),
                  pl.BlockSpec(w.shape, lambda t: (0, 0)), pl.BlockSpec(fw.shape, lambda t: (0, 0))],
        out_specs=row(d),
        out_shape=jax.ShapeDtypeStruct((n_tok, d), F32),
        input_output_aliases={4: 0},
        compiler_params=pltpu.CompilerParams(dimension_semantics=("arbitrary",),
                                             vmem_limit_bytes=VMEM_LIMIT),
        name="outproj",
    )(yf, yp, ya, sg, h2, w, fw)


def _rope_tables(seq_len, rows):
    inv = 1.0 / (ROPE_THETA ** (jnp.arange(0, QK_ROPE_DIM, 2, dtype=F32) / QK_ROPE_DIM))
    pos = (jnp.arange(rows) % seq_len).astype(F32)
    ang = pos[:, None] * inv[None, :]
    cos, sin = jnp.cos(ang), jnp.sin(ang)
    return jnp.tile(cos, (1, 4)), jnp.tile(jnp.concatenate([-sin, sin], axis=-1), (1, 2))


def _dft_tables(seq_len):
    half = seq_len // 2
    hp, kp = _round_up(half, 16), _round_up(half, LANES)
    k = jnp.arange(hp, dtype=jnp.int32)[:, None]
    l = jnp.arange(kp, dtype=jnp.int32)[None, :]
    valid = (k < half) & (l < half)
    mats = []
    for parity in (0, 1):
        ang = ((2 * k + parity) * l % seq_len).astype(F32) * (2.0 * math.pi / seq_len)
        mats.append(jnp.stack([jnp.where(valid, jnp.cos(ang), 0.0), jnp.where(valid, -jnp.sin(ang), 0.0)]))
    return jnp.stack(mats).astype(BF16)


def _channel_dft(seq_len):
    c = np.arange(FOURIER_HEAD_DIM)
    ang = 2.0 * np.pi * ((c[:, None] * c[None, :]) % FOURIER_HEAD_DIM) / FOURIER_HEAD_DIM
    norm = 1.0 / math.sqrt(seq_len * FOURIER_HEAD_DIM)
    eye = np.eye(FOURIER_HEADS)
    return jnp.asarray(np.stack([np.kron(eye, np.cos(ang) * norm), np.kron(eye, np.sin(ang) * norm)]), F32)


def _block_diag(w):
    g, c, d = w.shape
    out = jnp.zeros((g * c, g * d), w.dtype)
    for i in range(g):
        out = out.at[i * c:(i + 1) * c, i * d:(i + 1) * d].set(w[i])
    return out


def _pool_inv_count(seq_len):
    idx = np.arange(seq_len)
    cols = []
    for w in POOL_WINDOWS:
        cnt = np.clip(idx + w // 2, 0, seq_len) - np.clip(idx - w // 2, 0, seq_len)
        cols.append(np.repeat((1.0 / cnt)[:, None], POOL_GROUP_DIM, axis=1))
    return jnp.asarray(np.concatenate(cols, axis=1), F32)


def _layer_weights(w_in, w_uq, w_ukv):
    sizes = (256, 256, 256, 256, w_uq.shape[0], w_ukv.shape[0], QK_ROPE_DIM, 512)
    offs = np.concatenate([[0], np.cumsum(sizes)])
    f_in, f_gate, p_in, p_gate, c_q, c_kv, k_r, a_gate = (w_in[:, offs[i]:offs[i + 1]] for i in range(8))
    half = QK_ROPE_DIM // 2
    swap = lambda a: jnp.concatenate([a[..., half:], a[..., :half]], axis=-1)
    w1 = jnp.concatenate([f_in, p_in, f_gate, p_gate, a_gate, c_q, c_kv,
                          k_r, k_r, swap(k_r), swap(k_r)], axis=1).astype(BF16)
    uq = w_uq.reshape(w_uq.shape[0], MLA_HEADS, QK_NOPE_DIM + QK_ROPE_DIM)
    q_rope = uq[:, :, QK_NOPE_DIM:]
    wuq = jnp.concatenate([uq[:, :, :QK_NOPE_DIM].reshape(w_uq.shape[0], -1),
                           q_rope.reshape(w_uq.shape[0], -1),
                           swap(q_rope).reshape(w_uq.shape[0], -1)], axis=1).astype(BF16)
    ukv = w_ukv.reshape(w_ukv.shape[0], MLA_HEADS, QK_NOPE_DIM + V_HEAD_DIM)
    wukv = jnp.concatenate([ukv[:, :, :QK_NOPE_DIM].reshape(w_ukv.shape[0], -1),
                            ukv[:, :, QK_NOPE_DIM:].reshape(w_ukv.shape[0], -1)], axis=1).astype(BF16)
    return w1, wuq, wukv


def kernel(x, meta_tokens, norm_w, w_in, fourier_w, pool_w, pool_scale, q_norm_w, w_uq, kv_norm_w, w_ukv,
           w_out, final_norm_w):
    bsz, seq, d = x.shape
    depth = norm_w.shape[0]
    seq_len = seq + N_META
    n_tok = bsz * seq_len
    tm = 256
    assert n_tok % tm == 0 and seq_len % 16 == 0 and (seq_len // 2) % 8 == 0

    h = jnp.concatenate([jnp.broadcast_to(meta_tokens[None].astype(x.dtype), (bsz, N_META, d)), x], axis=1)
    h2 = h.reshape(n_tok, d)

    cos_t, sin_t = _rope_tables(seq_len, seq_len + tm)
    mats = _dft_tables(seq_len)
    cs = _channel_dft(seq_len)
    icnt = _pool_inv_count(seq_len)
    fw = final_norm_w.reshape(1, d)

    for l in range(depth):
        w1, wuq, wukv = _layer_weights(w_in[l], w_uq[l], w_ukv[l])
        fp, sg, q, k, v = _inproj(h2, norm_w[l].reshape(1, d), w1, q_norm_w[l].reshape(1, -1), wuq,
                                  kv_norm_w[l].reshape(1, -1), wukv, cos_t, sin_t, tm=tm, seq_len=seq_len)
        fp3 = fp.reshape(bsz, seq_len, 512)
        yf = _fourier(fp3, cs, _block_diag(fourier_w[l]), mats).reshape(n_tok, 256)
        yp = _pool(fp3, icnt, _block_diag(pool_w[l]).astype(BF16), pool_scale[l].reshape(1, -1)).reshape(n_tok, 256)
        ya = _attention(q.reshape(bsz, seq_len, 1024), k.reshape(bsz, seq_len, 1024),
                        v.reshape(bsz, seq_len, 512), tq=ATTN_TQ, tk=ATTN_TK).reshape(n_tok, 512)
        h2 = _outproj(yf, yp, ya, sg, h2, w_out[l].astype(BF16), fw, tm=tm, final=(l == depth - 1))

    return h2.reshape(bsz, seq_len, d)[:, N_META:]
```

```python
import functools
import math

import numpy as np
import jax
import jax.numpy as jnp
from jax import lax
from jax.experimental import pallas as pl
from jax.experimental.pallas import tpu as pltpu

N_META = 16
FOURIER_HEADS = 4
FOURIER_HEAD_DIM = 64
POOL_WINDOWS = (2, 4, 8, 16)
POOL_GROUP_DIM = 64
MLA_HEADS = 4
QK_NOPE_DIM = 128
QK_ROPE_DIM = 64
V_HEAD_DIM = 128
ROPE_THETA = 10000.0
NORM_EPS = 1e-6

LANES = 128
POOL_PAD = 32
VMEM_LIMIT = 56 * 1024 * 1024
ATTN_TQ = 1056
ATTN_TK = 1024

F32 = jnp.float32
BF16 = jnp.bfloat16


def _round_up(a, m):
    return (a + m - 1) // m * m


def _dot(a, b):
    return jnp.dot(a, b, preferred_element_type=F32)


def _dot_nt(a, b):
    return lax.dot_general(a, b, (((1,), (1,)), ((), ())), preferred_element_type=F32)


def _rms(x, w):
    return x * lax.rsqrt(jnp.mean(x * x, axis=-1, keepdims=True) + NORM_EPS) * w


def _inproj_kernel(x_ref, nw_ref, w1_ref, qnw_ref, wuq_ref, kvnw_ref, wukv_ref, cos_ref, sin_ref,
                   fp_ref, sg_ref, q_ref, k_ref, v_ref, *, tm, seq_len, scale, d_q, d_kv):
    t = pl.program_id(0)
    n = _rms(x_ref[...], nw_ref[...]).astype(BF16)

    fp_ref[...] = _dot(n, w1_ref[:, 0:512])
    g = _dot(n, w1_ref[:, 512:1536])
    sg_ref[...] = (g * pl.reciprocal(1.0 + jnp.exp(-g))).astype(BF16)

    start = pl.multiple_of(lax.rem(t * tm, seq_len), 16)
    cos = cos_ref[pl.ds(start, tm), :]
    sin = sin_ref[pl.ds(start, tm), :]

    o = 1536
    cq = _rms(_dot(n, w1_ref[:, o:o + d_q]), qnw_ref[...]).astype(BF16)
    o += d_q
    ckv = _rms(_dot(n, w1_ref[:, o:o + d_kv]), kvnw_ref[...]).astype(BF16)
    o += d_kv
    kr = _dot(n, w1_ref[:, o:o + 2 * LANES])

    qf = _dot(cq, wuq_ref[...])
    nq = MLA_HEADS * QK_NOPE_DIM
    nr = MLA_HEADS * QK_ROPE_DIM
    for h in range(MLA_HEADS):
        q_ref[:, 256 * h:256 * h + 128] = (qf[:, 128 * h:128 * h + 128] * scale).astype(BF16)
    for pair in range(MLA_HEADS // 2):
        r = qf[:, nq + 128 * pair:nq + 128 * pair + 128]
        rs = qf[:, nq + nr + 128 * pair:nq + nr + 128 * pair + 128]
        roped = ((r * cos + rs * sin) * scale).astype(BF16)
        for h in (2 * pair, 2 * pair + 1):
            q_ref[:, 256 * h + 128:256 * h + 256] = roped

    kvf = _dot(ckv, wukv_ref[...])
    v_ref[...] = kvf[:, nq:].astype(BF16)
    kroped = kr[:, :LANES] * cos + kr[:, LANES:] * sin
    lane = lax.broadcasted_iota(jnp.int32, kroped.shape, 1)
    k_lo = jnp.where(lane < QK_ROPE_DIM, kroped, 0.0).astype(BF16)
    k_hi = jnp.where(lane >= QK_ROPE_DIM, kroped, 0.0).astype(BF16)
    for h in range(MLA_HEADS):
        k_ref[:, 256 * h:256 * h + 128] = kvf[:, 128 * h:128 * h + 128].astype(BF16)
        k_ref[:, 256 * h + 128:256 * h + 256] = k_lo if h % 2 == 0 else k_hi


def _inproj(h2, nw, w1, qnw, wuq, kvnw, wukv, cos_t, sin_t, *, tm, seq_len):
    n_tok, d = h2.shape
    d_q, d_kv = qnw.shape[1], kvnw.shape[1]
    scale = float((QK_NOPE_DIM + QK_ROPE_DIM) ** -0.5 * math.log2(math.e))
    const = lambda a: pl.BlockSpec(a.shape, lambda t: (0,) * a.ndim)
    row = lambda w: pl.BlockSpec((tm, w), lambda t: (t, 0))
    return pl.pallas_call(
        functools.partial(_inproj_kernel, tm=tm, seq_len=seq_len, scale=scale, d_q=d_q, d_kv=d_kv),
        grid=(n_tok // tm,),
        in_specs=[row(d), const(nw), const(w1), const(qnw), const(wuq), const(kvnw), const(wukv),
                  const(cos_t), const(sin_t)],
        out_specs=[row(512), row(1024), row(1024), row(1024), row(512)],
        out_shape=[jax.ShapeDtypeStruct((n_tok, 512), F32),
                   jax.ShapeDtypeStruct((n_tok, 1024), BF16),
                   jax.ShapeDtypeStruct((n_tok, 1024), BF16),
                   jax.ShapeDtypeStruct((n_tok, 1024), BF16),
                   jax.ShapeDtypeStruct((n_tok, 512), BF16)],
        compiler_params=pltpu.CompilerParams(dimension_semantics=("arbitrary",),
                                             vmem_limit_bytes=VMEM_LIMIT),
        name="inproj",
    )(h2, nw, w1, qnw, wuq, kvnw, wukv, cos_t, sin_t)


def _fourier_kernel(x_ref, cs_ref, wbd_ref, mats_ref, o_ref, pq_ref, xs_ref, *, half):
    p = pl.program_id(0)
    b = pl.program_id(1)
    kp = xs_ref.shape[0]

    @pl.when(b == 0)
    def _():
        w = wbd_ref[...]
        pq_ref[:, 0:256] = jnp.dot(cs_ref[0], w, preferred_element_type=F32,
                                   precision=lax.Precision.HIGHEST).astype(BF16)
        pq_ref[:, 256:512] = jnp.dot(cs_ref[1], w, preferred_element_type=F32,
                                     precision=lax.Precision.HIGHEST).astype(BF16)
        xs_ref[half:kp, :] = jnp.zeros((kp - half, xs_ref.shape[1]), F32)

    sign = (1 - 2 * p).astype(F32)
    xs_ref[0:half, :] = x_ref[0, 0:half, :] + sign * x_ref[0, half:2 * half, :]
    ab = _dot(xs_ref[...].astype(BF16), pq_ref[...])
    y = _dot(mats_ref[0, 0], ab[:, 0:256].astype(BF16)) + _dot(mats_ref[0, 1], ab[:, 256:512].astype(BF16))
    o_ref[0] = y[0:half].astype(BF16)


def _fourier(fp3, cs, wbd, mats):
    bsz, seq_len, _ = fp3.shape
    half = seq_len // 2
    hp, kp = mats.shape[2], mats.shape[3]
    return pl.pallas_call(
        functools.partial(_fourier_kernel, half=half),
        grid=(2, bsz),
        in_specs=[pl.BlockSpec((1, seq_len, 256), lambda p, b: (b, 0, 0)),
                  pl.BlockSpec(cs.shape, lambda p, b: (0, 0, 0)),
                  pl.BlockSpec(wbd.shape, lambda p, b: (0, 0)),
                  pl.BlockSpec((1, 2, hp, kp), lambda p, b: (p, 0, 0, 0), pipeline_mode=pl.Buffered(1))],
        out_specs=pl.BlockSpec((1, half, 256), lambda p, b: (b, 0, p)),
        out_shape=jax.ShapeDtypeStruct((bsz, half, 512), BF16),
        scratch_shapes=[pltpu.VMEM((256, 512), BF16), pltpu.VMEM((kp, 256), F32)],
        compiler_params=pltpu.CompilerParams(dimension_semantics=("arbitrary", "arbitrary"),
                                             vmem_limit_bytes=VMEM_LIMIT),
        name="fourier",
    )(fp3, cs, wbd, mats)


def _pool_kernel(x_ref, icnt_ref, wbd_ref, ps_ref, o_ref, xp_ref, s2_ref, s4_ref, s8_ref, *, seq_len):
    pad = POOL_PAD
    tot = seq_len + 2 * pad
    zeros = jnp.zeros((pad, 256), F32)
    xp_ref[0:pad, :] = zeros
    xp_ref[pad + seq_len:tot, :] = zeros
    xp_ref[pad:pad + seq_len, :] = x_ref[0]

    def level(src, dst, lo, hi, d_lo, d_hi):
        chunk = 512
        for c0 in range(lo, hi, chunk):
            c1 = min(c0 + chunk, hi)
            dst[c0:c1, :] = src[c0 + d_lo:c1 + d_lo, :] + src[c0 + d_hi:c1 + d_hi, :]

    level(xp_ref, s2_ref, 8, tot - 8, -1, 0)
    level(s2_ref, s4_ref, 16, tot - 16, -1, 1)
    level(s4_ref, s8_ref, 24, tot - 24, -2, 2)

    chunk = 512
    for c0 in range(0, seq_len, chunk):
        c1 = min(c0 + chunk, seq_len)
        a0, a1 = pad + c0, pad + c1
        s16 = s8_ref[a0 - 4:a1 - 4, :] + s8_ref[a0 + 4:a1 + 4, :]
        lane = lax.broadcasted_iota(jnp.int32, s16.shape, 1)
        hi_sel = jnp.where(lane < 192, s8_ref[a0:a1, :], s16)
        lo_sel = jnp.where(lane < 64, s2_ref[a0:a1, :], s4_ref[a0:a1, :])
        win = jnp.where(lane < 128, lo_sel, hi_sel)
        pooled = (win * icnt_ref[c0:c1, :] - xp_ref[a0:a1, :]).astype(BF16)
        o_ref[0, c0:c1, :] = (_dot(pooled, wbd_ref[...]) * ps_ref[...]).astype(BF16)


def _pool(fp3, icnt, wbd, ps):
    bsz, seq_len, _ = fp3.shape
    tot = seq_len + 2 * POOL_PAD
    return pl.pallas_call(
        functools.partial(_pool_kernel, seq_len=seq_len),
        grid=(bsz,),
        in_specs=[pl.BlockSpec((1, seq_len, 256), lambda b: (b, 0, 1)),
                  pl.BlockSpec(icnt.shape, lambda b: (0, 0)),
                  pl.BlockSpec(wbd.shape, lambda b: (0, 0)),
                  pl.BlockSpec(ps.shape, lambda b: (0, 0))],
        out_specs=pl.BlockSpec((1, seq_len, 256), lambda b: (b, 0, 0)),
        out_shape=jax.ShapeDtypeStruct((bsz, seq_len, 256), BF16),
        scratch_shapes=[pltpu.VMEM((tot, 256), F32)] * 4,
        compiler_params=pltpu.CompilerParams(dimension_semantics=("arbitrary",),
                                             vmem_limit_bytes=VMEM_LIMIT),
        name="pool",
    )(fp3, icnt, wbd, ps)


def _kv_chunks(lp, tk):
    n = max(lp // tk, 1)
    return [(i * tk, tk if i < n - 1 else lp - i * tk) for i in range(n)]


def _attn_kernel(q_ref, k_ref, v_ref, o_ref, kp_ref, vp_ref, *, seq_len, tq, tk):
    lp = kp_ref.shape[0]
    pad = lp - seq_len
    kp_ref[0:seq_len, :] = k_ref[0]
    vp_ref[0:seq_len, 0:V_HEAD_DIM] = v_ref[0]
    vp_ref[0:seq_len, V_HEAD_DIM:] = jnp.ones((seq_len, V_HEAD_DIM), BF16)
    if pad:
        kp_ref[seq_len:lp, :] = jnp.zeros((pad, kp_ref.shape[1]), BF16)
        vp_ref[seq_len:lp, :] = jnp.zeros((pad, vp_ref.shape[1]), BF16)
    chunks = _kv_chunks(lp, tk)

    def attend(qs):
        ms = [None] * len(qs)
        accs = [None] * len(qs)
        for c0, cw in chunks:
            ss = [_dot_nt(q, kp_ref[c0:c0 + cw, :]) for q in qs]
            for j, s in enumerate(ss):
                if c0 + cw > seq_len:
                    col = lax.broadcasted_iota(jnp.int32, (s.shape[0], LANES), 1) + (c0 + cw - LANES)
                    last = jnp.where(col < seq_len, s[:, cw - LANES:], -jnp.inf)
                    s = last if cw == LANES else jnp.concatenate([s[:, :cw - LANES], last], axis=-1)
                m_c = jnp.max(s, axis=-1, keepdims=True)
                m_new = m_c if ms[j] is None else jnp.maximum(ms[j], m_c)
                pv = _dot(jnp.exp2(s - m_new).astype(BF16), vp_ref[c0:c0 + cw, :])
                accs[j] = pv if ms[j] is None else jnp.exp2(ms[j] - m_new) * accs[j] + pv
                ms[j] = m_new
        return [(a[:, :V_HEAD_DIM] * pl.reciprocal(a[:, V_HEAD_DIM:])).astype(BF16) for a in accs]

    def q_body(i, _):
        r0 = pl.multiple_of(jnp.minimum(i * tq, seq_len - tq), 16)
        rows = (r0, r0 + tq // 2)
        for r, o in zip(rows, attend([q_ref[0, pl.ds(r, tq // 2), :] for r in rows])):
            o_ref[0, pl.ds(r, tq // 2), :] = o
        return 0

    lax.fori_loop(0, pl.cdiv(seq_len, tq), q_body, 0)


def _attention(q3, k3, v3, *, tq, tk):
    bsz, seq_len, _ = q3.shape
    lp = _round_up(seq_len, LANES)
    return pl.pallas_call(
        functools.partial(_attn_kernel, seq_len=seq_len, tq=tq, tk=tk),
        grid=(bsz, MLA_HEADS),
        in_specs=[pl.BlockSpec((1, seq_len, 256), lambda b, h: (b, 0, h)),
                  pl.BlockSpec((1, seq_len, 256), lambda b, h: (b, 0, h)),
                  pl.BlockSpec((1, seq_len, V_HEAD_DIM), lambda b, h: (b, 0, h))],
        out_specs=pl.BlockSpec((1, seq_len, V_HEAD_DIM), lambda b, h: (b, 0, h)),
        out_shape=jax.ShapeDtypeStruct((bsz, seq_len, MLA_HEADS * V_HEAD_DIM), BF16),
        scratch_shapes=[pltpu.VMEM((lp, 256), BF16), pltpu.VMEM((lp, 2 * V_HEAD_DIM), BF16)],
        compiler_params=pltpu.CompilerParams(dimension_semantics=("arbitrary", "arbitrary"),
                                             vmem_limit_bytes=VMEM_LIMIT),
        name="attention",
    )(q3, k3, v3)


def _outproj_kernel(yf_ref, yp_ref, ya_ref, sg_ref, h_ref, w_ref, fw_ref, o_ref, *, final):
    y = jnp.concatenate([yf_ref[...], yp_ref[...], ya_ref[...]], axis=-1)
    acc = h_ref[...] + _dot(y * sg_ref[...], w_ref[...])
    o_ref[...] = _rms(acc, fw_ref[...]) if final else acc


def _outproj(yf, yp, ya, sg, h2, w, fw, *, tm, final):
    n_tok, d = h2.shape
    row = lambda width: pl.BlockSpec((tm, width), lambda t: (t, 0))
    return pl.pallas_call(
        functools.partial(_outproj_kernel, final=final),
        grid=(n_tok // tm,),
        in_specs=[row(256), row(256), row(512), row(1024), row(d),
                  pl.BlockSpec(w.shape, lambda t: (0, 0)), pl.BlockSpec(fw.shape, lambda t: (0, 0))],
        out_specs=row(d),
        out_shape=jax.ShapeDtypeStruct((n_tok, d), F32),
        input_output_aliases={4: 0},
        compiler_params=pltpu.CompilerParams(dimension_semantics=("arbitrary",),
                                             vmem_limit_bytes=VMEM_LIMIT),
        name="outproj",
    )(yf, yp, ya, sg, h2, w, fw)


def _rope_tables(seq_len, rows):
    inv = 1.0 / (ROPE_THETA ** (jnp.arange(0, QK_ROPE_DIM, 2, dtype=F32) / QK_ROPE_DIM))
    pos = (jnp.arange(rows) % seq_len).astype(F32)
    ang = pos[:, None] * inv[None, :]
    cos, sin = jnp.cos(ang), jnp.sin(ang)
    return jnp.tile(cos, (1, 4)), jnp.tile(jnp.concatenate([-sin, sin], axis=-1), (1, 2))


def _dft_tables(seq_len):
    half = seq_len // 2
    hp, kp = _round_up(half, 16), _round_up(half, LANES)
    k = jnp.arange(hp, dtype=jnp.int32)[:, None]
    l = jnp.arange(kp, dtype=jnp.int32)[None, :]
    valid = (k < half) & (l < half)
    mats = []
    for parity in (0, 1):
        ang = ((2 * k + parity) * l % seq_len).astype(F32) * (2.0 * math.pi / seq_len)
        mats.append(jnp.stack([jnp.where(valid, jnp.cos(ang), 0.0), jnp.where(valid, -jnp.sin(ang), 0.0)]))
    return jnp.stack(mats).astype(BF16)


def _channel_dft(seq_len):
    c = np.arange(FOURIER_HEAD_DIM)
    ang = 2.0 * np.pi * ((c[:, None] * c[None, :]) % FOURIER_HEAD_DIM) / FOURIER_HEAD_DIM
    norm = 1.0 / math.sqrt(seq_len * FOURIER_HEAD_DIM)
    eye = np.eye(FOURIER_HEADS)
    return jnp.asarray(np.stack([np.kron(eye, np.cos(ang) * norm), np.kron(eye, np.sin(ang) * norm)]), F32)


def _block_diag(w):
    g, c, d = w.shape
    out = jnp.zeros((g * c, g * d), w.dtype)
    for i in range(g):
        out = out.at[i * c:(i + 1) * c, i * d:(i + 1) * d].set(w[i])
    return out


def _pool_inv_count(seq_len):
    idx = np.arange(seq_len)
    cols = []
    for w in POOL_WINDOWS:
        cnt = np.clip(idx + w // 2, 0, seq_len) - np.clip(idx - w // 2, 0, seq_len)
        cols.append(np.repeat((1.0 / cnt)[:, None], POOL_GROUP_DIM, axis=1))
    return jnp.asarray(np.concatenate(cols, axis=1), F32)


def _layer_weights(w_in, w_uq, w_ukv):
    sizes = (256, 256, 256, 256, w_uq.shape[0], w_ukv.shape[0], QK_ROPE_DIM, 512)
    offs = np.concatenate([[0], np.cumsum(sizes)])
    f_in, f_gate, p_in, p_gate, c_q, c_kv, k_r, a_gate = (w_in[:, offs[i]:offs[i + 1]] for i in range(8))
    half = QK_ROPE_DIM // 2
    swap = lambda a: jnp.concatenate([a[..., half:], a[..., :half]], axis=-1)
    w1 = jnp.concatenate([f_in, p_in, f_gate, p_gate, a_gate, c_q, c_kv,
                          k_r, k_r, swap(k_r), swap(k_r)], axis=1).astype(BF16)
    uq = w_uq.reshape(w_uq.shape[0], MLA_HEADS, QK_NOPE_DIM + QK_ROPE_DIM)
    q_rope = uq[:, :, QK_NOPE_DIM:]
    wuq = jnp.concatenate([uq[:, :, :QK_NOPE_DIM].reshape(w_uq.shape[0], -1),
                           q_rope.reshape(w_uq.shape[0], -1),
                           swap(q_rope).reshape(w_uq.shape[0], -1)], axis=1).astype(BF16)
    ukv = w_ukv.reshape(w_ukv.shape[0], MLA_HEADS, QK_NOPE_DIM + V_HEAD_DIM)
    wukv = jnp.concatenate([ukv[:, :, :QK_NOPE_DIM].reshape(w_ukv.shape[0], -1),
                            ukv[:, :, QK_NOPE_DIM:].reshape(w_ukv.shape[0], -1)], axis=1).astype(BF16)
    return w1, wuq, wukv


def kernel(x, meta_tokens, norm_w, w_in, fourier_w, pool_w, pool_scale, q_norm_w, w_uq, kv_norm_w, w_ukv,
           w_out, final_norm_w):
    bsz, seq, d = x.shape
    depth = norm_w.shape[0]
    seq_len = seq + N_META
    n_tok = bsz * seq_len
    tm = 256
    assert n_tok % tm == 0 and seq_len % 16 == 0 and (seq_len // 2) % 8 == 0

    h = jnp.concatenate([jnp.broadcast_to(meta_tokens[None].astype(x.dtype), (bsz, N_META, d)), x], axis=1)
    h2 = h.reshape(n_tok, d)

    cos_t, sin_t = _rope_tables(seq_len, seq_len + tm)
    mats = _dft_tables(seq_len)
    cs = _channel_dft(seq_len)
    icnt = _pool_inv_count(seq_len)
    fw = final_norm_w.reshape(1, d)

    for l in range(depth):
        w1, wuq, wukv = _layer_weights(w_in[l], w_uq[l], w_ukv[l])
        fp, sg, q, k, v = _inproj(h2, norm_w[l].reshape(1, d), w1, q_norm_w[l].reshape(1, -1), wuq,
                                  kv_norm_w[l].reshape(1, -1), wukv, cos_t, sin_t, tm=tm, seq_len=seq_len)
        fp3 = fp.reshape(bsz, seq_len, 512)
        yf = _fourier(fp3, cs, _block_diag(fourier_w[l]), mats).reshape(n_tok, 256)
        yp = _pool(fp3, icnt, _block_diag(pool_w[l]).astype(BF16), pool_scale[l].reshape(1, -1)).reshape(n_tok, 256)
        ya = _attention(q.reshape(bsz, seq_len, 1024), k.reshape(bsz, seq_len, 1024),
                        v.reshape(bsz, seq_len, 512), tq=ATTN_TQ, tk=ATTN_TK).reshape(n_tok, 512)
        h2 = _outproj(yf, yp, ya, sg, h2, w_out[l].astype(BF16), fw, tm=tm, final=(l == depth - 1))

    return h2.reshape(bsz, seq_len, d)[:, N_META:]
```

```python
import functools
import math

import numpy as np
import jax
import jax.numpy as jnp
from jax import lax
from jax.experimental import pallas as pl
from jax.experimental.pallas import tpu as pltpu

N_META = 16
FOURIER_HEADS = 4
FOURIER_HEAD_DIM = 64
POOL_WINDOWS = (2, 4, 8, 16)
POOL_GROUP_DIM = 64
MLA_HEADS = 4
QK_NOPE_DIM = 128
QK_ROPE_DIM = 64
V_HEAD_DIM = 128
ROPE_THETA = 10000.0
NORM_EPS = 1e-6

LANES = 128
POOL_PAD = 32
VMEM_LIMIT = 56 * 1024 * 1024
ATTN_TQ = 1056
ATTN_TK = 1024

F32 = jnp.float32
BF16 = jnp.bfloat16


def _round_up(a, m):
    return (a + m - 1) // m * m


def _dot(a, b):
    return jnp.dot(a, b, preferred_element_type=F32)


def _dot_nt(a, b):
    return lax.dot_general(a, b, (((1,), (1,)), ((), ())), preferred_element_type=F32)


def _rms(x, w):
    return x * lax.rsqrt(jnp.mean(x * x, axis=-1, keepdims=True) + NORM_EPS) * w


def _project_in(x, nw_ref, w1_ref, qnw_ref, wuq_ref, kvnw_ref, wukv_ref, cos_ref, sin_ref,
                fp_ref, sg_ref, q_ref, k_ref, v_ref, *, tm, seq_len, scale, d_q, d_kv):
    t = pl.program_id(0)
    n = _rms(x, nw_ref[...]).astype(BF16)

    fp_ref[...] = _dot(n, w1_ref[:, 0:512])
    g = _dot(n, w1_ref[:, 512:1536])
    sg_ref[...] = (g * pl.reciprocal(1.0 + jnp.exp(-g))).astype(BF16)

    start = pl.multiple_of(lax.rem(t * tm, seq_len), 16)
    cos = cos_ref[pl.ds(start, tm), :]
    sin = sin_ref[pl.ds(start, tm), :]

    o = 1536
    cq = _rms(_dot(n, w1_ref[:, o:o + d_q]), qnw_ref[...]).astype(BF16)
    o += d_q
    ckv = _rms(_dot(n, w1_ref[:, o:o + d_kv]), kvnw_ref[...]).astype(BF16)
    o += d_kv
    kr = _dot(n, w1_ref[:, o:o + 2 * LANES])

    qf = _dot(cq, wuq_ref[...])
    nq = MLA_HEADS * QK_NOPE_DIM
    nr = MLA_HEADS * QK_ROPE_DIM
    for h in range(MLA_HEADS):
        q_ref[:, 256 * h:256 * h + 128] = (qf[:, 128 * h:128 * h + 128] * scale).astype(BF16)
    for pair in range(MLA_HEADS // 2):
        r = qf[:, nq + 128 * pair:nq + 128 * pair + 128]
        rs = qf[:, nq + nr + 128 * pair:nq + nr + 128 * pair + 128]
        roped = ((r * cos + rs * sin) * scale).astype(BF16)
        for h in (2 * pair, 2 * pair + 1):
            q_ref[:, 256 * h + 128:256 * h + 256] = roped

    kvf = _dot(ckv, wukv_ref[...])
    v_ref[...] = kvf[:, nq:].astype(BF16)
    kroped = kr[:, :LANES] * cos + kr[:, LANES:] * sin
    lane = lax.broadcasted_iota(jnp.int32, kroped.shape, 1)
    k_lo = jnp.where(lane < QK_ROPE_DIM, kroped, 0.0).astype(BF16)
    k_hi = jnp.where(lane >= QK_ROPE_DIM, kroped, 0.0).astype(BF16)
    for h in range(MLA_HEADS):
        k_ref[:, 256 * h:256 * h + 128] = kvf[:, 128 * h:128 * h + 128].astype(BF16)
        k_ref[:, 256 * h + 128:256 * h + 256] = k_lo if h % 2 == 0 else k_hi


def _mix_residual(yf_ref, yp_ref, ya_ref, sg_ref, h_ref, w_ref):
    y = jnp.concatenate([yf_ref[...], yp_ref[...], ya_ref[...]], axis=-1)
    return h_ref[...] + _dot(y * sg_ref[...], w_ref[...])


def _inproj_kernel(x_ref, *refs, **kw):
    _project_in(x_ref[...], *refs, **kw)


def _fused_kernel(yf_ref, yp_ref, ya_ref, sg_ref, h_ref, w_ref, *refs, **kw):
    h_new = _mix_residual(yf_ref, yp_ref, ya_ref, sg_ref, h_ref, w_ref)
    refs[8][...] = h_new
    _project_in(h_new, *refs[:8], *refs[9:], **kw)


def _final_kernel(yf_ref, yp_ref, ya_ref, sg_ref, h_ref, w_ref, fw_ref, o_ref):
    o_ref[0] = _rms(_mix_residual(yf_ref, yp_ref, ya_ref, sg_ref, h_ref, w_ref), fw_ref[...])


_MIX_WIDTHS = (256, 256, 512, 1024)
_IN_OUT = ((512, F32), (1024, BF16), (1024, BF16), (1024, BF16), (512, BF16))


def _token_call(mix, h2, w_out, in_w, *, tm, seq_len):
    n_tok, d = h2.shape
    nw, w1, qnw, wuq, kvnw, wukv, cos_t, sin_t = in_w
    scale = float((QK_NOPE_DIM + QK_ROPE_DIM) ** -0.5 * math.log2(math.e))
    const = lambda a: pl.BlockSpec(a.shape, lambda t: (0,) * a.ndim)
    row = lambda w: pl.BlockSpec((tm, w), lambda t: (t, 0))
    kw = dict(tm=tm, seq_len=seq_len, scale=scale, d_q=qnw.shape[1], d_kv=kvnw.shape[1])
    in_specs = [const(a) for a in in_w]
    out_specs = [row(w) for w, _ in _IN_OUT]
    out_shape = [jax.ShapeDtypeStruct((n_tok, w), dt) for w, dt in _IN_OUT]
    if mix is None:
        body, args, aliases = _inproj_kernel, (h2, *in_w), {}
        in_specs = [row(d)] + in_specs
    else:
        body, args, aliases = _fused_kernel, (*mix, h2, w_out, *in_w), {4: 0}
        in_specs = [row(w) for w in _MIX_WIDTHS] + [row(d), const(w_out)] + in_specs
        out_specs = [row(d)] + out_specs
        out_shape = [jax.ShapeDtypeStruct((n_tok, d), F32)] + out_shape
    return pl.pallas_call(
        functools.partial(body, **kw),
        grid=(n_tok // tm,),
        in_specs=in_specs, out_specs=out_specs, out_shape=out_shape,
        input_output_aliases=aliases,
        compiler_params=pltpu.CompilerParams(dimension_semantics=("arbitrary",),
                                             vmem_limit_bytes=VMEM_LIMIT),
        name="inproj" if mix is None else "outin",
    )(*args)


def _final_call(mix, h2, w_out, fw, *, bsz, seq_len, tm):
    n_tok, d = h2.shape
    seq = seq_len - N_META
    row = lambda w: pl.BlockSpec((pl.Element(tm), pl.Element(w)),
                                 lambda b, j: (pl.multiple_of(b * seq_len + N_META + j * tm, 16), 0))
    const = lambda a: pl.BlockSpec(a.shape, lambda b, j: (0,) * a.ndim)
    return pl.pallas_call(
        _final_kernel,
        grid=(bsz, seq // tm),
        in_specs=[row(w) for w in _MIX_WIDTHS] + [row(d), const(w_out), const(fw)],
        out_specs=pl.BlockSpec((1, tm, d), lambda b, j: (b, j, 0)),
        out_shape=jax.ShapeDtypeStruct((bsz, seq, d), F32),
        compiler_params=pltpu.CompilerParams(dimension_semantics=("arbitrary", "arbitrary"),
                                             vmem_limit_bytes=VMEM_LIMIT),
        name="final",
    )(*mix, h2, w_out, fw)


def _fourier_kernel(x_ref, cs_ref, wbd_ref, mats_ref, o_ref, pq_ref, xs_ref, *, half):
    p = pl.program_id(0)
    b = pl.program_id(1)
    kp = xs_ref.shape[0]

    @pl.when(b == 0)
    def _():
        w = wbd_ref[...]
        pq_ref[:, 0:256] = jnp.dot(cs_ref[0], w, preferred_element_type=F32,
                                   precision=lax.Precision.HIGHEST).astype(BF16)
        pq_ref[:, 256:512] = jnp.dot(cs_ref[1], w, preferred_element_type=F32,
                                     precision=lax.Precision.HIGHEST).astype(BF16)
        xs_ref[half:kp, :] = jnp.zeros((kp - half, xs_ref.shape[1]), F32)

    sign = (1 - 2 * p).astype(F32)
    xs_ref[0:half, :] = x_ref[0, 0:half, :] + sign * x_ref[0, half:2 * half, :]
    ab = _dot(xs_ref[...].astype(BF16), pq_ref[...])
    y = _dot(mats_ref[0, 0], ab[:, 0:256].astype(BF16)) + _dot(mats_ref[0, 1], ab[:, 256:512].astype(BF16))
    o_ref[0] = y[0:half].astype(BF16)


def _fourier(fp3, cs, wbd, mats):
    bsz, seq_len, _ = fp3.shape
    half = seq_len // 2
    hp, kp = mats.shape[2], mats.shape[3]
    return pl.pallas_call(
        functools.partial(_fourier_kernel, half=half),
        grid=(2, bsz),
        in_specs=[pl.BlockSpec((1, seq_len, 256), lambda p, b: (b, 0, 0)),
                  pl.BlockSpec(cs.shape, lambda p, b: (0, 0, 0)),
                  pl.BlockSpec(wbd.shape, lambda p, b: (0, 0)),
                  pl.BlockSpec((1, 2, hp, kp), lambda p, b: (p, 0, 0, 0), pipeline_mode=pl.Buffered(1))],
        out_specs=pl.BlockSpec((1, half, 256), lambda p, b: (b, 0, p)),
        out_shape=jax.ShapeDtypeStruct((bsz, half, 512), BF16),
        scratch_shapes=[pltpu.VMEM((256, 512), BF16), pltpu.VMEM((kp, 256), F32)],
        compiler_params=pltpu.CompilerParams(dimension_semantics=("arbitrary", "arbitrary"),
                                             vmem_limit_bytes=VMEM_LIMIT),
        name="fourier",
    )(fp3, cs, wbd, mats)


def _pool_kernel(x_ref, icnt_ref, wbd_ref, ps_ref, o_ref, xp_ref, s2_ref, s4_ref, s8_ref, *, seq_len):
    pad = POOL_PAD
    tot = seq_len + 2 * pad
    zeros = jnp.zeros((pad, 256), F32)
    xp_ref[0:pad, :] = zeros
    xp_ref[pad + seq_len:tot, :] = zeros
    xp_ref[pad:pad + seq_len, :] = x_ref[0]

    def level(src, dst, lo, hi, d_lo, d_hi):
        chunk = 512
        for c0 in range(lo, hi, chunk):
            c1 = min(c0 + chunk, hi)
            dst[c0:c1, :] = src[c0 + d_lo:c1 + d_lo, :] + src[c0 + d_hi:c1 + d_hi, :]

    level(xp_ref, s2_ref, 8, tot - 8, -1, 0)
    level(s2_ref, s4_ref, 16, tot - 16, -1, 1)
    level(s4_ref, s8_ref, 24, tot - 24, -2, 2)

    chunk = 512
    for c0 in range(0, seq_len, chunk):
        c1 = min(c0 + chunk, seq_len)
        a0, a1 = pad + c0, pad + c1
        s16 = s8_ref[a0 - 4:a1 - 4, :] + s8_ref[a0 + 4:a1 + 4, :]
        lane = lax.broadcasted_iota(jnp.int32, s16.shape, 1)
        hi_sel = jnp.where(lane < 192, s8_ref[a0:a1, :], s16)
        lo_sel = jnp.where(lane < 64, s2_ref[a0:a1, :], s4_ref[a0:a1, :])
        win = jnp.where(lane < 128, lo_sel, hi_sel)
        pooled = (win * icnt_ref[c0:c1, :] - xp_ref[a0:a1, :]).astype(BF16)
        o_ref[0, c0:c1, :] = (_dot(pooled, wbd_ref[...]) * ps_ref[...]).astype(BF16)


def _pool(fp3, icnt, wbd, ps):
    bsz, seq_len, _ = fp3.shape
    tot = seq_len + 2 * POOL_PAD
    return pl.pallas_call(
        functools.partial(_pool_kernel, seq_len=seq_len),
        grid=(bsz,),
        in_specs=[pl.BlockSpec((1, seq_len, 256), lambda b: (b, 0, 1)),
                  pl.BlockSpec(icnt.shape, lambda b: (0, 0)),
                  pl.BlockSpec(wbd.shape, lambda b: (0, 0)),
                  pl.BlockSpec(ps.shape, lambda b: (0, 0))],
        out_specs=pl.BlockSpec((1, seq_len, 256), lambda b: (b, 0, 0)),
        out_shape=jax.ShapeDtypeStruct((bsz, seq_len, 256), BF16),
        scratch_shapes=[pltpu.VMEM((tot, 256), F32)] * 4,
        compiler_params=pltpu.CompilerParams(dimension_semantics=("arbitrary",),
                                             vmem_limit_bytes=VMEM_LIMIT),
        name="pool",
    )(fp3, icnt, wbd, ps)


def _kv_chunks(lp, tk):
    n = max(lp // tk, 1)
    return [(i * tk, tk if i < n - 1 else lp - i * tk) for i in range(n)]


def _attn_kernel(q_ref, k_ref, v_ref, o_ref, kp_ref, vp_ref, *, seq_len, tq, tk):
    lp = kp_ref.shape[0]
    pad = lp - seq_len
    kp_ref[0:seq_len, :] = k_ref[0]
    vp_ref[0:seq_len, 0:V_HEAD_DIM] = v_ref[0]
    vp_ref[0:seq_len, V_HEAD_DIM:] = jnp.ones((seq_len, V_HEAD_DIM), BF16)
    if pad:
        kp_ref[seq_len:lp, :] = jnp.zeros((pad, kp_ref.shape[1]), BF16)
        vp_ref[seq_len:lp, :] = jnp.zeros((pad, vp_ref.shape[1]), BF16)
    chunks = _kv_chunks(lp, tk)

    def attend(qs):
        ms = [None] * len(qs)
        accs = [None] * len(qs)
        for c0, cw in chunks:
            ss = [_dot_nt(q, kp_ref[c0:c0 + cw, :]) for q in qs]
            for j, s in enumerate(ss):
                if c0 + cw > seq_len:
                    col = lax.broadcasted_iota(jnp.int32, (s.shape[0], LANES), 1) + (c0 + cw - LANES)
                    last = jnp.where(col < seq_len, s[:, cw - LANES:], -jnp.inf)
                    s = last if cw == LANES else jnp.concatenate([s[:, :cw - LANES], last], axis=-1)
                m_c = jnp.max(s, axis=-1, keepdims=True)
                m_new = m_c if ms[j] is None else jnp.maximum(ms[j], m_c)
                pv = _dot(jnp.exp2(s - m_new).astype(BF16), vp_ref[c0:c0 + cw, :])
                accs[j] = pv if ms[j] is None else jnp.exp2(ms[j] - m_new) * accs[j] + pv
                ms[j] = m_new
        return [(a[:, :V_HEAD_DIM] * pl.reciprocal(a[:, V_HEAD_DIM:])).astype(BF16) for a in accs]

    def q_body(i, _):
        r0 = pl.multiple_of(jnp.minimum(i * tq, seq_len - tq), 16)
        rows = (r0, r0 + tq // 2)
        for r, o in zip(rows, attend([q_ref[0, pl.ds(r, tq // 2), :] for r in rows])):
            o_ref[0, pl.ds(r, tq // 2), :] = o
        return 0

    lax.fori_loop(0, pl.cdiv(seq_len, tq), q_body, 0)


def _attention(q3, k3, v3, *, tq, tk):
    bsz, seq_len, _ = q3.shape
    assert tq <= seq_len and tq % 32 == 0
    lp = _round_up(seq_len, LANES)
    return pl.pallas_call(
        functools.partial(_attn_kernel, seq_len=seq_len, tq=tq, tk=tk),
        grid=(bsz, MLA_HEADS),
        in_specs=[pl.BlockSpec((1, seq_len, 256), lambda b, h: (b, 0, h)),
                  pl.BlockSpec((1, seq_len, 256), lambda b, h: (b, 0, h)),
                  pl.BlockSpec((1, seq_len, V_HEAD_DIM), lambda b, h: (b, 0, h))],
        out_specs=pl.BlockSpec((1, seq_len, V_HEAD_DIM), lambda b, h: (b, 0, h)),
        out_shape=jax.ShapeDtypeStruct((bsz, seq_len, MLA_HEADS * V_HEAD_DIM), BF16),
        scratch_shapes=[pltpu.VMEM((lp, 256), BF16), pltpu.VMEM((lp, 2 * V_HEAD_DIM), BF16)],
        compiler_params=pltpu.CompilerParams(dimension_semantics=("arbitrary", "arbitrary"),
                                             vmem_limit_bytes=VMEM_LIMIT),
        name="attention",
    )(q3, k3, v3)


def _rope_tables(seq_len, rows):
    inv = 1.0 / (ROPE_THETA ** (jnp.arange(0, QK_ROPE_DIM, 2, dtype=F32) / QK_ROPE_DIM))
    pos = (jnp.arange(rows) % seq_len).astype(F32)
    ang = pos[:, None] * inv[None, :]
    cos, sin = jnp.cos(ang), jnp.sin(ang)
    return jnp.tile(cos, (1, 4)), jnp.tile(jnp.concatenate([-sin, sin], axis=-1), (1, 2))


def _dft_tables(seq_len):
    half = seq_len // 2
    hp, kp = _round_up(half, 16), _round_up(half, LANES)
    step = 64
    k = jnp.arange(hp, dtype=jnp.int32)[:, None]
    a = step * jnp.arange(kp // step, dtype=jnp.int32)[None, :]
    b = jnp.arange(step, dtype=jnp.int32)[None, :]
    valid = ((k < half)[:, :, None] & ((a[:, :, None] + b[:, None, :]) < half)).reshape(hp, kp)
    unit = 2.0 * math.pi / seq_len
    mats = []
    for parity in (0, 1):
        row = 2 * k + parity
        ang_a = (row * a % seq_len).astype(F32) * unit
        ang_b = (row * b % seq_len).astype(F32) * unit
        ca, sa = jnp.cos(ang_a)[:, :, None], jnp.sin(ang_a)[:, :, None]
        cb, sb = jnp.cos(ang_b)[:, None, :], jnp.sin(ang_b)[:, None, :]
        cos = (ca * cb - sa * sb).reshape(hp, kp)
        msin = -(sa * cb + ca * sb).reshape(hp, kp)
        mats.append(jnp.stack([jnp.where(valid, cos, 0.0), jnp.where(valid, msin, 0.0)]))
    return jnp.stack(mats).astype(BF16)


def _channel_dft(seq_len):
    c = np.arange(FOURIER_HEAD_DIM)
    ang = 2.0 * np.pi * ((c[:, None] * c[None, :]) % FOURIER_HEAD_DIM) / FOURIER_HEAD_DIM
    norm = 1.0 / math.sqrt(seq_len * FOURIER_HEAD_DIM)
    eye = np.eye(FOURIER_HEADS)
    return jnp.asarray(np.stack([np.kron(eye, np.cos(ang) * norm), np.kron(eye, np.sin(ang) * norm)]), F32)


def _block_diag(w):
    g, c, d = w.shape
    out = jnp.zeros((g * c, g * d), w.dtype)
    for i in range(g):
        out = out.at[i * c:(i + 1) * c, i * d:(i + 1) * d].set(w[i])
    return out


def _pool_inv_count(seq_len):
    idx = np.arange(seq_len)
    cols = []
    for w in POOL_WINDOWS:
        cnt = np.clip(idx + w // 2, 0, seq_len) - np.clip(idx - w // 2, 0, seq_len)
        cols.append(np.repeat((1.0 / cnt)[:, None], POOL_GROUP_DIM, axis=1))
    return jnp.asarray(np.concatenate(cols, axis=1), F32)


def _layer_weights(w_in, w_uq, w_ukv):
    sizes = (256, 256, 256, 256, w_uq.shape[0], w_ukv.shape[0], QK_ROPE_DIM, 512)
    offs = np.concatenate([[0], np.cumsum(sizes)])
    f_in, f_gate, p_in, p_gate, c_q, c_kv, k_r, a_gate = (w_in[:, offs[i]:offs[i + 1]] for i in range(8))
    half = QK_ROPE_DIM // 2
    swap = lambda a: jnp.concatenate([a[..., half:], a[..., :half]], axis=-1)
    w1 = jnp.concatenate([f_in, p_in, f_gate, p_gate, a_gate, c_q, c_kv,
                          k_r, k_r, swap(k_r), swap(k_r)], axis=1).astype(BF16)
    uq = w_uq.reshape(w_uq.shape[0], MLA_HEADS, QK_NOPE_DIM + QK_ROPE_DIM)
    q_rope = uq[:, :, QK_NOPE_DIM:]
    wuq = jnp.concatenate([uq[:, :, :QK_NOPE_DIM].reshape(w_uq.shape[0], -1),
                           q_rope.reshape(w_uq.shape[0], -1),
                           swap(q_rope).reshape(w_uq.shape[0], -1)], axis=1).astype(BF16)
    ukv = w_ukv.reshape(w_ukv.shape[0], MLA_HEADS, QK_NOPE_DIM + V_HEAD_DIM)
    wukv = jnp.concatenate([ukv[:, :, :QK_NOPE_DIM].reshape(w_ukv.shape[0], -1),
                            ukv[:, :, QK_NOPE_DIM:].reshape(w_ukv.shape[0], -1)], axis=1).astype(BF16)
    return w1, wuq, wukv


def kernel(x, meta_tokens, norm_w, w_in, fourier_w, pool_w, pool_scale, q_norm_w, w_uq, kv_norm_w, w_ukv,
           w_out, final_norm_w):
    bsz, seq, d = x.shape
    depth = norm_w.shape[0]
    seq_len = seq + N_META
    n_tok = bsz * seq_len
    tm = 256
    assert n_tok % tm == 0 and seq_len % 16 == 0 and (seq_len // 2) % 8 == 0

    h = jnp.concatenate([jnp.broadcast_to(meta_tokens[None].astype(x.dtype), (bsz, N_META, d)), x], axis=1)
    h2 = h.reshape(n_tok, d)

    cos_t, sin_t = _rope_tables(seq_len, seq_len + tm)
    mats = _dft_tables(seq_len)
    cs = _channel_dft(seq_len)
    icnt = _pool_inv_count(seq_len)
    fw = final_norm_w.reshape(1, d)

    def in_weights(l):
        w1, wuq, wukv = _layer_weights(w_in[l], w_uq[l], w_ukv[l])
        return (norm_w[l].reshape(1, d), w1, q_norm_w[l].reshape(1, -1), wuq, kv_norm_w[l].reshape(1, -1), wukv,
                cos_t, sin_t)

    fp, sg, q, k, v = _token_call(None, h2, None, in_weights(0), tm=tm, seq_len=seq_len)
    for l in range(depth):
        fp3 = fp.reshape(bsz, seq_len, 512)
        yf = _fourier(fp3, cs, _block_diag(fourier_w[l]), mats).reshape(n_tok, 256)
        yp = _pool(fp3, icnt, _block_diag(pool_w[l]).astype(BF16), pool_scale[l].reshape(1, -1)).reshape(n_tok, 256)
        ya = _attention(q.reshape(bsz, seq_len, 1024), k.reshape(bsz, seq_len, 1024),
                        v.reshape(bsz, seq_len, 512), tq=ATTN_TQ, tk=ATTN_TK).reshape(n_tok, 512)
        mix = (yf, yp, ya, sg)
        if l == depth - 1:
            return _final_call(mix, h2, w_out[l].astype(BF16), fw, bsz=bsz, seq_len=seq_len, tm=tm)
        h2, fp, sg, q, k, v = _token_call(mix, h2, w_out[l].astype(BF16), in_weights(l + 1), tm=tm, seq_len=seq_len)
```

```python
import functools
import math

import numpy as np
import jax
import jax.numpy as jnp
from jax import lax
from jax.experimental import pallas as pl
from jax.experimental.pallas import tpu as pltpu

N_META = 16
FOURIER_HEADS = 4
FOURIER_HEAD_DIM = 64
POOL_WINDOWS = (2, 4, 8, 16)
POOL_GROUP_DIM = 64
MLA_HEADS = 4
QK_NOPE_DIM = 128
QK_ROPE_DIM = 64
V_HEAD_DIM = 128
ROPE_THETA = 10000.0
NORM_EPS = 1e-6

LANES = 128
POOL_PAD = 32
VMEM_LIMIT = 56 * 1024 * 1024
ATTN_TQ = 1056
ATTN_TK = 1024

F32 = jnp.float32
BF16 = jnp.bfloat16


def _round_up(a, m):
    return (a + m - 1) // m * m


def _dot(a, b):
    return jnp.dot(a, b, preferred_element_type=F32)


def _dot_nt(a, b):
    return lax.dot_general(a, b, (((1,), (1,)), ((), ())), preferred_element_type=F32)


def _rms(x, w):
    return x * lax.rsqrt(jnp.mean(x * x, axis=-1, keepdims=True) + NORM_EPS) * w


def _project_in(x, nw_ref, w1_ref, qnw_ref, wuq_ref, kvnw_ref, wukv_ref, cos_ref, sin_ref,
                fp_ref, sg_ref, q_ref, k_ref, v_ref, *, tm, seq_len, scale, d_q, d_kv):
    t = pl.program_id(0)
    n = _rms(x, nw_ref[...]).astype(BF16)

    fp_ref[...] = _dot(n, w1_ref[:, 0:512])
    g = _dot(n, w1_ref[:, 512:1536])
    sg_ref[...] = (g * pl.reciprocal(1.0 + jnp.exp(-g))).astype(BF16)

    start = pl.multiple_of(lax.rem(t * tm, seq_len), 16)
    cos = cos_ref[pl.ds(start, tm), :]
    sin = sin_ref[pl.ds(start, tm), :]

    o = 1536
    cq = _rms(_dot(n, w1_ref[:, o:o + d_q]), qnw_ref[...]).astype(BF16)
    o += d_q
    ckv = _rms(_dot(n, w1_ref[:, o:o + d_kv]), kvnw_ref[...]).astype(BF16)
    o += d_kv
    kr = _dot(n, w1_ref[:, o:o + 2 * LANES])

    qf = _dot(cq, wuq_ref[...])
    nq = MLA_HEADS * QK_NOPE_DIM
    nr = MLA_HEADS * QK_ROPE_DIM
    for h in range(MLA_HEADS):
        q_ref[:, 256 * h:256 * h + 128] = (qf[:, 128 * h:128 * h + 128] * scale).astype(BF16)
    for pair in range(MLA_HEADS // 2):
        r = qf[:, nq + 128 * pair:nq + 128 * pair + 128]
        rs = qf[:, nq + nr + 128 * pair:nq + nr + 128 * pair + 128]
        roped = ((r * cos + rs * sin) * scale).astype(BF16)
        for h in (2 * pair, 2 * pair + 1):
            q_ref[:, 256 * h + 128:256 * h + 256] = roped

    kvf = _dot(ckv, wukv_ref[...])
    v_ref[...] = kvf[:, nq:].astype(BF16)
    kroped = kr[:, :LANES] * cos + kr[:, LANES:] * sin
    lane = lax.broadcasted_iota(jnp.int32, kroped.shape, 1)
    k_lo = jnp.where(lane < QK_ROPE_DIM, kroped, 0.0).astype(BF16)
    k_hi = jnp.where(lane >= QK_ROPE_DIM, kroped, 0.0).astype(BF16)
    for h in range(MLA_HEADS):
        k_ref[:, 256 * h:256 * h + 128] = kvf[:, 128 * h:128 * h + 128].astype(BF16)
        k_ref[:, 256 * h + 128:256 * h + 256] = k_lo if h % 2 == 0 else k_hi


def _mix_residual(yf_ref, yp_ref, ya_ref, sg_ref, h_ref, w_ref):
    y = jnp.concatenate([yf_ref[...], yp_ref[...], ya_ref[...]], axis=-1)
    return h_ref[...] + _dot(y * sg_ref[...], w_ref[...])


def _inproj_kernel(x_ref, *refs, **kw):
    _project_in(x_ref[...], *refs, **kw)


def _fused_kernel(yf_ref, yp_ref, ya_ref, sg_ref, h_ref, w_ref, *refs, **kw):
    h_new = _mix_residual(yf_ref, yp_ref, ya_ref, sg_ref, h_ref, w_ref)
    refs[8][...] = h_new
    _project_in(h_new, *refs[:8], *refs[9:], **kw)


def _final_kernel(yf_ref, yp_ref, ya_ref, sg_ref, h_ref, w_ref, fw_ref, o_ref):
    o_ref[0] = _rms(_mix_residual(yf_ref, yp_ref, ya_ref, sg_ref, h_ref, w_ref), fw_ref[...])


_MIX_WIDTHS = (256, 256, 512, 1024)
_IN_OUT = ((512, F32), (1024, BF16), (1024, BF16), (1024, BF16), (512, BF16))


def _token_call(mix, h2, w_out, in_w, *, tm, seq_len):
    n_tok, d = h2.shape
    nw, w1, qnw, wuq, kvnw, wukv, cos_t, sin_t = in_w
    scale = float((QK_NOPE_DIM + QK_ROPE_DIM) ** -0.5 * math.log2(math.e))
    const = lambda a: pl.BlockSpec(a.shape, lambda t: (0,) * a.ndim)
    row = lambda w: pl.BlockSpec((tm, w), lambda t: (t, 0))
    kw = dict(tm=tm, seq_len=seq_len, scale=scale, d_q=qnw.shape[1], d_kv=kvnw.shape[1])
    in_specs = [const(a) for a in in_w]
    out_specs = [row(w) for w, _ in _IN_OUT]
    out_shape = [jax.ShapeDtypeStruct((n_tok, w), dt) for w, dt in _IN_OUT]
    if mix is None:
        body, args, aliases = _inproj_kernel, (h2, *in_w), {}
        in_specs = [row(d)] + in_specs
    else:
        body, args, aliases = _fused_kernel, (*mix, h2, w_out, *in_w), {4: 0}
        in_specs = [row(w) for w in _MIX_WIDTHS] + [row(d), const(w_out)] + in_specs
        out_specs = [row(d)] + out_specs
        out_shape = [jax.ShapeDtypeStruct((n_tok, d), F32)] + out_shape
    return pl.pallas_call(
        functools.partial(body, **kw),
        grid=(n_tok // tm,),
        in_specs=in_specs, out_specs=out_specs, out_shape=out_shape,
        input_output_aliases=aliases,
        compiler_params=pltpu.CompilerParams(dimension_semantics=("arbitrary",),
                                             vmem_limit_bytes=VMEM_LIMIT),
        name="inproj" if mix is None else "outin",
    )(*args)


def _final_call(mix, h2, w_out, fw, *, bsz, seq_len, tm):
    n_tok, d = h2.shape
    seq = seq_len - N_META
    row = lambda w: pl.BlockSpec((pl.Element(tm), pl.Element(w)),
                                 lambda b, j: (pl.multiple_of(b * seq_len + N_META + j * tm, 16), 0))
    const = lambda a: pl.BlockSpec(a.shape, lambda b, j: (0,) * a.ndim)
    return pl.pallas_call(
        _final_kernel,
        grid=(bsz, seq // tm),
        in_specs=[row(w) for w in _MIX_WIDTHS] + [row(d), const(w_out), const(fw)],
        out_specs=pl.BlockSpec((1, tm, d), lambda b, j: (b, j, 0)),
        out_shape=jax.ShapeDtypeStruct((bsz, seq, d), F32),
        compiler_params=pltpu.CompilerParams(dimension_semantics=("arbitrary", "arbitrary"),
                                             vmem_limit_bytes=VMEM_LIMIT),
        name="final",
    )(*mix, h2, w_out, fw)


def _fourier_kernel(x_ref, cs_ref, wbd_ref, mats_ref, o_ref, pq_ref, u_ref, y_ref, *, quarter):
    b = pl.program_id(0)
    qp = u_ref.shape[1]

    @pl.when(b == 0)
    def _():
        w = wbd_ref[...]
        pq_ref[:, 0:256] = jnp.dot(cs_ref[0], w, preferred_element_type=F32,
                                   precision=lax.Precision.HIGHEST).astype(BF16)
        pq_ref[:, 256:512] = jnp.dot(cs_ref[1], w, preferred_element_type=F32,
                                     precision=lax.Precision.HIGHEST).astype(BF16)
        u_ref[:, quarter:qp, :] = jnp.zeros((4, qp - quarter, u_ref.shape[2]), F32)

    x0, x1, x2, x3 = (x_ref[0, j * quarter:(j + 1) * quarter, :] for j in range(4))
    s02, s13, d02, d13 = x0 + x2, x1 + x3, x0 - x2, x1 - x3
    for j, u in enumerate((s02 + s13, s02 - s13, d02, d13)):
        u_ref[j, 0:quarter, :] = u
    pq = pq_ref[...]
    ab = [_dot(u_ref[j].astype(BF16), pq) for j in range(4)]
    re = (ab[0][:, :256], ab[2][:, :256] - ab[3][:, 256:], ab[1][:, :256], ab[2][:, :256] + ab[3][:, 256:])
    im = (ab[0][:, 256:], ab[2][:, 256:] + ab[3][:, :256], ab[1][:, 256:], ab[2][:, 256:] - ab[3][:, :256])
    for r in range(4):
        rhs = jnp.concatenate([re[r].astype(BF16), im[r].astype(BF16)], axis=0)
        y = _dot(mats_ref[r], rhs)
        for c in range(2):
            y_ref[c, pl.ds(r, quarter, stride=4), :] = y[0:quarter, c * LANES:(c + 1) * LANES]
    for c in range(2):
        o_ref[0, :, c * LANES:(c + 1) * LANES] = y_ref[c].astype(BF16)


def _fourier(fp3, cs, wbd, mats):
    bsz, seq_len, _ = fp3.shape
    quarter = seq_len // 4
    qp = mats.shape[2] // 2
    return pl.pallas_call(
        functools.partial(_fourier_kernel, quarter=quarter),
        grid=(bsz,),
        in_specs=[pl.BlockSpec((1, seq_len, 256), lambda b: (b, 0, 0)),
                  pl.BlockSpec(cs.shape, lambda b: (0, 0, 0)),
                  pl.BlockSpec(wbd.shape, lambda b: (0, 0)),
                  pl.BlockSpec(mats.shape, lambda b: (0, 0, 0), pipeline_mode=pl.Buffered(1))],
        out_specs=pl.BlockSpec((1, seq_len, 256), lambda b: (b, 0, 0)),
        out_shape=jax.ShapeDtypeStruct((bsz, seq_len, 256), BF16),
        scratch_shapes=[pltpu.VMEM((256, 512), BF16), pltpu.VMEM((4, qp, 256), F32),
                        pltpu.VMEM((2, seq_len, LANES), F32)],
        compiler_params=pltpu.CompilerParams(dimension_semantics=("arbitrary",),
                                             vmem_limit_bytes=VMEM_LIMIT),
        name="fourier",
    )(fp3, cs, wbd, mats)


def _pool_kernel(x_ref, icnt_ref, wbd_ref, ps_ref, o_ref, xp_ref, s2_ref, s4_ref, s8_ref, *, seq_len):
    pad = POOL_PAD
    tot = seq_len + 2 * pad
    zeros = jnp.zeros((pad, 256), F32)
    xp_ref[0:pad, :] = zeros
    xp_ref[pad + seq_len:tot, :] = zeros
    xp_ref[pad:pad + seq_len, :] = x_ref[0]

    def level(src, dst, lo, hi, d_lo, d_hi):
        chunk = 512
        for c0 in range(lo, hi, chunk):
            c1 = min(c0 + chunk, hi)
            dst[c0:c1, :] = src[c0 + d_lo:c1 + d_lo, :] + src[c0 + d_hi:c1 + d_hi, :]

    level(xp_ref, s2_ref, 8, tot - 8, -1, 0)
    level(s2_ref, s4_ref, 16, tot - 16, -1, 1)
    level(s4_ref, s8_ref, 24, tot - 24, -2, 2)

    chunk = 512
    for c0 in range(0, seq_len, chunk):
        c1 = min(c0 + chunk, seq_len)
        a0, a1 = pad + c0, pad + c1
        s16 = s8_ref[a0 - 4:a1 - 4, :] + s8_ref[a0 + 4:a1 + 4, :]
        lane = lax.broadcasted_iota(jnp.int32, s16.shape, 1)
        hi_sel = jnp.where(lane < 192, s8_ref[a0:a1, :], s16)
        lo_sel = jnp.where(lane < 64, s2_ref[a0:a1, :], s4_ref[a0:a1, :])
        win = jnp.where(lane < 128, lo_sel, hi_sel)
        pooled = (win * icnt_ref[c0:c1, :] - xp_ref[a0:a1, :]).astype(BF16)
        o_ref[0, c0:c1, :] = (_dot(pooled, wbd_ref[...]) * ps_ref[...]).astype(BF16)


def _pool(fp3, icnt, wbd, ps):
    bsz, seq_len, _ = fp3.shape
    tot = seq_len + 2 * POOL_PAD
    return pl.pallas_call(
        functools.partial(_pool_kernel, seq_len=seq_len),
        grid=(bsz,),
        in_specs=[pl.BlockSpec((1, seq_len, 256), lambda b: (b, 0, 1)),
                  pl.BlockSpec(icnt.shape, lambda b: (0, 0)),
                  pl.BlockSpec(wbd.shape, lambda b: (0, 0)),
                  pl.BlockSpec(ps.shape, lambda b: (0, 0))],
        out_specs=pl.BlockSpec((1, seq_len, 256), lambda b: (b, 0, 0)),
        out_shape=jax.ShapeDtypeStruct((bsz, seq_len, 256), BF16),
        scratch_shapes=[pltpu.VMEM((tot, 256), F32)] * 4,
        compiler_params=pltpu.CompilerParams(dimension_semantics=("arbitrary",),
                                             vmem_limit_bytes=VMEM_LIMIT),
        name="pool",
    )(fp3, icnt, wbd, ps)


def _kv_chunks(lp, tk):
    n = max(lp // tk, 1)
    return [(i * tk, tk if i < n - 1 else lp - i * tk) for i in range(n)]


def _attn_kernel(q_ref, k_ref, v_ref, o_ref, kp_ref, vp_ref, *, seq_len, tq, tk):
    lp = kp_ref.shape[0]
    pad = lp - seq_len
    kp_ref[0:seq_len, :] = k_ref[0]
    vp_ref[0:seq_len, 0:V_HEAD_DIM] = v_ref[0]
    vp_ref[0:seq_len, V_HEAD_DIM:] = jnp.ones((seq_len, V_HEAD_DIM), BF16)
    if pad:
        kp_ref[seq_len:lp, :] = jnp.zeros((pad, kp_ref.shape[1]), BF16)
        vp_ref[seq_len:lp, :] = jnp.zeros((pad, vp_ref.shape[1]), BF16)
    chunks = _kv_chunks(lp, tk)

    def attend(qs):
        ms = [None] * len(qs)
        accs = [None] * len(qs)
        for c0, cw in chunks:
            ss = [_dot_nt(q, kp_ref[c0:c0 + cw, :]) for q in qs]
            for j, s in enumerate(ss):
                if c0 + cw > seq_len:
                    col = lax.broadcasted_iota(jnp.int32, (s.shape[0], LANES), 1) + (c0 + cw - LANES)
                    last = jnp.where(col < seq_len, s[:, cw - LANES:], -jnp.inf)
                    s = last if cw == LANES else jnp.concatenate([s[:, :cw - LANES], last], axis=-1)
                m_c = jnp.max(s, axis=-1, keepdims=True)
                m_new = m_c if ms[j] is None else jnp.maximum(ms[j], m_c)
                pv = _dot(jnp.exp2(s - m_new).astype(BF16), vp_ref[c0:c0 + cw, :])
                accs[j] = pv if ms[j] is None else jnp.exp2(ms[j] - m_new) * accs[j] + pv
                ms[j] = m_new
        return [(a[:, :V_HEAD_DIM] * pl.reciprocal(a[:, V_HEAD_DIM:])).astype(BF16) for a in accs]

    def q_body(i, _):
        r0 = pl.multiple_of(jnp.minimum(i * tq, seq_len - tq), 16)
        rows = (r0, r0 + tq // 2)
        for r, o in zip(rows, attend([q_ref[0, pl.ds(r, tq // 2), :] for r in rows])):
            o_ref[0, pl.ds(r, tq // 2), :] = o
        return 0

    lax.fori_loop(0, pl.cdiv(seq_len, tq), q_body, 0)


def _attention(q3, k3, v3, *, tq, tk):
    bsz, seq_len, _ = q3.shape
    assert tq <= seq_len and tq % 32 == 0
    lp = _round_up(seq_len, LANES)
    return pl.pallas_call(
        functools.partial(_attn_kernel, seq_len=seq_len, tq=tq, tk=tk),
        grid=(bsz, MLA_HEADS),
        in_specs=[pl.BlockSpec((1, seq_len, 256), lambda b, h: (b, 0, h)),
                  pl.BlockSpec((1, seq_len, 256), lambda b, h: (b, 0, h)),
                  pl.BlockSpec((1, seq_len, V_HEAD_DIM), lambda b, h: (b, 0, h))],
        out_specs=pl.BlockSpec((1, seq_len, V_HEAD_DIM), lambda b, h: (b, 0, h)),
        out_shape=jax.ShapeDtypeStruct((bsz, seq_len, MLA_HEADS * V_HEAD_DIM), BF16),
        scratch_shapes=[pltpu.VMEM((lp, 256), BF16), pltpu.VMEM((lp, 2 * V_HEAD_DIM), BF16)],
        compiler_params=pltpu.CompilerParams(dimension_semantics=("arbitrary", "arbitrary"),
                                             vmem_limit_bytes=VMEM_LIMIT),
        name="attention",
    )(q3, k3, v3)


def _rope_tables(seq_len, rows):
    inv = 1.0 / (ROPE_THETA ** (jnp.arange(0, QK_ROPE_DIM, 2, dtype=F32) / QK_ROPE_DIM))
    pos = (jnp.arange(rows) % seq_len).astype(F32)
    ang = pos[:, None] * inv[None, :]
    cos, sin = jnp.cos(ang), jnp.sin(ang)
    return jnp.tile(cos, (1, 4)), jnp.tile(jnp.concatenate([-sin, sin], axis=-1), (1, 2))


def _dft_tables(seq_len):
    quarter = seq_len // 4
    hp, qp = _round_up(quarter, 16), _round_up(quarter, LANES)
    step = 64
    k = jnp.arange(hp, dtype=jnp.int32)[:, None]
    a = step * jnp.arange(qp // step, dtype=jnp.int32)[None, :]
    b = jnp.arange(step, dtype=jnp.int32)[None, :]
    valid = ((k < quarter)[:, :, None] & ((a[:, :, None] + b[:, None, :]) < quarter)).reshape(hp, qp)
    unit = 2.0 * math.pi / seq_len
    mats = []
    for r in range(4):
        row = 4 * k + r
        ang_a = (row * a % seq_len).astype(F32) * unit
        ang_b = (row * b % seq_len).astype(F32) * unit
        ca, sa = jnp.cos(ang_a)[:, :, None], jnp.sin(ang_a)[:, :, None]
        cb, sb = jnp.cos(ang_b)[:, None, :], jnp.sin(ang_b)[:, None, :]
        cos = (ca * cb - sa * sb).reshape(hp, qp)
        msin = -(sa * cb + ca * sb).reshape(hp, qp)
        mats.append(jnp.concatenate([jnp.where(valid, cos, 0.0), jnp.where(valid, msin, 0.0)], axis=1))
    return jnp.stack(mats).astype(BF16)


def _channel_dft(seq_len):
    c = np.arange(FOURIER_HEAD_DIM)
    ang = 2.0 * np.pi * ((c[:, None] * c[None, :]) % FOURIER_HEAD_DIM) / FOURIER_HEAD_DIM
    norm = 1.0 / math.sqrt(seq_len * FOURIER_HEAD_DIM)
    eye = np.eye(FOURIER_HEADS)
    return jnp.asarray(np.stack([np.kron(eye, np.cos(ang) * norm), np.kron(eye, np.sin(ang) * norm)]), F32)


def _block_diag(w):
    g, c, d = w.shape
    out = jnp.zeros((g * c, g * d), w.dtype)
    for i in range(g):
        out = out.at[i * c:(i + 1) * c, i * d:(i + 1) * d].set(w[i])
    return out


def _pool_inv_count(seq_len):
    idx = np.arange(seq_len)
    cols = []
    for w in POOL_WINDOWS:
        cnt = np.clip(idx + w // 2, 0, seq_len) - np.clip(idx - w // 2, 0, seq_len)
        cols.append(np.repeat((1.0 / cnt)[:, None], POOL_GROUP_DIM, axis=1))
    return jnp.asarray(np.concatenate(cols, axis=1), F32)


def _layer_weights(w_in, w_uq, w_ukv):
    sizes = (256, 256, 256, 256, w_uq.shape[0], w_ukv.shape[0], QK_ROPE_DIM, 512)
    offs = np.concatenate([[0], np.cumsum(sizes)])
    f_in, f_gate, p_in, p_gate, c_q, c_kv, k_r, a_gate = (w_in[:, offs[i]:offs[i + 1]] for i in range(8))
    half = QK_ROPE_DIM // 2
    swap = lambda a: jnp.concatenate([a[..., half:], a[..., :half]], axis=-1)
    w1 = jnp.concatenate([f_in, p_in, f_gate, p_gate, a_gate, c_q, c_kv,
                          k_r, k_r, swap(k_r), swap(k_r)], axis=1).astype(BF16)
    uq = w_uq.reshape(w_uq.shape[0], MLA_HEADS, QK_NOPE_DIM + QK_ROPE_DIM)
    q_rope = uq[:, :, QK_NOPE_DIM:]
    wuq = jnp.concatenate([uq[:, :, :QK_NOPE_DIM].reshape(w_uq.shape[0], -1),
                           q_rope.reshape(w_uq.shape[0], -1),
                           swap(q_rope).reshape(w_uq.shape[0], -1)], axis=1).astype(BF16)
    ukv = w_ukv.reshape(w_ukv.shape[0], MLA_HEADS, QK_NOPE_DIM + V_HEAD_DIM)
    wukv = jnp.concatenate([ukv[:, :, :QK_NOPE_DIM].reshape(w_ukv.shape[0], -1),
                            ukv[:, :, QK_NOPE_DIM:].reshape(w_ukv.shape[0], -1)], axis=1).astype(BF16)
    return w1, wuq, wukv


def kernel(x, meta_tokens, norm_w, w_in, fourier_w, pool_w, pool_scale, q_norm_w, w_uq, kv_norm_w, w_ukv,
           w_out, final_norm_w):
    bsz, seq, d = x.shape
    depth = norm_w.shape[0]
    seq_len = seq + N_META
    n_tok = bsz * seq_len
    tm = 256
    assert n_tok % tm == 0 and seq_len % 16 == 0

    h = jnp.concatenate([jnp.broadcast_to(meta_tokens[None].astype(x.dtype), (bsz, N_META, d)), x], axis=1)
    h2 = h.reshape(n_tok, d)

    cos_t, sin_t = _rope_tables(seq_len, seq_len + tm)
    mats = _dft_tables(seq_len)
    cs = _channel_dft(seq_len)
    icnt = _pool_inv_count(seq_len)
    fw = final_norm_w.reshape(1, d)

    def in_weights(l):
        w1, wuq, wukv = _layer_weights(w_in[l], w_uq[l], w_ukv[l])
        return (norm_w[l].reshape(1, d), w1, q_norm_w[l].reshape(1, -1), wuq, kv_norm_w[l].reshape(1, -1), wukv,
                cos_t, sin_t)

    fp, sg, q, k, v = _token_call(None, h2, None, in_weights(0), tm=tm, seq_len=seq_len)
    for l in range(depth):
        fp3 = fp.reshape(bsz, seq_len, 512)
        yf = _fourier(fp3, cs, _block_diag(fourier_w[l]), mats).reshape(n_tok, 256)
        yp = _pool(fp3, icnt, _block_diag(pool_w[l]).astype(BF16), pool_scale[l].reshape(1, -1)).reshape(n_tok, 256)
        ya = _attention(q.reshape(bsz, seq_len, 1024), k.reshape(bsz, seq_len, 1024),
                        v.reshape(bsz, seq_len, 512), tq=ATTN_TQ, tk=ATTN_TK).reshape(n_tok, 512)
        mix = (yf, yp, ya, sg)
        if l == depth - 1:
            return _final_call(mix, h2, w_out[l].astype(BF16), fw, bsz=bsz, seq_len=seq_len, tm=tm)
        h2, fp, sg, q, k, v = _token_call(mix, h2, w_out[l].astype(BF16), in_weights(l + 1), tm=tm, seq_len=seq_len)
```

```python
import functools
import math

import numpy as np
import jax
import jax.numpy as jnp
from jax import lax
from jax.experimental import pallas as pl
from jax.experimental.pallas import tpu as pltpu

N_META = 16
FOURIER_HEADS = 4
FOURIER_HEAD_DIM = 64
POOL_WINDOWS = (2, 4, 8, 16)
POOL_GROUP_DIM = 64
MLA_HEADS = 4
QK_NOPE_DIM = 128
QK_ROPE_DIM = 64
V_HEAD_DIM = 128
ROPE_THETA = 10000.0
NORM_EPS = 1e-6

LANES = 128
POOL_PAD = 32
VMEM_LIMIT = 56 * 1024 * 1024
ATTN_STREAMS = 2
ATTN_TQ = 1056

F32 = jnp.float32
BF16 = jnp.bfloat16


def _round_up(a, m):
    return (a + m - 1) // m * m


def _dot(a, b):
    return jnp.dot(a, b, preferred_element_type=F32)


def _dot_nt(a, b):
    return lax.dot_general(a, b, (((1,), (1,)), ((), ())), preferred_element_type=F32)


def _rms(x, w):
    return x * lax.rsqrt(jnp.mean(x * x, axis=-1, keepdims=True) + NORM_EPS) * w


def _project_in(x, nw_ref, w1_ref, qnw_ref, wuq_ref, kvnw_ref, wukv_ref, cos_ref, sin_ref,
                fp_ref, sg_ref, q_ref, k_ref, v_ref, *, tm, seq_len, scale, d_q, d_kv):
    t = pl.program_id(0)
    n = _rms(x, nw_ref[...]).astype(BF16)

    fp_ref[...] = _dot(n, w1_ref[:, 0:512])
    g = _dot(n, w1_ref[:, 512:1536])
    sg_ref[...] = (g * pl.reciprocal(1.0 + jnp.exp(-g))).astype(BF16)

    start = pl.multiple_of(lax.rem(t * tm, seq_len), 16)
    cos = cos_ref[pl.ds(start, tm), :]
    sin = sin_ref[pl.ds(start, tm), :]

    o = 1536
    cq = _rms(_dot(n, w1_ref[:, o:o + d_q]), qnw_ref[...]).astype(BF16)
    o += d_q
    ckv = _rms(_dot(n, w1_ref[:, o:o + d_kv]), kvnw_ref[...]).astype(BF16)
    o += d_kv
    kr = _dot(n, w1_ref[:, o:o + 2 * LANES])

    qf = _dot(cq, wuq_ref[...])
    nq = MLA_HEADS * QK_NOPE_DIM
    nr = MLA_HEADS * QK_ROPE_DIM
    for h in range(MLA_HEADS):
        q_ref[:, 256 * h:256 * h + 128] = (qf[:, 128 * h:128 * h + 128] * scale).astype(BF16)
    for pair in range(MLA_HEADS // 2):
        r = qf[:, nq + 128 * pair:nq + 128 * pair + 128]
        rs = qf[:, nq + nr + 128 * pair:nq + nr + 128 * pair + 128]
        roped = ((r * cos + rs * sin) * scale).astype(BF16)
        for h in (2 * pair, 2 * pair + 1):
            q_ref[:, 256 * h + 128:256 * h + 256] = roped

    kvf = _dot(ckv, wukv_ref[...])
    v_ref[...] = kvf[:, nq:].astype(BF16)
    kroped = kr[:, :LANES] * cos + kr[:, LANES:] * sin
    lane = lax.broadcasted_iota(jnp.int32, kroped.shape, 1)
    k_lo = jnp.where(lane < QK_ROPE_DIM, kroped, 0.0).astype(BF16)
    k_hi = jnp.where(lane >= QK_ROPE_DIM, kroped, 0.0).astype(BF16)
    for h in range(MLA_HEADS):
        k_ref[:, 256 * h:256 * h + 128] = kvf[:, 128 * h:128 * h + 128].astype(BF16)
        k_ref[:, 256 * h + 128:256 * h + 256] = k_lo if h % 2 == 0 else k_hi


def _mix_residual(yf_ref, yp_ref, ya_ref, sg_ref, h_ref, w_ref):
    y = jnp.concatenate([yf_ref[...], yp_ref[...], ya_ref[...]], axis=-1)
    return h_ref[...] + _dot(y * sg_ref[...], w_ref[...])


def _inproj_kernel(x_ref, *refs, **kw):
    _project_in(x_ref[...], *refs, **kw)


def _fused_kernel(yf_ref, yp_ref, ya_ref, sg_ref, h_ref, w_ref, *refs, **kw):
    h_new = _mix_residual(yf_ref, yp_ref, ya_ref, sg_ref, h_ref, w_ref)
    refs[8][...] = h_new
    _project_in(h_new, *refs[:8], *refs[9:], **kw)


def _final_kernel(yf_ref, yp_ref, ya_ref, sg_ref, h_ref, w_ref, fw_ref, o_ref):
    o_ref[0] = _rms(_mix_residual(yf_ref, yp_ref, ya_ref, sg_ref, h_ref, w_ref), fw_ref[...])


_MIX_WIDTHS = (256, 256, 512, 1024)
_IN_OUT = ((512, F32), (1024, BF16), (1024, BF16), (1024, BF16), (512, BF16))


def _token_call(mix, h2, w_out, in_w, *, tm, seq_len):
    n_tok, d = h2.shape
    nw, w1, qnw, wuq, kvnw, wukv, cos_t, sin_t = in_w
    scale = float((QK_NOPE_DIM + QK_ROPE_DIM) ** -0.5 * math.log2(math.e))
    const = lambda a: pl.BlockSpec(a.shape, lambda t: (0,) * a.ndim)
    row = lambda w: pl.BlockSpec((tm, w), lambda t: (t, 0))
    kw = dict(tm=tm, seq_len=seq_len, scale=scale, d_q=qnw.shape[1], d_kv=kvnw.shape[1])
    in_specs = [const(a) for a in in_w]
    out_specs = [row(w) for w, _ in _IN_OUT]
    out_shape = [jax.ShapeDtypeStruct((n_tok, w), dt) for w, dt in _IN_OUT]
    if mix is None:
        body, args, aliases = _inproj_kernel, (h2, *in_w), {}
        in_specs = [row(d)] + in_specs
    else:
        body, args, aliases = _fused_kernel, (*mix, h2, w_out, *in_w), {4: 0}
        in_specs = [row(w) for w in _MIX_WIDTHS] + [row(d), const(w_out)] + in_specs
        out_specs = [row(d)] + out_specs
        out_shape = [jax.ShapeDtypeStruct((n_tok, d), F32)] + out_shape
    return pl.pallas_call(
        functools.partial(body, **kw),
        grid=(n_tok // tm,),
        in_specs=in_specs, out_specs=out_specs, out_shape=out_shape,
        input_output_aliases=aliases,
        compiler_params=pltpu.CompilerParams(dimension_semantics=("arbitrary",),
                                             vmem_limit_bytes=VMEM_LIMIT),
        name="inproj" if mix is None else "outin",
    )(*args)


def _final_call(mix, h2, w_out, fw, *, bsz, seq_len, tm):
    n_tok, d = h2.shape
    seq = seq_len - N_META
    row = lambda w: pl.BlockSpec((pl.Element(tm), pl.Element(w)),
                                 lambda b, j: (pl.multiple_of(b * seq_len + N_META + j * tm, 16), 0))
    const = lambda a: pl.BlockSpec(a.shape, lambda b, j: (0,) * a.ndim)
    return pl.pallas_call(
        _final_kernel,
        grid=(bsz, seq // tm),
        in_specs=[row(w) for w in _MIX_WIDTHS] + [row(d), const(w_out), const(fw)],
        out_specs=pl.BlockSpec((1, tm, d), lambda b, j: (b, j, 0)),
        out_shape=jax.ShapeDtypeStruct((bsz, seq, d), F32),
        compiler_params=pltpu.CompilerParams(dimension_semantics=("arbitrary", "arbitrary"),
                                             vmem_limit_bytes=VMEM_LIMIT),
        name="final",
    )(*mix, h2, w_out, fw)


def _fourier_kernel(x_ref, cs_ref, wbd_ref, mats_ref, o_ref, pq_ref, u_ref, y_ref, *, quarter):
    b = pl.program_id(0)
    qp = u_ref.shape[1]

    @pl.when(b == 0)
    def _():
        w = wbd_ref[...]
        pq_ref[:, 0:256] = jnp.dot(cs_ref[0], w, preferred_element_type=F32,
                                   precision=lax.Precision.HIGHEST).astype(BF16)
        pq_ref[:, 256:512] = jnp.dot(cs_ref[1], w, preferred_element_type=F32,
                                     precision=lax.Precision.HIGHEST).astype(BF16)
        u_ref[:, quarter:qp, :] = jnp.zeros((4, qp - quarter, u_ref.shape[2]), F32)

    x0, x1, x2, x3 = (x_ref[0, j * quarter:(j + 1) * quarter, :] for j in range(4))
    s02, s13, d02, d13 = x0 + x2, x1 + x3, x0 - x2, x1 - x3
    for j, u in enumerate((s02 + s13, s02 - s13, d02, d13)):
        u_ref[j, 0:quarter, :] = u
    pq = pq_ref[...]
    ab = [_dot(u_ref[j].astype(BF16), pq) for j in range(4)]
    re = (ab[0][:, :256], ab[2][:, :256] - ab[3][:, 256:], ab[1][:, :256], ab[2][:, :256] + ab[3][:, 256:])
    im = (ab[0][:, 256:], ab[2][:, 256:] + ab[3][:, :256], ab[1][:, 256:], ab[2][:, 256:] - ab[3][:, :256])
    for r in range(4):
        rhs = jnp.concatenate([re[r].astype(BF16), im[r].astype(BF16)], axis=0)
        y = _dot(mats_ref[r], rhs)
        for c in range(2):
            y_ref[c, pl.ds(r, quarter, stride=4), :] = y[0:quarter, c * LANES:(c + 1) * LANES]
    for c in range(2):
        o_ref[0, :, c * LANES:(c + 1) * LANES] = y_ref[c].astype(BF16)


def _fourier(fp3, cs, wbd, mats):
    bsz, seq_len, _ = fp3.shape
    quarter = seq_len // 4
    qp = mats.shape[2] // 2
    return pl.pallas_call(
        functools.partial(_fourier_kernel, quarter=quarter),
        grid=(bsz,),
        in_specs=[pl.BlockSpec((1, seq_len, 256), lambda b: (b, 0, 0)),
                  pl.BlockSpec(cs.shape, lambda b: (0, 0, 0)),
                  pl.BlockSpec(wbd.shape, lambda b: (0, 0)),
                  pl.BlockSpec(mats.shape, lambda b: (0, 0, 0), pipeline_mode=pl.Buffered(1))],
        out_specs=pl.BlockSpec((1, seq_len, 256), lambda b: (b, 0, 0)),
        out_shape=jax.ShapeDtypeStruct((bsz, seq_len, 256), BF16),
        scratch_shapes=[pltpu.VMEM((256, 512), BF16), pltpu.VMEM((4, qp, 256), F32),
                        pltpu.VMEM((2, seq_len, LANES), F32)],
        compiler_params=pltpu.CompilerParams(dimension_semantics=("arbitrary",),
                                             vmem_limit_bytes=VMEM_LIMIT),
        name="fourier",
    )(fp3, cs, wbd, mats)


def _pool_kernel(x_ref, icnt_ref, wbd_ref, ps_ref, o_ref, xp_ref, s2_ref, s4_ref, s8_ref, *, seq_len):
    pad = POOL_PAD
    tot = seq_len + 2 * pad
    zeros = jnp.zeros((pad, 256), F32)
    xp_ref[0:pad, :] = zeros
    xp_ref[pad + seq_len:tot, :] = zeros
    xp_ref[pad:pad + seq_len, :] = x_ref[0]

    def level(src, dst, lo, hi, d_lo, d_hi):
        chunk = 512
        for c0 in range(lo, hi, chunk):
            c1 = min(c0 + chunk, hi)
            dst[c0:c1, :] = src[c0 + d_lo:c1 + d_lo, :] + src[c0 + d_hi:c1 + d_hi, :]

    level(xp_ref, s2_ref, 8, tot - 8, -1, 0)
    level(s2_ref, s4_ref, 16, tot - 16, -1, 1)
    level(s4_ref, s8_ref, 24, tot - 24, -2, 2)

    chunk = 512
    for c0 in range(0, seq_len, chunk):
        c1 = min(c0 + chunk, seq_len)
        a0, a1 = pad + c0, pad + c1
        s16 = s8_ref[a0 - 4:a1 - 4, :] + s8_ref[a0 + 4:a1 + 4, :]
        lane = lax.broadcasted_iota(jnp.int32, s16.shape, 1)
        hi_sel = jnp.where(lane < 192, s8_ref[a0:a1, :], s16)
        lo_sel = jnp.where(lane < 64, s2_ref[a0:a1, :], s4_ref[a0:a1, :])
        win = jnp.where(lane < 128, lo_sel, hi_sel)
        pooled = (win * icnt_ref[c0:c1, :] - xp_ref[a0:a1, :]).astype(BF16)
        o_ref[0, c0:c1, :] = (_dot(pooled, wbd_ref[...]) * ps_ref[...]).astype(BF16)


def _pool(fp3, icnt, wbd, ps):
    bsz, seq_len, _ = fp3.shape
    tot = seq_len + 2 * POOL_PAD
    return pl.pallas_call(
        functools.partial(_pool_kernel, seq_len=seq_len),
        grid=(bsz,),
        in_specs=[pl.BlockSpec((1, seq_len, 256), lambda b: (b, 0, 1)),
                  pl.BlockSpec(icnt.shape, lambda b: (0, 0)),
                  pl.BlockSpec(wbd.shape, lambda b: (0, 0)),
                  pl.BlockSpec(ps.shape, lambda b: (0, 0))],
        out_specs=pl.BlockSpec((1, seq_len, 256), lambda b: (b, 0, 0)),
        out_shape=jax.ShapeDtypeStruct((bsz, seq_len, 256), BF16),
        scratch_shapes=[pltpu.VMEM((tot, 256), F32)] * 4,
        compiler_params=pltpu.CompilerParams(dimension_semantics=("arbitrary",),
                                             vmem_limit_bytes=VMEM_LIMIT),
        name="pool",
    )(fp3, icnt, wbd, ps)


def _attn_kernel(q_ref, k_ref, v_ref, o_ref, kp_ref, vp_ref, *, seq_len, tq):
    lp = kp_ref.shape[0]
    pad = lp - seq_len
    kp_ref[0:seq_len, :] = k_ref[0]
    vp_ref[0:seq_len, 0:V_HEAD_DIM] = v_ref[0]
    vp_ref[0:seq_len, V_HEAD_DIM:] = jnp.ones((seq_len, V_HEAD_DIM), BF16)
    if pad:
        kp_ref[seq_len:lp, :] = jnp.zeros((pad, kp_ref.shape[1]), BF16)
        vp_ref[seq_len:lp, :] = jnp.zeros((pad, vp_ref.shape[1]), BF16)

    def attend(qs):
        outs = []
        for s in [_dot_nt(q, kp_ref[...]) for q in qs]:
            if pad:
                col = lax.broadcasted_iota(jnp.int32, (s.shape[0], LANES), 1) + (lp - LANES)
                last = jnp.where(col < seq_len, s[:, lp - LANES:], -jnp.inf)
                s = jnp.concatenate([s[:, :lp - LANES], last], axis=-1)
            p = jnp.exp2(s - jnp.max(s, axis=-1, keepdims=True)).astype(BF16)
            pv = _dot(p, vp_ref[...])
            outs.append((pv[:, :V_HEAD_DIM] * pl.reciprocal(pv[:, V_HEAD_DIM:])).astype(BF16))
        return outs

    ts = tq // ATTN_STREAMS

    def q_body(i, _):
        r0 = pl.multiple_of(jnp.minimum(i * tq, seq_len - tq), 16)
        rows = [r0 + j * ts for j in range(ATTN_STREAMS)]
        for r, o in zip(rows, attend([q_ref[0, pl.ds(r, ts), :] for r in rows])):
            o_ref[0, pl.ds(r, ts), :] = o
        return 0

    lax.fori_loop(0, pl.cdiv(seq_len, tq), q_body, 0)


def _attention(q3, k3, v3, *, tq):
    bsz, seq_len, _ = q3.shape
    assert tq <= seq_len and tq % 32 == 0
    lp = _round_up(seq_len, LANES)
    return pl.pallas_call(
        functools.partial(_attn_kernel, seq_len=seq_len, tq=tq),
        grid=(bsz, MLA_HEADS),
        in_specs=[pl.BlockSpec((1, seq_len, 256), lambda b, h: (b, 0, h)),
                  pl.BlockSpec((1, seq_len, 256), lambda b, h: (b, 0, h)),
                  pl.BlockSpec((1, seq_len, V_HEAD_DIM), lambda b, h: (b, 0, h))],
        out_specs=pl.BlockSpec((1, seq_len, V_HEAD_DIM), lambda b, h: (b, 0, h)),
        out_shape=jax.ShapeDtypeStruct((bsz, seq_len, MLA_HEADS * V_HEAD_DIM), BF16),
        scratch_shapes=[pltpu.VMEM((lp, 256), BF16), pltpu.VMEM((lp, 2 * V_HEAD_DIM), BF16)],
        compiler_params=pltpu.CompilerParams(dimension_semantics=("arbitrary", "arbitrary"),
                                             vmem_limit_bytes=VMEM_LIMIT),
        name="attention",
    )(q3, k3, v3)


def _rope_tables(seq_len, rows):
    inv = 1.0 / (ROPE_THETA ** (jnp.arange(0, QK_ROPE_DIM, 2, dtype=F32) / QK_ROPE_DIM))
    pos = (jnp.arange(rows) % seq_len).astype(F32)
    ang = pos[:, None] * inv[None, :]
    cos, sin = jnp.cos(ang), jnp.sin(ang)
    return jnp.tile(cos, (1, 4)), jnp.tile(jnp.concatenate([-sin, sin], axis=-1), (1, 2))


def _dft_tables(seq_len):
    quarter = seq_len // 4
    hp, qp = _round_up(quarter, 16), _round_up(quarter, LANES)
    step = 64
    k = jnp.arange(hp, dtype=jnp.int32)[:, None]
    a = step * jnp.arange(qp // step, dtype=jnp.int32)[None, :]
    b = jnp.arange(step, dtype=jnp.int32)[None, :]
    valid = ((k < quarter)[:, :, None] & ((a[:, :, None] + b[:, None, :]) < quarter)).reshape(hp, qp)
    unit = 2.0 * math.pi / seq_len
    mats = []
    for r in range(4):
        row = 4 * k + r
        ang_a = (row * a % seq_len).astype(F32) * unit
        ang_b = (row * b % seq_len).astype(F32) * unit
        ca, sa = jnp.cos(ang_a)[:, :, None], jnp.sin(ang_a)[:, :, None]
        cb, sb = jnp.cos(ang_b)[:, None, :], jnp.sin(ang_b)[:, None, :]
        cos = (ca * cb - sa * sb).reshape(hp, qp)
        msin = -(sa * cb + ca * sb).reshape(hp, qp)
        mats.append(jnp.concatenate([jnp.where(valid, cos, 0.0), jnp.where(valid, msin, 0.0)], axis=1))
    return jnp.stack(mats).astype(BF16)


def _channel_dft(seq_len):
    c = np.arange(FOURIER_HEAD_DIM)
    ang = 2.0 * np.pi * ((c[:, None] * c[None, :]) % FOURIER_HEAD_DIM) / FOURIER_HEAD_DIM
    norm = 1.0 / math.sqrt(seq_len * FOURIER_HEAD_DIM)
    eye = np.eye(FOURIER_HEADS)
    return jnp.asarray(np.stack([np.kron(eye, np.cos(ang) * norm), np.kron(eye, np.sin(ang) * norm)]), F32)


def _block_diag(w):
    n, g, c, d = w.shape
    out = jnp.zeros((n, g * c, g * d), w.dtype)
    for i in range(g):
        out = out.at[:, i * c:(i + 1) * c, i * d:(i + 1) * d].set(w[:, i])
    return out


def _pool_inv_count(seq_len):
    idx = np.arange(seq_len)
    cols = []
    for w in POOL_WINDOWS:
        cnt = np.clip(idx + w // 2, 0, seq_len) - np.clip(idx - w // 2, 0, seq_len)
        cols.append(np.repeat((1.0 / cnt)[:, None], POOL_GROUP_DIM, axis=1))
    return jnp.asarray(np.concatenate(cols, axis=1), F32)


def _layer_weights(w_in, w_uq, w_ukv):
    n, d_q, d_kv = w_in.shape[0], w_uq.shape[1], w_ukv.shape[1]
    sizes = (256, 256, 256, 256, d_q, d_kv, QK_ROPE_DIM, 512)
    offs = np.concatenate([[0], np.cumsum(sizes)])
    f_in, f_gate, p_in, p_gate, c_q, c_kv, k_r, a_gate = (w_in[..., offs[i]:offs[i + 1]] for i in range(8))
    half = QK_ROPE_DIM // 2
    swap = lambda a: jnp.concatenate([a[..., half:], a[..., :half]], axis=-1)
    w1 = jnp.concatenate([f_in, p_in, f_gate, p_gate, a_gate, c_q, c_kv,
                          k_r, k_r, swap(k_r), swap(k_r)], axis=-1).astype(BF16)
    uq = w_uq.reshape(n, d_q, MLA_HEADS, QK_NOPE_DIM + QK_ROPE_DIM)
    q_rope = uq[..., QK_NOPE_DIM:]
    wuq = jnp.concatenate([uq[..., :QK_NOPE_DIM].reshape(n, d_q, -1), q_rope.reshape(n, d_q, -1),
                           swap(q_rope).reshape(n, d_q, -1)], axis=-1).astype(BF16)
    ukv = w_ukv.reshape(n, d_kv, MLA_HEADS, QK_NOPE_DIM + V_HEAD_DIM)
    wukv = jnp.concatenate([ukv[..., :QK_NOPE_DIM].reshape(n, d_kv, -1),
                            ukv[..., QK_NOPE_DIM:].reshape(n, d_kv, -1)], axis=-1).astype(BF16)
    return w1, wuq, wukv


def kernel(x, meta_tokens, norm_w, w_in, fourier_w, pool_w, pool_scale, q_norm_w, w_uq, kv_norm_w, w_ukv,
           w_out, final_norm_w):
    bsz, seq, d = x.shape
    depth = norm_w.shape[0]
    seq_len = seq + N_META
    n_tok = bsz * seq_len
    tm = 256
    assert n_tok % tm == 0 and seq_len % 16 == 0

    h = jnp.concatenate([jnp.broadcast_to(meta_tokens[None].astype(x.dtype), (bsz, N_META, d)), x], axis=1)
    h2 = h.reshape(n_tok, d)

    cos_t, sin_t = _rope_tables(seq_len, seq_len + tm)
    mats = _dft_tables(seq_len)
    cs = _channel_dft(seq_len)
    icnt = _pool_inv_count(seq_len)
    fw = final_norm_w.reshape(1, d)

    w1, wuq, wukv = _layer_weights(w_in, w_uq, w_ukv)
    w_o = w_out.astype(BF16)
    f_bd = _block_diag(fourier_w)
    p_bd = _block_diag(pool_w).astype(BF16)

    def in_weights(l):
        return (norm_w[l].reshape(1, d), w1[l], q_norm_w[l].reshape(1, -1), wuq[l], kv_norm_w[l].reshape(1, -1),
                wukv[l], cos_t, sin_t)

    fp, sg, q, k, v = _token_call(None, h2, None, in_weights(0), tm=tm, seq_len=seq_len)
    for l in range(depth):
        fp3 = fp.reshape(bsz, seq_len, 512)
        yf = _fourier(fp3, cs, f_bd[l], mats).reshape(n_tok, 256)
        yp = _pool(fp3, icnt, p_bd[l], pool_scale[l].reshape(1, -1)).reshape(n_tok, 256)
        ya = _attention(q.reshape(bsz, seq_len, 1024), k.reshape(bsz, seq_len, 1024),
                        v.reshape(bsz, seq_len, 512), tq=ATTN_TQ).reshape(n_tok, 512)
        mix = (yf, yp, ya, sg)
        if l == depth - 1:
            return _final_call(mix, h2, w_o[l], fw, bsz=bsz, seq_len=seq_len, tm=tm)
        h2, fp, sg, q, k, v = _token_call(mix, h2, w_o[l], in_weights(l + 1), tm=tm, seq_len=seq_len)
```

```python
import functools
import math

import numpy as np
import jax
import jax.numpy as jnp
from jax import lax
from jax.experimental import pallas as pl
from jax.experimental.pallas import tpu as pltpu

N_META = 16
FOURIER_HEADS = 4
FOURIER_HEAD_DIM = 64
POOL_WINDOWS = (2, 4, 8, 16)
POOL_GROUP_DIM = 64
MLA_HEADS = 4
QK_NOPE_DIM = 128
QK_ROPE_DIM = 64
V_HEAD_DIM = 128
ROPE_THETA = 10000.0
NORM_EPS = 1e-6

LANES = 128
POOL_PAD = 32
VMEM_LIMIT = 56 * 1024 * 1024
FINAL_TM = 1024
ATTN_STREAMS = 2
ATTN_TQ = 1056

F32 = jnp.float32
BF16 = jnp.bfloat16


def _round_up(a, m):
    return (a + m - 1) // m * m


def _dot(a, b):
    return jnp.dot(a, b, preferred_element_type=F32)


def _dot_nt(a, b):
    return lax.dot_general(a, b, (((1,), (1,)), ((), ())), preferred_element_type=F32)


def _rms(x, w):
    return x * lax.rsqrt(jnp.mean(x * x, axis=-1, keepdims=True) + NORM_EPS) * w


def _project_in(x, nw_ref, w1_ref, qnw_ref, wuq_ref, kvnw_ref, wukv_ref, cos_ref, sin_ref,
                fp_ref, sg_ref, q_ref, k_ref, v_ref, *, tm, seq_len, scale, d_q, d_kv):
    t = pl.program_id(0)
    n = _rms(x, nw_ref[...]).astype(BF16)

    fp_ref[...] = _dot(n, w1_ref[:, 0:512])
    g = _dot(n, w1_ref[:, 512:1536])
    sg_ref[...] = (g * pl.reciprocal(1.0 + jnp.exp(-g))).astype(BF16)

    start = pl.multiple_of(lax.rem(t * tm, seq_len), 16)
    cos = cos_ref[pl.ds(start, tm), :]
    sin = sin_ref[pl.ds(start, tm), :]

    o = 1536
    cq = _rms(_dot(n, w1_ref[:, o:o + d_q]), qnw_ref[...]).astype(BF16)
    o += d_q
    ckv = _rms(_dot(n, w1_ref[:, o:o + d_kv]), kvnw_ref[...]).astype(BF16)
    o += d_kv
    kr = _dot(n, w1_ref[:, o:o + 2 * LANES])

    qf = _dot(cq, wuq_ref[...])
    nq = MLA_HEADS * QK_NOPE_DIM
    nr = MLA_HEADS * QK_ROPE_DIM
    for h in range(MLA_HEADS):
        q_ref[:, 256 * h:256 * h + 128] = (qf[:, 128 * h:128 * h + 128] * scale).astype(BF16)
    for pair in range(MLA_HEADS // 2):
        r = qf[:, nq + 128 * pair:nq + 128 * pair + 128]
        rs = qf[:, nq + nr + 128 * pair:nq + nr + 128 * pair + 128]
        roped = ((r * cos + rs * sin) * scale).astype(BF16)
        for h in (2 * pair, 2 * pair + 1):
            q_ref[:, 256 * h + 128:256 * h + 256] = roped

    kvf = _dot(ckv, wukv_ref[...])
    v_ref[...] = kvf[:, nq:].astype(BF16)
    kroped = kr[:, :LANES] * cos + kr[:, LANES:] * sin
    lane = lax.broadcasted_iota(jnp.int32, kroped.shape, 1)
    k_lo = jnp.where(lane < QK_ROPE_DIM, kroped, 0.0).astype(BF16)
    k_hi = jnp.where(lane >= QK_ROPE_DIM, kroped, 0.0).astype(BF16)
    for h in range(MLA_HEADS):
        k_ref[:, 256 * h:256 * h + 128] = kvf[:, 128 * h:128 * h + 128].astype(BF16)
        k_ref[:, 256 * h + 128:256 * h + 256] = k_lo if h % 2 == 0 else k_hi


def _mix_residual(yf_ref, yp_ref, ya_ref, sg_ref, h_ref, w_ref):
    y = jnp.concatenate([yf_ref[...], yp_ref[...], ya_ref[...]], axis=-1)
    return h_ref[...] + _dot(y * sg_ref[...], w_ref[...])


def _embed_kernel(xw_ref, meta_ref, *refs, tm, seq_len, **kw):
    t = pl.program_id(0)
    h_ref = refs[8]
    first_pos = lax.rem(t * tm, seq_len)
    meta_group = lax.rem(seq_len - first_pos, seq_len) // N_META
    for j in range(tm // N_META):
        src = pl.multiple_of(N_META * (j - (meta_group < j).astype(jnp.int32)), N_META)
        blk = jnp.where(meta_group == j, meta_ref[...], xw_ref[pl.ds(src, N_META), :])
        h_ref[N_META * j:N_META * (j + 1), :] = blk
    _project_in(h_ref[...], *refs[:8], *refs[9:], tm=tm, seq_len=seq_len, **kw)


def _fused_kernel(yf_ref, yp_ref, ya_ref, sg_ref, h_ref, w_ref, *refs, **kw):
    h_new = _mix_residual(yf_ref, yp_ref, ya_ref, sg_ref, h_ref, w_ref)
    refs[8][...] = h_new
    _project_in(h_new, *refs[:8], *refs[9:], **kw)


def _final_kernel(yf_ref, yp_ref, ya_ref, sg_ref, h_ref, w_ref, fw_ref, o_ref):
    o_ref[0] = _rms(_mix_residual(yf_ref, yp_ref, ya_ref, sg_ref, h_ref, w_ref), fw_ref[...])


_MIX_WIDTHS = (256, 256, 512, 1024)
_IN_OUT = ((512, F32), (1024, BF16), (1024, BF16), (1024, BF16), (512, BF16))


def _token_call(src, in_w, *, n_tok, tm, seq_len):
    nw, w1, qnw, wuq, kvnw, wukv, cos_t, sin_t = in_w
    d = nw.shape[1]
    scale = float((QK_NOPE_DIM + QK_ROPE_DIM) ** -0.5 * math.log2(math.e))
    const = lambda a: pl.BlockSpec(a.shape, lambda t: (0,) * a.ndim)
    row = lambda w: pl.BlockSpec((tm, w), lambda t: (t, 0))
    kw = dict(tm=tm, seq_len=seq_len, scale=scale, d_q=qnw.shape[1], d_kv=kvnw.shape[1])
    out_specs = [row(d)] + [row(w) for w, _ in _IN_OUT]
    out_shape = [jax.ShapeDtypeStruct((n_tok, d), F32)] + [jax.ShapeDtypeStruct((n_tok, w), dt) for w, dt in _IN_OUT]
    if len(src) == 2:
        x2, meta = src
        seq = seq_len - N_META

        def window(t):
            b0, p0 = (t * tm) // seq_len, lax.rem(t * tm, seq_len)
            return pl.multiple_of(b0 * seq + jnp.maximum(p0 - N_META, 0), N_META), 0

        body, args, aliases = _embed_kernel, (x2, meta, *in_w), {}
        in_specs = [pl.BlockSpec((pl.Element(tm), pl.Element(d)), window), const(meta)]
    else:
        mix, h2, w_out = src
        body, args, aliases = _fused_kernel, (*mix, h2, w_out, *in_w), {4: 0}
        in_specs = [row(w) for w in _MIX_WIDTHS] + [row(d), const(w_out)]
    in_specs += [const(a) for a in in_w]
    return pl.pallas_call(
        functools.partial(body, **kw),
        grid=(n_tok // tm,),
        in_specs=in_specs, out_specs=out_specs, out_shape=out_shape,
        input_output_aliases=aliases,
        compiler_params=pltpu.CompilerParams(dimension_semantics=("arbitrary",),
                                             vmem_limit_bytes=VMEM_LIMIT),
        name="embed" if len(src) == 2 else "outin",
    )(*args)


def _final_call(mix, h2, w_out, fw, *, bsz, seq_len, tm):
    n_tok, d = h2.shape
    seq = seq_len - N_META
    row = lambda w: pl.BlockSpec((pl.Element(tm), pl.Element(w)),
                                 lambda b, j: (pl.multiple_of(b * seq_len + N_META + j * tm, 16), 0))
    const = lambda a: pl.BlockSpec(a.shape, lambda b, j: (0,) * a.ndim)
    return pl.pallas_call(
        _final_kernel,
        grid=(bsz, seq // tm),
        in_specs=[row(w) for w in _MIX_WIDTHS] + [row(d), const(w_out), const(fw)],
        out_specs=pl.BlockSpec((1, tm, d), lambda b, j: (b, j, 0)),
        out_shape=jax.ShapeDtypeStruct((bsz, seq, d), F32),
        compiler_params=pltpu.CompilerParams(dimension_semantics=("arbitrary", "arbitrary"),
                                             vmem_limit_bytes=VMEM_LIMIT),
        name="final",
    )(*mix, h2, w_out, fw)


def _fourier_kernel(x_ref, cs_ref, wbd_ref, mats_ref, o_ref, pq_ref, u_ref, y_ref, *, quarter):
    b = pl.program_id(0)
    qp = u_ref.shape[1]

    @pl.when(b == 0)
    def _():
        w = wbd_ref[...]
        pq_ref[:, 0:256] = jnp.dot(cs_ref[0], w, preferred_element_type=F32,
                                   precision=lax.Precision.HIGHEST).astype(BF16)
        pq_ref[:, 256:512] = jnp.dot(cs_ref[1], w, preferred_element_type=F32,
                                     precision=lax.Precision.HIGHEST).astype(BF16)
        u_ref[:, quarter:qp, :] = jnp.zeros((4, qp - quarter, u_ref.shape[2]), F32)

    x0, x1, x2, x3 = (x_ref[0, j * quarter:(j + 1) * quarter, :] for j in range(4))
    s02, s13, d02, d13 = x0 + x2, x1 + x3, x0 - x2, x1 - x3
    for j, u in enumerate((s02 + s13, s02 - s13, d02, d13)):
        u_ref[j, 0:quarter, :] = u
    pq = pq_ref[...]
    ab = [_dot(u_ref[j].astype(BF16), pq) for j in range(4)]
    re = (ab[0][:, :256], ab[2][:, :256] - ab[3][:, 256:], ab[1][:, :256], ab[2][:, :256] + ab[3][:, 256:])
    im = (ab[0][:, 256:], ab[2][:, 256:] + ab[3][:, :256], ab[1][:, 256:], ab[2][:, 256:] - ab[3][:, :256])
    for r in range(4):
        rhs = jnp.concatenate([re[r].astype(BF16), im[r].astype(BF16)], axis=0)
        y = _dot(mats_ref[r], rhs)
        for c in range(2):
            y_ref[c, pl.ds(r, quarter, stride=4), :] = y[0:quarter, c * LANES:(c + 1) * LANES]
    for c in range(2):
        o_ref[0, :, c * LANES:(c + 1) * LANES] = y_ref[c].astype(BF16)


def _fourier(fp3, cs, wbd, mats):
    bsz, seq_len, _ = fp3.shape
    quarter = seq_len // 4
    qp = mats.shape[2] // 2
    return pl.pallas_call(
        functools.partial(_fourier_kernel, quarter=quarter),
        grid=(bsz,),
        in_specs=[pl.BlockSpec((1, seq_len, 256), lambda b: (b, 0, 0)),
                  pl.BlockSpec(cs.shape, lambda b: (0, 0, 0)),
                  pl.BlockSpec(wbd.shape, lambda b: (0, 0)),
                  pl.BlockSpec(mats.shape, lambda b: (0, 0, 0), pipeline_mode=pl.Buffered(1))],
        out_specs=pl.BlockSpec((1, seq_len, 256), lambda b: (b, 0, 0)),
        out_shape=jax.ShapeDtypeStruct((bsz, seq_len, 256), BF16),
        scratch_shapes=[pltpu.VMEM((256, 512), BF16), pltpu.VMEM((4, qp, 256), F32),
                        pltpu.VMEM((2, seq_len, LANES), F32)],
        compiler_params=pltpu.CompilerParams(dimension_semantics=("arbitrary",),
                                             vmem_limit_bytes=VMEM_LIMIT),
        name="fourier",
    )(fp3, cs, wbd, mats)


def _pool_kernel(x_ref, icnt_ref, wbd_ref, ps_ref, o_ref, xp_ref, s2_ref, s4_ref, s8_ref, *, seq_len):
    pad = POOL_PAD
    tot = seq_len + 2 * pad
    zeros = jnp.zeros((pad, 256), F32)
    xp_ref[0:pad, :] = zeros
    xp_ref[pad + seq_len:tot, :] = zeros
    xp_ref[pad:pad + seq_len, :] = x_ref[0]

    def level(src, dst, lo, hi, d_lo, d_hi):
        chunk = 512
        for c0 in range(lo, hi, chunk):
            c1 = min(c0 + chunk, hi)
            dst[c0:c1, :] = src[c0 + d_lo:c1 + d_lo, :] + src[c0 + d_hi:c1 + d_hi, :]

    level(xp_ref, s2_ref, 8, tot - 8, -1, 0)
    level(s2_ref, s4_ref, 16, tot - 16, -1, 1)
    level(s4_ref, s8_ref, 24, tot - 24, -2, 2)

    chunk = 512
    for c0 in range(0, seq_len, chunk):
        c1 = min(c0 + chunk, seq_len)
        a0, a1 = pad + c0, pad + c1
        s16 = s8_ref[a0 - 4:a1 - 4, :] + s8_ref[a0 + 4:a1 + 4, :]
        lane = lax.broadcasted_iota(jnp.int32, s16.shape, 1)
        hi_sel = jnp.where(lane < 192, s8_ref[a0:a1, :], s16)
        lo_sel = jnp.where(lane < 64, s2_ref[a0:a1, :], s4_ref[a0:a1, :])
        win = jnp.where(lane < 128, lo_sel, hi_sel)
        pooled = (win * icnt_ref[c0:c1, :] - xp_ref[a0:a1, :]).astype(BF16)
        o_ref[0, c0:c1, :] = (_dot(pooled, wbd_ref[...]) * ps_ref[...]).astype(BF16)


def _pool(fp3, icnt, wbd, ps):
    bsz, seq_len, _ = fp3.shape
    tot = seq_len + 2 * POOL_PAD
    return pl.pallas_call(
        functools.partial(_pool_kernel, seq_len=seq_len),
        grid=(bsz,),
        in_specs=[pl.BlockSpec((1, seq_len, 256), lambda b: (b, 0, 1)),
                  pl.BlockSpec(icnt.shape, lambda b: (0, 0)),
                  pl.BlockSpec(wbd.shape, lambda b: (0, 0)),
                  pl.BlockSpec(ps.shape, lambda b: (0, 0))],
        out_specs=pl.BlockSpec((1, seq_len, 256), lambda b: (b, 0, 0)),
        out_shape=jax.ShapeDtypeStruct((bsz, seq_len, 256), BF16),
        scratch_shapes=[pltpu.VMEM((tot, 256), F32)] * 4,
        compiler_params=pltpu.CompilerParams(dimension_semantics=("arbitrary",),
                                             vmem_limit_bytes=VMEM_LIMIT),
        name="pool",
    )(fp3, icnt, wbd, ps)


def _attn_kernel(q_ref, k_ref, v_ref, o_ref, kp_ref, vp_ref, *, seq_len, tq):
    lp = kp_ref.shape[0]
    pad = lp - seq_len
    kp_ref[0:seq_len, :] = k_ref[0]
    vp_ref[0:seq_len, 0:V_HEAD_DIM] = v_ref[0]
    vp_ref[0:seq_len, V_HEAD_DIM:] = jnp.ones((seq_len, V_HEAD_DIM), BF16)
    if pad:
        kp_ref[seq_len:lp, :] = jnp.zeros((pad, kp_ref.shape[1]), BF16)
        vp_ref[seq_len:lp, :] = jnp.zeros((pad, vp_ref.shape[1]), BF16)

    def attend(qs):
        outs = []
        for s in [_dot_nt(q, kp_ref[...]) for q in qs]:
            if pad:
                col = lax.broadcasted_iota(jnp.int32, (s.shape[0], LANES), 1) + (lp - LANES)
                last = jnp.where(col < seq_len, s[:, lp - LANES:], -jnp.inf)
                s = jnp.concatenate([s[:, :lp - LANES], last], axis=-1)
            p = jnp.exp2(s - jnp.max(s, axis=-1, keepdims=True)).astype(BF16)
            pv = _dot(p, vp_ref[...])
            outs.append((pv[:, :V_HEAD_DIM] * pl.reciprocal(pv[:, V_HEAD_DIM:])).astype(BF16))
        return outs

    ts = tq // ATTN_STREAMS

    def q_body(i, _):
        r0 = pl.multiple_of(jnp.minimum(i * tq, seq_len - tq), 16)
        rows = [r0 + j * ts for j in range(ATTN_STREAMS)]
        for r, o in zip(rows, attend([q_ref[0, pl.ds(r, ts), :] for r in rows])):
            o_ref[0, pl.ds(r, ts), :] = o
        return 0

    lax.fori_loop(0, pl.cdiv(seq_len, tq), q_body, 0)


def _attention(q3, k3, v3, *, tq):
    bsz, seq_len, _ = q3.shape
    assert tq <= seq_len and tq % 32 == 0
    lp = _round_up(seq_len, LANES)
    return pl.pallas_call(
        functools.partial(_attn_kernel, seq_len=seq_len, tq=tq),
        grid=(bsz, MLA_HEADS),
        in_specs=[pl.BlockSpec((1, seq_len, 256), lambda b, h: (b, 0, h)),
                  pl.BlockSpec((1, seq_len, 256), lambda b, h: (b, 0, h)),
                  pl.BlockSpec((1, seq_len, V_HEAD_DIM), lambda b, h: (b, 0, h))],
        out_specs=pl.BlockSpec((1, seq_len, V_HEAD_DIM), lambda b, h: (b, 0, h)),
        out_shape=jax.ShapeDtypeStruct((bsz, seq_len, MLA_HEADS * V_HEAD_DIM), BF16),
        scratch_shapes=[pltpu.VMEM((lp, 256), BF16), pltpu.VMEM((lp, 2 * V_HEAD_DIM), BF16)],
        compiler_params=pltpu.CompilerParams(dimension_semantics=("arbitrary", "arbitrary"),
                                             vmem_limit_bytes=VMEM_LIMIT),
        name="attention",
    )(q3, k3, v3)


def _rope_tables(seq_len, rows):
    inv = 1.0 / (ROPE_THETA ** (jnp.arange(0, QK_ROPE_DIM, 2, dtype=F32) / QK_ROPE_DIM))
    pos = (jnp.arange(rows) % seq_len).astype(F32)
    ang = pos[:, None] * inv[None, :]
    cos, sin = jnp.cos(ang), jnp.sin(ang)
    return jnp.tile(cos, (1, 4)), jnp.tile(jnp.concatenate([-sin, sin], axis=-1), (1, 2))


def _dft_tables(seq_len):
    quarter = seq_len // 4
    hp, qp = _round_up(quarter, 16), _round_up(quarter, LANES)
    step = 64
    k = jnp.arange(hp, dtype=jnp.int32)[:, None]
    a = step * jnp.arange(qp // step, dtype=jnp.int32)[None, :]
    b = jnp.arange(step, dtype=jnp.int32)[None, :]
    valid = ((k < quarter)[:, :, None] & ((a[:, :, None] + b[:, None, :]) < quarter)).reshape(hp, qp)
    unit = 2.0 * math.pi / seq_len
    mats = []
    for r in range(4):
        row = 4 * k + r
        ang_a = (row * a % seq_len).astype(F32) * unit
        ang_b = (row * b % seq_len).astype(F32) * unit
        ca, sa = jnp.cos(ang_a)[:, :, None], jnp.sin(ang_a)[:, :, None]
        cb, sb = jnp.cos(ang_b)[:, None, :], jnp.sin(ang_b)[:, None, :]
        cos = (ca * cb - sa * sb).reshape(hp, qp)
        msin = -(sa * cb + ca * sb).reshape(hp, qp)
        mats.append(jnp.concatenate([jnp.where(valid, cos, 0.0), jnp.where(valid, msin, 0.0)], axis=1))
    return jnp.stack(mats).astype(BF16)


def _channel_dft(seq_len):
    c = np.arange(FOURIER_HEAD_DIM)
    ang = 2.0 * np.pi * ((c[:, None] * c[None, :]) % FOURIER_HEAD_DIM) / FOURIER_HEAD_DIM
    norm = 1.0 / math.sqrt(seq_len * FOURIER_HEAD_DIM)
    eye = np.eye(FOURIER_HEADS)
    return jnp.asarray(np.stack([np.kron(eye, np.cos(ang) * norm), np.kron(eye, np.sin(ang) * norm)]), F32)


def _block_diag(w):
    n, g, c, d = w.shape
    out = jnp.zeros((n, g * c, g * d), w.dtype)
    for i in range(g):
        out = out.at[:, i * c:(i + 1) * c, i * d:(i + 1) * d].set(w[:, i])
    return out


def _pool_inv_count(seq_len):
    idx = np.arange(seq_len)
    cols = []
    for w in POOL_WINDOWS:
        cnt = np.clip(idx + w // 2, 0, seq_len) - np.clip(idx - w // 2, 0, seq_len)
        cols.append(np.repeat((1.0 / cnt)[:, None], POOL_GROUP_DIM, axis=1))
    return jnp.asarray(np.concatenate(cols, axis=1), F32)


def _layer_weights(w_in, w_uq, w_ukv):
    n, d_q, d_kv = w_in.shape[0], w_uq.shape[1], w_ukv.shape[1]
    sizes = (256, 256, 256, 256, d_q, d_kv, QK_ROPE_DIM, 512)
    offs = np.concatenate([[0], np.cumsum(sizes)])
    f_in, f_gate, p_in, p_gate, c_q, c_kv, k_r, a_gate = (w_in[..., offs[i]:offs[i + 1]] for i in range(8))
    half = QK_ROPE_DIM // 2
    swap = lambda a: jnp.concatenate([a[..., half:], a[..., :half]], axis=-1)
    w1 = jnp.concatenate([f_in, p_in, f_gate, p_gate, a_gate, c_q, c_kv,
                          k_r, k_r, swap(k_r), swap(k_r)], axis=-1).astype(BF16)
    uq = w_uq.reshape(n, d_q, MLA_HEADS, QK_NOPE_DIM + QK_ROPE_DIM)
    q_rope = uq[..., QK_NOPE_DIM:]
    wuq = jnp.concatenate([uq[..., :QK_NOPE_DIM].reshape(n, d_q, -1), q_rope.reshape(n, d_q, -1),
                           swap(q_rope).reshape(n, d_q, -1)], axis=-1).astype(BF16)
    ukv = w_ukv.reshape(n, d_kv, MLA_HEADS, QK_NOPE_DIM + V_HEAD_DIM)
    wukv = jnp.concatenate([ukv[..., :QK_NOPE_DIM].reshape(n, d_kv, -1),
                            ukv[..., QK_NOPE_DIM:].reshape(n, d_kv, -1)], axis=-1).astype(BF16)
    return w1, wuq, wukv


def kernel(x, meta_tokens, norm_w, w_in, fourier_w, pool_w, pool_scale, q_norm_w, w_uq, kv_norm_w, w_ukv,
           w_out, final_norm_w):
    bsz, seq, d = x.shape
    depth = norm_w.shape[0]
    seq_len = seq + N_META
    n_tok = bsz * seq_len
    tm = 256
    assert n_tok % tm == 0 and seq_len % 16 == 0 and tm % N_META == 0 and seq % FINAL_TM == 0

    cos_t, sin_t = _rope_tables(seq_len, seq_len + tm)
    mats = _dft_tables(seq_len)
    cs = _channel_dft(seq_len)
    icnt = _pool_inv_count(seq_len)
    fw = final_norm_w.reshape(1, d)

    w1, wuq, wukv = _layer_weights(w_in, w_uq, w_ukv)
    w_o = w_out.astype(BF16)
    f_bd = _block_diag(fourier_w)
    p_bd = _block_diag(pool_w).astype(BF16)

    def in_weights(l):
        return (norm_w[l].reshape(1, d), w1[l], q_norm_w[l].reshape(1, -1), wuq[l], kv_norm_w[l].reshape(1, -1),
                wukv[l], cos_t, sin_t)

    call = functools.partial(_token_call, n_tok=n_tok, tm=tm, seq_len=seq_len)
    h2, fp, sg, q, k, v = call((x.reshape(bsz * seq, d), meta_tokens.astype(x.dtype)), in_weights(0))
    for l in range(depth):
        fp3 = fp.reshape(bsz, seq_len, 512)
        yf = _fourier(fp3, cs, f_bd[l], mats).reshape(n_tok, 256)
        yp = _pool(fp3, icnt, p_bd[l], pool_scale[l].reshape(1, -1)).reshape(n_tok, 256)
        ya = _attention(q.reshape(bsz, seq_len, 1024), k.reshape(bsz, seq_len, 1024),
                        v.reshape(bsz, seq_len, 512), tq=ATTN_TQ).reshape(n_tok, 512)
        mix = (yf, yp, ya, sg)
        if l == depth - 1:
            return _final_call(mix, h2, w_o[l], fw, bsz=bsz, seq_len=seq_len, tm=FINAL_TM)
        h2, fp, sg, q, k, v = call((mix, h2, w_o[l]), in_weights(l + 1))
```

```python
import functools
import math

import numpy as np
import jax
import jax.numpy as jnp
from jax import lax
from jax.experimental import pallas as pl
from jax.experimental.pallas import tpu as pltpu

N_META = 16
FOURIER_HEADS = 4
FOURIER_HEAD_DIM = 64
POOL_WINDOWS = (2, 4, 8, 16)
POOL_GROUP_DIM = 64
MLA_HEADS = 4
QK_NOPE_DIM = 128
QK_ROPE_DIM = 64
V_HEAD_DIM = 128
ROPE_THETA = 10000.0
NORM_EPS = 1e-6

LANES = 128
POOL_PAD = 32
VMEM_LIMIT = 56 * 1024 * 1024
FINAL_TM = 1024
ATTN_STREAMS = 2
ATTN_TS = 256
ATTN_ONES = 16

F32 = jnp.float32
BF16 = jnp.bfloat16


def _round_up(a, m):
    return (a + m - 1) // m * m


def _dot(a, b):
    return jnp.dot(a, b, preferred_element_type=F32)


def _dot_nt(a, b):
    return lax.dot_general(a, b, (((1,), (1,)), ((), ())), preferred_element_type=F32)


def _rms(x, w):
    return x * lax.rsqrt(jnp.mean(x * x, axis=-1, keepdims=True) + NORM_EPS) * w


def _project_in(x, nw_ref, w1_ref, qnw_ref, wuq_ref, kvnw_ref, wukv_ref, cos_ref, sin_ref,
                fp_ref, sg_ref, q_ref, k_ref, v_ref, *, tm, seq_len, scale, d_q, d_kv):
    t = pl.program_id(0)
    n = _rms(x, nw_ref[...]).astype(BF16)

    fp_ref[...] = _dot(n, w1_ref[:, 0:512])
    g = _dot(n, w1_ref[:, 512:1536])
    sg_ref[...] = (g * pl.reciprocal(1.0 + jnp.exp(-g))).astype(BF16)

    start = pl.multiple_of(lax.rem(t * tm, seq_len), 16)
    cos = cos_ref[pl.ds(start, tm), :]
    sin = sin_ref[pl.ds(start, tm), :]

    o = 1536
    cq = _rms(_dot(n, w1_ref[:, o:o + d_q]), qnw_ref[...]).astype(BF16)
    o += d_q
    ckv = _rms(_dot(n, w1_ref[:, o:o + d_kv]), kvnw_ref[...]).astype(BF16)
    o += d_kv
    kr = _dot(n, w1_ref[:, o:o + 2 * LANES])

    qf = _dot(cq, wuq_ref[...])
    nq = MLA_HEADS * QK_NOPE_DIM
    nr = MLA_HEADS * QK_ROPE_DIM
    for h in range(MLA_HEADS):
        q_ref[:, 256 * h:256 * h + 128] = (qf[:, 128 * h:128 * h + 128] * scale).astype(BF16)
    for pair in range(MLA_HEADS // 2):
        r = qf[:, nq + 128 * pair:nq + 128 * pair + 128]
        rs = qf[:, nq + nr + 128 * pair:nq + nr + 128 * pair + 128]
        roped = ((r * cos + rs * sin) * scale).astype(BF16)
        for h in (2 * pair, 2 * pair + 1):
            q_ref[:, 256 * h + 128:256 * h + 256] = roped

    kvf = _dot(ckv, wukv_ref[...])
    v_ref[...] = kvf[:, nq:].astype(BF16)
    kroped = kr[:, :LANES] * cos + kr[:, LANES:] * sin
    lane = lax.broadcasted_iota(jnp.int32, kroped.shape, 1)
    k_lo = jnp.where(lane < QK_ROPE_DIM, kroped, 0.0).astype(BF16)
    k_hi = jnp.where(lane >= QK_ROPE_DIM, kroped, 0.0).astype(BF16)
    for h in range(MLA_HEADS):
        k_ref[:, 256 * h:256 * h + 128] = kvf[:, 128 * h:128 * h + 128].astype(BF16)
        k_ref[:, 256 * h + 128:256 * h + 256] = k_lo if h % 2 == 0 else k_hi


def _mix_residual(yf_ref, yp_ref, ya_ref, sg_ref, h_ref, w_ref):
    y = jnp.concatenate([yf_ref[...], yp_ref[...], ya_ref[...]], axis=-1)
    return h_ref[...] + _dot(y * sg_ref[...], w_ref[...])


def _embed_kernel(xw_ref, meta_ref, *refs, tm, seq_len, **kw):
    t = pl.program_id(0)
    h_ref = refs[8]
    first_pos = lax.rem(t * tm, seq_len)
    meta_group = lax.rem(seq_len - first_pos, seq_len) // N_META
    for j in range(tm // N_META):
        src = pl.multiple_of(N_META * (j - (meta_group < j).astype(jnp.int32)), N_META)
        blk = jnp.where(meta_group == j, meta_ref[...], xw_ref[pl.ds(src, N_META), :])
        h_ref[N_META * j:N_META * (j + 1), :] = blk
    _project_in(h_ref[...], *refs[:8], *refs[9:], tm=tm, seq_len=seq_len, **kw)


def _fused_kernel(yf_ref, yp_ref, ya_ref, sg_ref, h_ref, w_ref, *refs, **kw):
    h_new = _mix_residual(yf_ref, yp_ref, ya_ref, sg_ref, h_ref, w_ref)
    refs[8][...] = h_new
    _project_in(h_new, *refs[:8], *refs[9:], **kw)


def _final_kernel(yf_ref, yp_ref, ya_ref, sg_ref, h_ref, w_ref, fw_ref, o_ref):
    o_ref[0] = _rms(_mix_residual(yf_ref, yp_ref, ya_ref, sg_ref, h_ref, w_ref), fw_ref[...])


_MIX_WIDTHS = (256, 256, 512, 1024)
_IN_OUT = ((512, F32), (1024, BF16), (1024, BF16), (1024, BF16), (512, BF16))


def _token_call(src, in_w, *, n_tok, tm, seq_len):
    nw, w1, qnw, wuq, kvnw, wukv, cos_t, sin_t = in_w
    d = nw.shape[1]
    scale = float((QK_NOPE_DIM + QK_ROPE_DIM) ** -0.5 * math.log2(math.e))
    const = lambda a: pl.BlockSpec(a.shape, lambda t: (0,) * a.ndim)
    row = lambda w: pl.BlockSpec((tm, w), lambda t: (t, 0))
    kw = dict(tm=tm, seq_len=seq_len, scale=scale, d_q=qnw.shape[1], d_kv=kvnw.shape[1])
    out_specs = [row(d)] + [row(w) for w, _ in _IN_OUT]
    out_shape = [jax.ShapeDtypeStruct((n_tok, d), F32)] + [jax.ShapeDtypeStruct((n_tok, w), dt) for w, dt in _IN_OUT]
    if len(src) == 2:
        x2, meta = src
        seq = seq_len - N_META

        def window(t):
            b0, p0 = (t * tm) // seq_len, lax.rem(t * tm, seq_len)
            return pl.multiple_of(b0 * seq + jnp.maximum(p0 - N_META, 0), N_META), 0

        body, args, aliases = _embed_kernel, (x2, meta, *in_w), {}
        in_specs = [pl.BlockSpec((pl.Element(tm), pl.Element(d)), window), const(meta)]
    else:
        mix, h2, w_out = src
        body, args, aliases = _fused_kernel, (*mix, h2, w_out, *in_w), {4: 0}
        in_specs = [row(w) for w in _MIX_WIDTHS] + [row(d), const(w_out)]
    in_specs += [const(a) for a in in_w]
    return pl.pallas_call(
        functools.partial(body, **kw),
        grid=(n_tok // tm,),
        in_specs=in_specs, out_specs=out_specs, out_shape=out_shape,
        input_output_aliases=aliases,
        compiler_params=pltpu.CompilerParams(dimension_semantics=("arbitrary",),
                                             vmem_limit_bytes=VMEM_LIMIT),
        name="embed" if len(src) == 2 else "outin",
    )(*args)


def _final_call(mix, h2, w_out, fw, *, bsz, seq_len, tm):
    n_tok, d = h2.shape
    seq = seq_len - N_META
    row = lambda w: pl.BlockSpec((pl.Element(tm), pl.Element(w)),
                                 lambda b, j: (pl.multiple_of(b * seq_len + N_META + j * tm, 16), 0))
    const = lambda a: pl.BlockSpec(a.shape, lambda b, j: (0,) * a.ndim)
    return pl.pallas_call(
        _final_kernel,
        grid=(bsz, seq // tm),
        in_specs=[row(w) for w in _MIX_WIDTHS] + [row(d), const(w_out), const(fw)],
        out_specs=pl.BlockSpec((1, tm, d), lambda b, j: (b, j, 0)),
        out_shape=jax.ShapeDtypeStruct((bsz, seq, d), F32),
        compiler_params=pltpu.CompilerParams(dimension_semantics=("arbitrary", "arbitrary"),
                                             vmem_limit_bytes=VMEM_LIMIT),
        name="final",
    )(*mix, h2, w_out, fw)


def _fourier_kernel(x_ref, cs_ref, wbd_ref, mats_ref, o_ref, pq_ref, u_ref, y_ref, *, quarter):
    b = pl.program_id(0)
    qp = u_ref.shape[1]

    @pl.when(b == 0)
    def _():
        w = wbd_ref[...]
        pq_ref[:, 0:256] = jnp.dot(cs_ref[0], w, preferred_element_type=F32,
                                   precision=lax.Precision.HIGHEST).astype(BF16)
        pq_ref[:, 256:512] = jnp.dot(cs_ref[1], w, preferred_element_type=F32,
                                     precision=lax.Precision.HIGHEST).astype(BF16)
        u_ref[:, quarter:qp, :] = jnp.zeros((4, qp - quarter, u_ref.shape[2]), F32)

    x0, x1, x2, x3 = (x_ref[0, j * quarter:(j + 1) * quarter, :] for j in range(4))
    s02, s13, d02, d13 = x0 + x2, x1 + x3, x0 - x2, x1 - x3
    for j, u in enumerate((s02 + s13, s02 - s13, d02, d13)):
        u_ref[j, 0:quarter, :] = u
    pq = pq_ref[...]
    ab = [_dot(u_ref[j].astype(BF16), pq) for j in range(4)]
    re = (ab[0][:, :256], ab[2][:, :256] - ab[3][:, 256:], ab[1][:, :256], ab[2][:, :256] + ab[3][:, 256:])
    im = (ab[0][:, 256:], ab[2][:, 256:] + ab[3][:, :256], ab[1][:, 256:], ab[2][:, 256:] - ab[3][:, :256])
    for r in range(4):
        rhs = jnp.concatenate([re[r].astype(BF16), im[r].astype(BF16)], axis=0)
        y = _dot(mats_ref[r], rhs)
        for c in range(2):
            y_ref[c, pl.ds(r, quarter, stride=4), :] = y[0:quarter, c * LANES:(c + 1) * LANES]
    for c in range(2):
        o_ref[0, :, c * LANES:(c + 1) * LANES] = y_ref[c].astype(BF16)


def _fourier(fp3, cs, wbd, mats):
    bsz, seq_len, _ = fp3.shape
    quarter = seq_len // 4
    qp = mats.shape[2] // 2
    return pl.pallas_call(
        functools.partial(_fourier_kernel, quarter=quarter),
        grid=(bsz,),
        in_specs=[pl.BlockSpec((1, seq_len, 256), lambda b: (b, 0, 0)),
                  pl.BlockSpec(cs.shape, lambda b: (0, 0, 0)),
                  pl.BlockSpec(wbd.shape, lambda b: (0, 0)),
                  pl.BlockSpec(mats.shape, lambda b: (0, 0, 0), pipeline_mode=pl.Buffered(1))],
        out_specs=pl.BlockSpec((1, seq_len, 256), lambda b: (b, 0, 0)),
        out_shape=jax.ShapeDtypeStruct((bsz, seq_len, 256), BF16),
        scratch_shapes=[pltpu.VMEM((256, 512), BF16), pltpu.VMEM((4, qp, 256), F32),
                        pltpu.VMEM((2, seq_len, LANES), F32)],
        compiler_params=pltpu.CompilerParams(dimension_semantics=("arbitrary",),
                                             vmem_limit_bytes=VMEM_LIMIT),
        name="fourier",
    )(fp3, cs, wbd, mats)


def _pool_kernel(x_ref, icnt_ref, wbd_ref, ps_ref, o_ref, xp_ref, s2_ref, s4_ref, s8_ref, *, seq_len):
    pad = POOL_PAD
    tot = seq_len + 2 * pad
    zeros = jnp.zeros((pad, 256), F32)
    xp_ref[0:pad, :] = zeros
    xp_ref[pad + seq_len:tot, :] = zeros
    xp_ref[pad:pad + seq_len, :] = x_ref[0]

    def level(src, dst, lo, hi, d_lo, d_hi):
        chunk = 512
        for c0 in range(lo, hi, chunk):
            c1 = min(c0 + chunk, hi)
            dst[c0:c1, :] = src[c0 + d_lo:c1 + d_lo, :] + src[c0 + d_hi:c1 + d_hi, :]

    level(xp_ref, s2_ref, 8, tot - 8, -1, 0)
    level(s2_ref, s4_ref, 16, tot - 16, -1, 1)
    level(s4_ref, s8_ref, 24, tot - 24, -2, 2)

    chunk = 512
    for c0 in range(0, seq_len, chunk):
        c1 = min(c0 + chunk, seq_len)
        a0, a1 = pad + c0, pad + c1
        s16 = s8_ref[a0 - 4:a1 - 4, :] + s8_ref[a0 + 4:a1 + 4, :]
        lane = lax.broadcasted_iota(jnp.int32, s16.shape, 1)
        hi_sel = jnp.where(lane < 192, s8_ref[a0:a1, :], s16)
        lo_sel = jnp.where(lane < 64, s2_ref[a0:a1, :], s4_ref[a0:a1, :])
        win = jnp.where(lane < 128, lo_sel, hi_sel)
        pooled = (win * icnt_ref[c0:c1, :] - xp_ref[a0:a1, :]).astype(BF16)
        o_ref[0, c0:c1, :] = (_dot(pooled, wbd_ref[...]) * ps_ref[...]).astype(BF16)


def _pool(fp3, icnt, wbd, ps):
    bsz, seq_len, _ = fp3.shape
    tot = seq_len + 2 * POOL_PAD
    return pl.pallas_call(
        functools.partial(_pool_kernel, seq_len=seq_len),
        grid=(bsz,),
        in_specs=[pl.BlockSpec((1, seq_len, 256), lambda b: (b, 0, 1)),
                  pl.BlockSpec(icnt.shape, lambda b: (0, 0)),
                  pl.BlockSpec(wbd.shape, lambda b: (0, 0)),
                  pl.BlockSpec(ps.shape, lambda b: (0, 0))],
        out_specs=pl.BlockSpec((1, seq_len, 256), lambda b: (b, 0, 0)),
        out_shape=jax.ShapeDtypeStruct((bsz, seq_len, 256), BF16),
        scratch_shapes=[pltpu.VMEM((tot, 256), F32)] * 4,
        compiler_params=pltpu.CompilerParams(dimension_semantics=("arbitrary",),
                                             vmem_limit_bytes=VMEM_LIMIT),
        name="pool",
    )(fp3, icnt, wbd, ps)


def _attn_kernel(q_ref, k_ref, v_ref, o_ref, vpad_ref, vt_ref, *s_refs, seq_len, ts):
    lp = vt_ref.shape[1]
    pad = lp - seq_len
    sa_refs, sb_refs = s_refs[:ATTN_STREAMS], s_refs[ATTN_STREAMS:]
    vpad_ref[0:seq_len, :] = v_ref[0].astype(F32)
    if pad:
        vpad_ref[seq_len:lp, :] = jnp.zeros((pad, V_HEAD_DIM), F32)
    vt_ref[0:V_HEAD_DIM, :] = vpad_ref[...].T.astype(BF16)
    key = lax.broadcasted_iota(jnp.int32, (ATTN_ONES, lp), 1)
    vt_ref[V_HEAD_DIM:, :] = jnp.where(key < seq_len, 1.0, 0.0).astype(BF16)

    group = ts * ATTN_STREAMS
    n_groups = seq_len // group
    tail = seq_len - n_groups * group
    half_keys = (seq_len // 2) // LANES * LANES

    def scores(bufs, rows):
        for buf, r in zip(bufs, rows):
            buf[...] = _dot_nt(k_ref[0], q_ref[0, pl.ds(r, ts), :])

    def finish(ov, r):
        o = ov[:V_HEAD_DIM] * pl.reciprocal(ov[V_HEAD_DIM:V_HEAD_DIM + 1])
        o_ref[0, pl.ds(r, ts), :] = o.T.astype(BF16)

    def probs(buf):
        st = buf[...]
        p = jnp.exp2(st - jnp.max(st, axis=0, keepdims=True)).astype(BF16)
        return jnp.concatenate([p, jnp.zeros((pad, ts), BF16)], axis=0) if pad else p

    def outputs(bufs, rows):
        for buf, r in zip(bufs, rows):
            finish(_dot(vt_ref[...], probs(buf)), r)

    def rows_of(g):
        rows = [g * group + j * ts for j in range(ATTN_STREAMS)]
        return rows if isinstance(g, int) else [pl.multiple_of(r, ts) for r in rows]

    def pair(k, _):
        scores(sb_refs, rows_of(2 * k + 1))
        outputs(sa_refs, rows_of(2 * k))
        scores(sa_refs, rows_of(2 * k + 2))
        outputs(sb_refs, rows_of(2 * k + 1))
        return 0

    scores(sa_refs, rows_of(0))
    lax.fori_loop(0, n_groups // 2 - 1, pair, 0)
    scores(sb_refs, rows_of(n_groups - 1))
    outputs(sa_refs, rows_of(n_groups - 2))
    if tail:
        r = seq_len - ts
        q = q_ref[0, r:seq_len, :]
        sa_refs[0][0:half_keys, :] = _dot_nt(k_ref[0, 0:half_keys, :], q)
        sa_refs[0][half_keys:seq_len, :] = _dot_nt(k_ref[0, half_keys:seq_len, :], q)
    outputs(sb_refs, rows_of(n_groups - 1))
    if tail:
        p = probs(sa_refs[0])
        finish(_dot(vt_ref[:, 0:half_keys], p[0:half_keys]) + _dot(vt_ref[:, half_keys:], p[half_keys:]), r)


def _attention(q3, k3, v3, *, ts):
    bsz, seq_len, _ = q3.shape
    assert ts % LANES == 0 and (seq_len // (ts * ATTN_STREAMS)) % 2 == 0 and seq_len % (ts * ATTN_STREAMS) <= ts
    lp = _round_up(seq_len, LANES)
    return pl.pallas_call(
        functools.partial(_attn_kernel, seq_len=seq_len, ts=ts),
        grid=(bsz, MLA_HEADS),
        in_specs=[pl.BlockSpec((1, seq_len, 256), lambda b, h: (b, 0, h)),
                  pl.BlockSpec((1, seq_len, 256), lambda b, h: (b, 0, h)),
                  pl.BlockSpec((1, seq_len, V_HEAD_DIM), lambda b, h: (b, 0, h))],
        out_specs=pl.BlockSpec((1, seq_len, V_HEAD_DIM), lambda b, h: (b, 0, h)),
        out_shape=jax.ShapeDtypeStruct((bsz, seq_len, MLA_HEADS * V_HEAD_DIM), BF16),
        scratch_shapes=[pltpu.VMEM((lp, V_HEAD_DIM), F32), pltpu.VMEM((V_HEAD_DIM + ATTN_ONES, lp), BF16)]
        + [pltpu.VMEM((seq_len, ts), F32)] * (2 * ATTN_STREAMS),
        compiler_params=pltpu.CompilerParams(dimension_semantics=("arbitrary", "arbitrary"),
                                             vmem_limit_bytes=VMEM_LIMIT),
        name="attention",
    )(q3, k3, v3)


def _rope_tables(seq_len, rows):
    inv = 1.0 / (ROPE_THETA ** (jnp.arange(0, QK_ROPE_DIM, 2, dtype=F32) / QK_ROPE_DIM))
    pos = (jnp.arange(rows) % seq_len).astype(F32)
    ang = pos[:, None] * inv[None, :]
    cos, sin = jnp.cos(ang), jnp.sin(ang)
    return jnp.tile(cos, (1, 4)), jnp.tile(jnp.concatenate([-sin, sin], axis=-1), (1, 2))


def _dft_tables(seq_len):
    quarter = seq_len // 4
    hp, qp = _round_up(quarter, 16), _round_up(quarter, LANES)
    step = 64
    k = jnp.arange(hp, dtype=jnp.int32)[:, None]
    a = step * jnp.arange(qp // step, dtype=jnp.int32)[None, :]
    b = jnp.arange(step, dtype=jnp.int32)[None, :]
    valid = ((k < quarter)[:, :, None] & ((a[:, :, None] + b[:, None, :]) < quarter)).reshape(hp, qp)
    unit = 2.0 * math.pi / seq_len
    mats = []
    for r in range(4):
        row = 4 * k + r
        ang_a = (row * a % seq_len).astype(F32) * unit
        ang_b = (row * b % seq_len).astype(F32) * unit
        ca, sa = jnp.cos(ang_a)[:, :, None], jnp.sin(ang_a)[:, :, None]
        cb, sb = jnp.cos(ang_b)[:, None, :], jnp.sin(ang_b)[:, None, :]
        cos = (ca * cb - sa * sb).reshape(hp, qp)
        msin = -(sa * cb + ca * sb).reshape(hp, qp)
        mats.append(jnp.concatenate([jnp.where(valid, cos, 0.0), jnp.where(valid, msin, 0.0)], axis=1))
    return jnp.stack(mats).astype(BF16)


def _channel_dft(seq_len):
    c = np.arange(FOURIER_HEAD_DIM)
    ang = 2.0 * np.pi * ((c[:, None] * c[None, :]) % FOURIER_HEAD_DIM) / FOURIER_HEAD_DIM
    norm = 1.0 / math.sqrt(seq_len * FOURIER_HEAD_DIM)
    eye = np.eye(FOURIER_HEADS)
    return jnp.asarray(np.stack([np.kron(eye, np.cos(ang) * norm), np.kron(eye, np.sin(ang) * norm)]), F32)


def _block_diag(w):
    n, g, c, d = w.shape
    out = jnp.zeros((n, g * c, g * d), w.dtype)
    for i in range(g):
        out = out.at[:, i * c:(i + 1) * c, i * d:(i + 1) * d].set(w[:, i])
    return out


def _pool_inv_count(seq_len):
    idx = np.arange(seq_len)
    cols = []
    for w in POOL_WINDOWS:
        cnt = np.clip(idx + w // 2, 0, seq_len) - np.clip(idx - w // 2, 0, seq_len)
        cols.append(np.repeat((1.0 / cnt)[:, None], POOL_GROUP_DIM, axis=1))
    return jnp.asarray(np.concatenate(cols, axis=1), F32)


def _layer_weights(w_in, w_uq, w_ukv):
    n, d_q, d_kv = w_in.shape[0], w_uq.shape[1], w_ukv.shape[1]
    sizes = (256, 256, 256, 256, d_q, d_kv, QK_ROPE_DIM, 512)
    offs = np.concatenate([[0], np.cumsum(sizes)])
    f_in, f_gate, p_in, p_gate, c_q, c_kv, k_r, a_gate = (w_in[..., offs[i]:offs[i + 1]] for i in range(8))
    half = QK_ROPE_DIM // 2
    swap = lambda a: jnp.concatenate([a[..., half:], a[..., :half]], axis=-1)
    w1 = jnp.concatenate([f_in, p_in, f_gate, p_gate, a_gate, c_q, c_kv,
                          k_r, k_r, swap(k_r), swap(k_r)], axis=-1).astype(BF16)
    uq = w_uq.reshape(n, d_q, MLA_HEADS, QK_NOPE_DIM + QK_ROPE_DIM)
    q_rope = uq[..., QK_NOPE_DIM:]
    wuq = jnp.concatenate([uq[..., :QK_NOPE_DIM].reshape(n, d_q, -1), q_rope.reshape(n, d_q, -1),
                           swap(q_rope).reshape(n, d_q, -1)], axis=-1).astype(BF16)
    ukv = w_ukv.reshape(n, d_kv, MLA_HEADS, QK_NOPE_DIM + V_HEAD_DIM)
    wukv = jnp.concatenate([ukv[..., :QK_NOPE_DIM].reshape(n, d_kv, -1),
                            ukv[..., QK_NOPE_DIM:].reshape(n, d_kv, -1)], axis=-1).astype(BF16)
    return w1, wuq, wukv


def kernel(x, meta_tokens, norm_w, w_in, fourier_w, pool_w, pool_scale, q_norm_w, w_uq, kv_norm_w, w_ukv,
           w_out, final_norm_w):
    bsz, seq, d = x.shape
    depth = norm_w.shape[0]
    seq_len = seq + N_META
    n_tok = bsz * seq_len
    tm = 256
    assert n_tok % tm == 0 and seq_len % 16 == 0 and tm % N_META == 0 and seq % FINAL_TM == 0

    cos_t, sin_t = _rope_tables(seq_len, seq_len + tm)
    mats = _dft_tables(seq_len)
    cs = _channel_dft(seq_len)
    icnt = _pool_inv_count(seq_len)
    fw = final_norm_w.reshape(1, d)

    w1, wuq, wukv = _layer_weights(w_in, w_uq, w_ukv)
    w_o = w_out.astype(BF16)
    f_bd = _block_diag(fourier_w)
    p_bd = _block_diag(pool_w).astype(BF16)

    def in_weights(l):
        return (norm_w[l].reshape(1, d), w1[l], q_norm_w[l].reshape(1, -1), wuq[l], kv_norm_w[l].reshape(1, -1),
                wukv[l], cos_t, sin_t)

    call = functools.partial(_token_call, n_tok=n_tok, tm=tm, seq_len=seq_len)
    h2, fp, sg, q, k, v = call((x.reshape(bsz * seq, d), meta_tokens.astype(x.dtype)), in_weights(0))
    for l in range(depth):
        fp3 = fp.reshape(bsz, seq_len, 512)
        yf = _fourier(fp3, cs, f_bd[l], mats).reshape(n_tok, 256)
        yp = _pool(fp3, icnt, p_bd[l], pool_scale[l].reshape(1, -1)).reshape(n_tok, 256)
        ya = _attention(q.reshape(bsz, seq_len, 1024), k.reshape(bsz, seq_len, 1024),
                        v.reshape(bsz, seq_len, 512), ts=ATTN_TS).reshape(n_tok, 512)
        mix = (yf, yp, ya, sg)
        if l == depth - 1:
            return _final_call(mix, h2, w_o[l], fw, bsz=bsz, seq_len=seq_len, tm=FINAL_TM)
        h2, fp, sg, q, k, v = call((mix, h2, w_o[l]), in_weights(l + 1))
```

```python
import functools
import math

import numpy as np
import jax
import jax.numpy as jnp
from jax import lax
from jax.experimental import pallas as pl
from jax.experimental.pallas import tpu as pltpu

N_META = 16
FOURIER_HEADS = 4
FOURIER_HEAD_DIM = 64
POOL_WINDOWS = (2, 4, 8, 16)
POOL_GROUP_DIM = 64
MLA_HEADS = 4
QK_NOPE_DIM = 128
QK_ROPE_DIM = 64
V_HEAD_DIM = 128
ROPE_THETA = 10000.0
NORM_EPS = 1e-6

LANES = 128
POOL_PAD = 32
VMEM_LIMIT = 56 * 1024 * 1024
FINAL_TM = 1024
ATTN_STREAMS = 2
ATTN_TS = 256
ATTN_ONES = 16

F32 = jnp.float32
BF16 = jnp.bfloat16


def _round_up(a, m):
    return (a + m - 1) // m * m


def _dot(a, b):
    return jnp.dot(a, b, preferred_element_type=F32)


def _dot_nt(a, b):
    return lax.dot_general(a, b, (((1,), (1,)), ((), ())), preferred_element_type=F32)


def _rms(x, w):
    return x * lax.rsqrt(jnp.mean(x * x, axis=-1, keepdims=True) + NORM_EPS) * w


def _project_in(x, nw_ref, w1_ref, qnw_ref, wuq_ref, kvnw_ref, wukv_ref, cos_ref, sin_ref,
                fp_ref, sg_ref, q_ref, k_ref, v_ref, *, tm, seq_len, scale, d_q, d_kv):
    t = pl.program_id(0)
    inv = lax.rsqrt(jnp.mean(x * x, axis=-1, keepdims=True) + NORM_EPS)
    n = (x * nw_ref[...]).astype(BF16)

    fp_ref[...] = _dot(n, w1_ref[:, 0:512]) * inv
    g = _dot(n, w1_ref[:, 512:1536]) * inv
    sg_ref[...] = (g * pl.reciprocal(1.0 + jnp.exp(-g))).astype(BF16)

    start = pl.multiple_of(lax.rem(t * tm, seq_len), 16)
    cos = cos_ref[pl.ds(start, tm), :]
    sin = sin_ref[pl.ds(start, tm), :]

    def latent(lo, width, w_ref):
        u = _dot(n, w1_ref[:, lo:lo + width])
        ms = jnp.mean(u * u, axis=-1, keepdims=True)
        return (u * w_ref[...]).astype(BF16), inv * lax.rsqrt(inv * inv * ms + NORM_EPS)

    o = 1536
    cq, q_inv = latent(o, d_q, qnw_ref)
    o += d_q
    ckv, kv_inv = latent(o, d_kv, kvnw_ref)
    o += d_kv
    kr = _dot(n, w1_ref[:, o:o + 2 * LANES]) * inv

    qf = _dot(cq, wuq_ref[...]) * (q_inv * scale)
    nq = MLA_HEADS * QK_NOPE_DIM
    nr = MLA_HEADS * QK_ROPE_DIM
    for h in range(MLA_HEADS):
        q_ref[:, 256 * h:256 * h + 128] = qf[:, 128 * h:128 * h + 128].astype(BF16)
    for pair in range(MLA_HEADS // 2):
        r = qf[:, nq + 128 * pair:nq + 128 * pair + 128]
        rs = qf[:, nq + nr + 128 * pair:nq + nr + 128 * pair + 128]
        roped = (r * cos + rs * sin).astype(BF16)
        for h in (2 * pair, 2 * pair + 1):
            q_ref[:, 256 * h + 128:256 * h + 256] = roped

    kvf = _dot(ckv, wukv_ref[...]) * kv_inv
    v_ref[...] = kvf[:, nq:].astype(BF16)
    kroped = kr[:, :LANES] * cos + kr[:, LANES:] * sin
    lane = lax.broadcasted_iota(jnp.int32, kroped.shape, 1)
    k_lo = jnp.where(lane < QK_ROPE_DIM, kroped, 0.0).astype(BF16)
    k_hi = jnp.where(lane >= QK_ROPE_DIM, kroped, 0.0).astype(BF16)
    for h in range(MLA_HEADS):
        k_ref[:, 256 * h:256 * h + 128] = kvf[:, 128 * h:128 * h + 128].astype(BF16)
        k_ref[:, 256 * h + 128:256 * h + 256] = k_lo if h % 2 == 0 else k_hi


def _mix_residual(yf_ref, yp_ref, ya_ref, sg_ref, h_ref, w_ref):
    y = jnp.concatenate([yf_ref[...], yp_ref[...], ya_ref[...]], axis=-1)
    return h_ref[...] + _dot(y * sg_ref[...], w_ref[...])


def _embed_kernel(xw_ref, meta_ref, *refs, tm, seq_len, **kw):
    t = pl.program_id(0)
    h_ref = refs[8]
    first_pos = lax.rem(t * tm, seq_len)
    meta_group = lax.rem(seq_len - first_pos, seq_len) // N_META
    for j in range(tm // N_META):
        src = pl.multiple_of(N_META * (j - (meta_group < j).astype(jnp.int32)), N_META)
        blk = jnp.where(meta_group == j, meta_ref[...], xw_ref[pl.ds(src, N_META), :])
        h_ref[N_META * j:N_META * (j + 1), :] = blk
    _project_in(h_ref[...], *refs[:8], *refs[9:], tm=tm, seq_len=seq_len, **kw)


def _fused_kernel(yf_ref, yp_ref, ya_ref, sg_ref, h_ref, w_ref, *refs, **kw):
    h_new = _mix_residual(yf_ref, yp_ref, ya_ref, sg_ref, h_ref, w_ref)
    refs[8][...] = h_new
    _project_in(h_new, *refs[:8], *refs[9:], **kw)


def _final_kernel(yf_ref, yp_ref, ya_ref, sg_ref, h_ref, w_ref, fw_ref, o_ref):
    o_ref[0] = _rms(_mix_residual(yf_ref, yp_ref, ya_ref, sg_ref, h_ref, w_ref), fw_ref[...])


_MIX_WIDTHS = (256, 256, 512, 1024)
_IN_OUT = ((512, F32), (1024, BF16), (1024, BF16), (1024, BF16), (512, BF16))


def _token_call(src, in_w, *, n_tok, tm, seq_len):
    nw, w1, qnw, wuq, kvnw, wukv, cos_t, sin_t = in_w
    d = nw.shape[1]
    scale = float((QK_NOPE_DIM + QK_ROPE_DIM) ** -0.5 * math.log2(math.e))
    const = lambda a: pl.BlockSpec(a.shape, lambda t: (0,) * a.ndim)
    row = lambda w: pl.BlockSpec((tm, w), lambda t: (t, 0))
    kw = dict(tm=tm, seq_len=seq_len, scale=scale, d_q=qnw.shape[1], d_kv=kvnw.shape[1])
    out_specs = [row(d)] + [row(w) for w, _ in _IN_OUT]
    out_shape = [jax.ShapeDtypeStruct((n_tok, d), F32)] + [jax.ShapeDtypeStruct((n_tok, w), dt) for w, dt in _IN_OUT]
    if len(src) == 2:
        x2, meta = src
        seq = seq_len - N_META

        def window(t):
            b0, p0 = (t * tm) // seq_len, lax.rem(t * tm, seq_len)
            return pl.multiple_of(b0 * seq + jnp.maximum(p0 - N_META, 0), N_META), 0

        body, args, aliases = _embed_kernel, (x2, meta, *in_w), {}
        in_specs = [pl.BlockSpec((pl.Element(tm), pl.Element(d)), window), const(meta)]
    else:
        mix, h2, w_out = src
        body, args, aliases = _fused_kernel, (*mix, h2, w_out, *in_w), {4: 0}
        in_specs = [row(w) for w in _MIX_WIDTHS] + [row(d), const(w_out)]
    in_specs += [const(a) for a in in_w]
    return pl.pallas_call(
        functools.partial(body, **kw),
        grid=(n_tok // tm,),
        in_specs=in_specs, out_specs=out_specs, out_shape=out_shape,
        input_output_aliases=aliases,
        compiler_params=pltpu.CompilerParams(dimension_semantics=("arbitrary",),
                                             vmem_limit_bytes=VMEM_LIMIT),
        name="embed" if len(src) == 2 else "outin",
    )(*args)


def _final_call(mix, h2, w_out, fw, *, bsz, seq_len, tm):
    n_tok, d = h2.shape
    seq = seq_len - N_META
    row = lambda w: pl.BlockSpec((pl.Element(tm), pl.Element(w)),
                                 lambda b, j: (pl.multiple_of(b * seq_len + N_META + j * tm, 16), 0))
    const = lambda a: pl.BlockSpec(a.shape, lambda b, j: (0,) * a.ndim)
    return pl.pallas_call(
        _final_kernel,
        grid=(bsz, seq // tm),
        in_specs=[row(w) for w in _MIX_WIDTHS] + [row(d), const(w_out), const(fw)],
        out_specs=pl.BlockSpec((1, tm, d), lambda b, j: (b, j, 0)),
        out_shape=jax.ShapeDtypeStruct((bsz, seq, d), F32),
        compiler_params=pltpu.CompilerParams(dimension_semantics=("arbitrary", "arbitrary"),
                                             vmem_limit_bytes=VMEM_LIMIT),
        name="final",
    )(*mix, h2, w_out, fw)


def _fourier_kernel(x_ref, cs_ref, wbd_ref, mats_ref, o_ref, pq_ref, u_ref, y_ref, *, quarter):
    b = pl.program_id(0)
    qp = u_ref.shape[1]

    @pl.when(b == 0)
    def _():
        w = wbd_ref[...]
        pq_ref[:, 0:256] = jnp.dot(cs_ref[0], w, preferred_element_type=F32,
                                   precision=lax.Precision.HIGHEST).astype(BF16)
        pq_ref[:, 256:512] = jnp.dot(cs_ref[1], w, preferred_element_type=F32,
                                     precision=lax.Precision.HIGHEST).astype(BF16)
        u_ref[:, quarter:qp, :] = jnp.zeros((4, qp - quarter, u_ref.shape[2]), F32)

    x0, x1, x2, x3 = (x_ref[0, j * quarter:(j + 1) * quarter, :] for j in range(4))
    s02, s13, d02, d13 = x0 + x2, x1 + x3, x0 - x2, x1 - x3
    for j, u in enumerate((s02 + s13, s02 - s13, d02, d13)):
        u_ref[j, 0:quarter, :] = u
    pq = pq_ref[...]
    ab = [_dot(u_ref[j].astype(BF16), pq) for j in range(4)]
    re = (ab[0][:, :256], ab[2][:, :256] - ab[3][:, 256:], ab[1][:, :256], ab[2][:, :256] + ab[3][:, 256:])
    im = (ab[0][:, 256:], ab[2][:, 256:] + ab[3][:, :256], ab[1][:, 256:], ab[2][:, 256:] - ab[3][:, :256])
    for r in range(4):
        rhs = jnp.concatenate([re[r].astype(BF16), im[r].astype(BF16)], axis=0)
        y = _dot(mats_ref[r], rhs)
        for c in range(2):
            y_ref[c, pl.ds(r, quarter, stride=4), :] = y[0:quarter, c * LANES:(c + 1) * LANES]
    for c in range(2):
        o_ref[0, :, c * LANES:(c + 1) * LANES] = y_ref[c].astype(BF16)


def _fourier(fp3, cs, wbd, mats):
    bsz, seq_len, _ = fp3.shape
    quarter = seq_len // 4
    qp = mats.shape[2] // 2
    return pl.pallas_call(
        functools.partial(_fourier_kernel, quarter=quarter),
        grid=(bsz,),
        in_specs=[pl.BlockSpec((1, seq_len, 256), lambda b: (b, 0, 0)),
                  pl.BlockSpec(cs.shape, lambda b: (0, 0, 0)),
                  pl.BlockSpec(wbd.shape, lambda b: (0, 0)),
                  pl.BlockSpec(mats.shape, lambda b: (0, 0, 0), pipeline_mode=pl.Buffered(1))],
        out_specs=pl.BlockSpec((1, seq_len, 256), lambda b: (b, 0, 0)),
        out_shape=jax.ShapeDtypeStruct((bsz, seq_len, 256), BF16),
        scratch_shapes=[pltpu.VMEM((256, 512), BF16), pltpu.VMEM((4, qp, 256), F32),
                        pltpu.VMEM((2, seq_len, LANES), F32)],
        compiler_params=pltpu.CompilerParams(dimension_semantics=("arbitrary",),
                                             vmem_limit_bytes=VMEM_LIMIT),
        name="fourier",
    )(fp3, cs, wbd, mats)


def _pool_kernel(x_ref, icnt_ref, wbd_ref, ps_ref, o_ref, xp_ref, s2_ref, s4_ref, s8_ref, *, seq_len):
    pad = POOL_PAD
    tot = seq_len + 2 * pad
    zeros = jnp.zeros((pad, 256), F32)
    xp_ref[0:pad, :] = zeros
    xp_ref[pad + seq_len:tot, :] = zeros
    xp_ref[pad:pad + seq_len, :] = x_ref[0]

    def level(src, dst, lo, hi, d_lo, d_hi):
        chunk = 512
        for c0 in range(lo, hi, chunk):
            c1 = min(c0 + chunk, hi)
            dst[c0:c1, :] = src[c0 + d_lo:c1 + d_lo, :] + src[c0 + d_hi:c1 + d_hi, :]

    level(xp_ref, s2_ref, 8, tot - 8, -1, 0)
    level(s2_ref, s4_ref, 16, tot - 16, -1, 1)
    level(s4_ref, s8_ref, 24, tot - 24, -2, 2)

    chunk = 512
    for c0 in range(0, seq_len, chunk):
        c1 = min(c0 + chunk, seq_len)
        a0, a1 = pad + c0, pad + c1
        s16 = s8_ref[a0 - 4:a1 - 4, :] + s8_ref[a0 + 4:a1 + 4, :]
        lane = lax.broadcasted_iota(jnp.int32, s16.shape, 1)
        hi_sel = jnp.where(lane < 192, s8_ref[a0:a1, :], s16)
        lo_sel = jnp.where(lane < 64, s2_ref[a0:a1, :], s4_ref[a0:a1, :])
        win = jnp.where(lane < 128, lo_sel, hi_sel)
        pooled = (win * icnt_ref[c0:c1, :] - xp_ref[a0:a1, :]).astype(BF16)
        o_ref[0, c0:c1, :] = (_dot(pooled, wbd_ref[...]) * ps_ref[...]).astype(BF16)


def _pool(fp3, icnt, wbd, ps):
    bsz, seq_len, _ = fp3.shape
    tot = seq_len + 2 * POOL_PAD
    return pl.pallas_call(
        functools.partial(_pool_kernel, seq_len=seq_len),
        grid=(bsz,),
        in_specs=[pl.BlockSpec((1, seq_len, 256), lambda b: (b, 0, 1)),
                  pl.BlockSpec(icnt.shape, lambda b: (0, 0)),
                  pl.BlockSpec(wbd.shape, lambda b: (0, 0)),
                  pl.BlockSpec(ps.shape, lambda b: (0, 0))],
        out_specs=pl.BlockSpec((1, seq_len, 256), lambda b: (b, 0, 0)),
        out_shape=jax.ShapeDtypeStruct((bsz, seq_len, 256), BF16),
        scratch_shapes=[pltpu.VMEM((tot, 256), F32)] * 4,
        compiler_params=pltpu.CompilerParams(dimension_semantics=("arbitrary",),
                                             vmem_limit_bytes=VMEM_LIMIT),
        name="pool",
    )(fp3, icnt, wbd, ps)


def _attn_kernel(q_ref, k_ref, v_ref, o_ref, vpad_ref, vt_ref, *s_refs, seq_len, ts):
    lp = vt_ref.shape[1]
    pad = lp - seq_len
    sa_refs, sb_refs = s_refs[:ATTN_STREAMS], s_refs[ATTN_STREAMS:]
    vpad_ref[0:seq_len, :] = v_ref[0].astype(F32)
    if pad:
        vpad_ref[seq_len:lp, :] = jnp.zeros((pad, V_HEAD_DIM), F32)
    vt_ref[0:V_HEAD_DIM, :] = vpad_ref[...].T.astype(BF16)
    key = lax.broadcasted_iota(jnp.int32, (ATTN_ONES, lp), 1)
    vt_ref[V_HEAD_DIM:, :] = jnp.where(key < seq_len, 1.0, 0.0).astype(BF16)

    group = ts * ATTN_STREAMS
    n_groups = seq_len // group
    tail = seq_len - n_groups * group
    half_keys = (seq_len // 2) // LANES * LANES

    def put(buf, *parts):
        lo = 0
        for st in parts:
            buf[lo:lo + st.shape[0], :] = st
            lo += st.shape[0]
        m = functools.reduce(jnp.maximum, [jnp.max(st, axis=0, keepdims=True) for st in parts])
        buf[seq_len:seq_len + 8, :] = jnp.broadcast_to(m, (8, ts))

    def scores(bufs, rows):
        for buf, r in zip(bufs, rows):
            put(buf, _dot_nt(k_ref[0], q_ref[0, pl.ds(r, ts), :]))

    def finish(ov, r):
        o = ov[:V_HEAD_DIM] * pl.reciprocal(ov[V_HEAD_DIM:V_HEAD_DIM + 1])
        o_ref[0, pl.ds(r, ts), :] = o.T.astype(BF16)

    def probs(buf):
        p = jnp.exp2(buf[0:seq_len, :] - buf[seq_len:seq_len + 1, :]).astype(BF16)
        return jnp.concatenate([p, jnp.zeros((pad, ts), BF16)], axis=0) if pad else p

    def outputs(bufs, rows):
        for buf, r in zip(bufs, rows):
            finish(_dot(vt_ref[...], probs(buf)), r)

    def rows_of(g):
        rows = [g * group + j * ts for j in range(ATTN_STREAMS)]
        return rows if isinstance(g, int) else [pl.multiple_of(r, ts) for r in rows]

    def pair(k, _):
        scores(sb_refs, rows_of(2 * k + 1))
        outputs(sa_refs, rows_of(2 * k))
        scores(sa_refs, rows_of(2 * k + 2))
        outputs(sb_refs, rows_of(2 * k + 1))
        return 0

    scores(sa_refs, rows_of(0))
    lax.fori_loop(0, n_groups // 2 - 1, pair, 0)
    scores(sb_refs, rows_of(n_groups - 1))
    outputs(sa_refs, rows_of(n_groups - 2))
    if tail:
        r = seq_len - ts
        q = q_ref[0, r:seq_len, :]
        put(sa_refs[0], _dot_nt(k_ref[0, 0:half_keys, :], q), _dot_nt(k_ref[0, half_keys:seq_len, :], q))
    outputs(sb_refs, rows_of(n_groups - 1))
    if tail:
        p = probs(sa_refs[0])
        finish(_dot(vt_ref[:, 0:half_keys], p[0:half_keys]) + _dot(vt_ref[:, half_keys:], p[half_keys:]), r)


def _attention(q3, k3, v3, *, ts):
    bsz, seq_len, _ = q3.shape
    assert ts % LANES == 0 and (seq_len // (ts * ATTN_STREAMS)) % 2 == 0 and seq_len % (ts * ATTN_STREAMS) <= ts
    lp = _round_up(seq_len, LANES)
    return pl.pallas_call(
        functools.partial(_attn_kernel, seq_len=seq_len, ts=ts),
        grid=(bsz, MLA_HEADS),
        in_specs=[pl.BlockSpec((1, seq_len, 256), lambda b, h: (b, 0, h)),
                  pl.BlockSpec((1, seq_len, 256), lambda b, h: (b, 0, h)),
                  pl.BlockSpec((1, seq_len, V_HEAD_DIM), lambda b, h: (b, 0, h))],
        out_specs=pl.BlockSpec((1, seq_len, V_HEAD_DIM), lambda b, h: (b, 0, h)),
        out_shape=jax.ShapeDtypeStruct((bsz, seq_len, MLA_HEADS * V_HEAD_DIM), BF16),
        scratch_shapes=[pltpu.VMEM((lp, V_HEAD_DIM), F32), pltpu.VMEM((V_HEAD_DIM + ATTN_ONES, lp), BF16)]
        + [pltpu.VMEM((seq_len + 8, ts), F32)] * (2 * ATTN_STREAMS),
        compiler_params=pltpu.CompilerParams(dimension_semantics=("arbitrary", "arbitrary"),
                                             vmem_limit_bytes=VMEM_LIMIT),
        name="attention",
    )(q3, k3, v3)


def _rope_tables(seq_len, rows):
    inv = 1.0 / (ROPE_THETA ** (jnp.arange(0, QK_ROPE_DIM, 2, dtype=F32) / QK_ROPE_DIM))
    pos = (jnp.arange(rows) % seq_len).astype(F32)
    ang = pos[:, None] * inv[None, :]
    cos, sin = jnp.cos(ang), jnp.sin(ang)
    return jnp.tile(cos, (1, 4)), jnp.tile(jnp.concatenate([-sin, sin], axis=-1), (1, 2))


def _dft_tables(seq_len):
    quarter = seq_len // 4
    hp, qp = _round_up(quarter, 16), _round_up(quarter, LANES)
    step = 64
    k = jnp.arange(hp, dtype=jnp.int32)[:, None]
    a = step * jnp.arange(qp // step, dtype=jnp.int32)[None, :]
    b = jnp.arange(step, dtype=jnp.int32)[None, :]
    valid = ((k < quarter)[:, :, None] & ((a[:, :, None] + b[:, None, :]) < quarter)).reshape(hp, qp)
    unit = 2.0 * math.pi / seq_len
    mats = []
    for r in range(4):
        row = 4 * k + r
        ang_a = (row * a % seq_len).astype(F32) * unit
        ang_b = (row * b % seq_len).astype(F32) * unit
        ca, sa = jnp.cos(ang_a)[:, :, None], jnp.sin(ang_a)[:, :, None]
        cb, sb = jnp.cos(ang_b)[:, None, :], jnp.sin(ang_b)[:, None, :]
        cos = (ca * cb - sa * sb).reshape(hp, qp)
        msin = -(sa * cb + ca * sb).reshape(hp, qp)
        mats.append(jnp.concatenate([jnp.where(valid, cos, 0.0), jnp.where(valid, msin, 0.0)], axis=1))
    return jnp.stack(mats).astype(BF16)


def _channel_dft(seq_len):
    c = np.arange(FOURIER_HEAD_DIM)
    ang = 2.0 * np.pi * ((c[:, None] * c[None, :]) % FOURIER_HEAD_DIM) / FOURIER_HEAD_DIM
    norm = 1.0 / math.sqrt(seq_len * FOURIER_HEAD_DIM)
    eye = np.eye(FOURIER_HEADS)
    return jnp.asarray(np.stack([np.kron(eye, np.cos(ang) * norm), np.kron(eye, np.sin(ang) * norm)]), F32)


def _block_diag(w):
    n, g, c, d = w.shape
    out = jnp.zeros((n, g * c, g * d), w.dtype)
    for i in range(g):
        out = out.at[:, i * c:(i + 1) * c, i * d:(i + 1) * d].set(w[:, i])
    return out


def _pool_inv_count(seq_len):
    idx = np.arange(seq_len)
    cols = []
    for w in POOL_WINDOWS:
        cnt = np.clip(idx + w // 2, 0, seq_len) - np.clip(idx - w // 2, 0, seq_len)
        cols.append(np.repeat((1.0 / cnt)[:, None], POOL_GROUP_DIM, axis=1))
    return jnp.asarray(np.concatenate(cols, axis=1), F32)


def _layer_weights(w_in, w_uq, w_ukv):
    n, d_q, d_kv = w_in.shape[0], w_uq.shape[1], w_ukv.shape[1]
    sizes = (256, 256, 256, 256, d_q, d_kv, QK_ROPE_DIM, 512)
    offs = np.concatenate([[0], np.cumsum(sizes)])
    f_in, f_gate, p_in, p_gate, c_q, c_kv, k_r, a_gate = (w_in[..., offs[i]:offs[i + 1]] for i in range(8))
    half = QK_ROPE_DIM // 2
    swap = lambda a: jnp.concatenate([a[..., half:], a[..., :half]], axis=-1)
    w1 = jnp.concatenate([f_in, p_in, f_gate, p_gate, a_gate, c_q, c_kv,
                          k_r, k_r, swap(k_r), swap(k_r)], axis=-1).astype(BF16)
    uq = w_uq.reshape(n, d_q, MLA_HEADS, QK_NOPE_DIM + QK_ROPE_DIM)
    q_rope = uq[..., QK_NOPE_DIM:]
    wuq = jnp.concatenate([uq[..., :QK_NOPE_DIM].reshape(n, d_q, -1), q_rope.reshape(n, d_q, -1),
                           swap(q_rope).reshape(n, d_q, -1)], axis=-1).astype(BF16)
    ukv = w_ukv.reshape(n, d_kv, MLA_HEADS, QK_NOPE_DIM + V_HEAD_DIM)
    wukv = jnp.concatenate([ukv[..., :QK_NOPE_DIM].reshape(n, d_kv, -1),
                            ukv[..., QK_NOPE_DIM:].reshape(n, d_kv, -1)], axis=-1).astype(BF16)
    return w1, wuq, wukv


def kernel(x, meta_tokens, norm_w, w_in, fourier_w, pool_w, pool_scale, q_norm_w, w_uq, kv_norm_w, w_ukv,
           w_out, final_norm_w):
    bsz, seq, d = x.shape
    depth = norm_w.shape[0]
    seq_len = seq + N_META
    n_tok = bsz * seq_len
    tm = 256
    assert n_tok % tm == 0 and seq_len % 16 == 0 and tm % N_META == 0 and seq % FINAL_TM == 0

    cos_t, sin_t = _rope_tables(seq_len, seq_len + tm)
    mats = _dft_tables(seq_len)
    cs = _channel_dft(seq_len)
    icnt = _pool_inv_count(seq_len)
    fw = final_norm_w.reshape(1, d)

    w1, wuq, wukv = _layer_weights(w_in, w_uq, w_ukv)
    w_o = w_out.astype(BF16)
    f_bd = _block_diag(fourier_w)
    p_bd = _block_diag(pool_w).astype(BF16)

    def in_weights(l):
        return (norm_w[l].reshape(1, d), w1[l], q_norm_w[l].reshape(1, -1), wuq[l], kv_norm_w[l].reshape(1, -1),
                wukv[l], cos_t, sin_t)

    call = functools.partial(_token_call, n_tok=n_tok, tm=tm, seq_len=seq_len)
    h2, fp, sg, q, k, v = call((x.reshape(bsz * seq, d), meta_tokens.astype(x.dtype)), in_weights(0))
    for l in range(depth):
        fp3 = fp.reshape(bsz, seq_len, 512)
        yf = _fourier(fp3, cs, f_bd[l], mats).reshape(n_tok, 256)
        yp = _pool(fp3, icnt, p_bd[l], pool_scale[l].reshape(1, -1)).reshape(n_tok, 256)
        ya = _attention(q.reshape(bsz, seq_len, 1024), k.reshape(bsz, seq_len, 1024),
                        v.reshape(bsz, seq_len, 512), ts=ATTN_TS).reshape(n_tok, 512)
        mix = (yf, yp, ya, sg)
        if l == depth - 1:
            return _final_call(mix, h2, w_o[l], fw, bsz=bsz, seq_len=seq_len, tm=FINAL_TM)
        h2, fp, sg, q, k, v = call((mix, h2, w_o[l]), in_weights(l + 1))
```

```python
import functools
import math

import numpy as np
import jax
import jax.numpy as jnp
from jax import lax
from jax.experimental import pallas as pl
from jax.experimental.pallas import tpu as pltpu

N_META = 16
FOURIER_HEADS = 4
FOURIER_HEAD_DIM = 64
POOL_WINDOWS = (2, 4, 8, 16)
POOL_GROUP_DIM = 64
MLA_HEADS = 4
QK_NOPE_DIM = 128
QK_ROPE_DIM = 64
V_HEAD_DIM = 128
ROPE_THETA = 10000.0
NORM_EPS = 1e-6

LANES = 128
POOL_PAD = 32
VMEM_LIMIT = 56 * 1024 * 1024
EMBED_TM = 256
TOKEN_TM = 512
FINAL_TM = 1024
ATTN_STREAMS = 2
ATTN_TS = 256
ATTN_KEY_PARTS = 2
ATTN_ONES = 16

F32 = jnp.float32
BF16 = jnp.bfloat16


def _round_up(a, m):
    return (a + m - 1) // m * m


def _dot(a, b):
    return jnp.dot(a, b, preferred_element_type=F32)


def _dot_nt(a, b):
    return lax.dot_general(a, b, (((1,), (1,)), ((), ())), preferred_element_type=F32)


def _rms(x, w):
    return x * lax.rsqrt(jnp.mean(x * x, axis=-1, keepdims=True) + NORM_EPS) * w


def _project_in(x, nw_ref, w1_ref, qnw_ref, wuq_ref, kvnw_ref, wukv_ref, cos_ref, sin_ref,
                fp_ref, sg_ref, q_ref, k_ref, v_ref, *, tm, seq_len, scale, d_q, d_kv):
    t = pl.program_id(0)
    inv = lax.rsqrt(jnp.mean(x * x, axis=-1, keepdims=True) + NORM_EPS)
    n = (x * nw_ref[...]).astype(BF16)

    fp_ref[...] = _dot(n, w1_ref[:, 0:512]) * inv
    g = _dot(n, w1_ref[:, 512:1536]) * inv
    sg_ref[...] = (g * pl.reciprocal(1.0 + jnp.exp(-g))).astype(BF16)

    start = pl.multiple_of(lax.rem(t * tm, seq_len), 16)
    cos = cos_ref[pl.ds(start, tm), :]
    sin = sin_ref[pl.ds(start, tm), :]

    def latent(lo, width, w_ref):
        u = _dot(n, w1_ref[:, lo:lo + width])
        ms = jnp.mean(u * u, axis=-1, keepdims=True)
        return (u * w_ref[...]).astype(BF16), inv * lax.rsqrt(inv * inv * ms + NORM_EPS)

    o = 1536
    cq, q_inv = latent(o, d_q, qnw_ref)
    o += d_q
    ckv, kv_inv = latent(o, d_kv, kvnw_ref)
    o += d_kv
    kr = _dot(n, w1_ref[:, o:o + 2 * LANES]) * inv

    qf = _dot(cq, wuq_ref[...]) * (q_inv * scale)
    nq = MLA_HEADS * QK_NOPE_DIM
    nr = MLA_HEADS * QK_ROPE_DIM
    for h in range(MLA_HEADS):
        q_ref[:, 256 * h:256 * h + 128] = qf[:, 128 * h:128 * h + 128].astype(BF16)
    for pair in range(MLA_HEADS // 2):
        r = qf[:, nq + 128 * pair:nq + 128 * pair + 128]
        rs = qf[:, nq + nr + 128 * pair:nq + nr + 128 * pair + 128]
        roped = (r * cos + rs * sin).astype(BF16)
        for h in (2 * pair, 2 * pair + 1):
            q_ref[:, 256 * h + 128:256 * h + 256] = roped

    kvf = _dot(ckv, wukv_ref[...]) * kv_inv
    v_ref[...] = kvf[:, nq:].astype(BF16)
    kroped = kr[:, :LANES] * cos + kr[:, LANES:] * sin
    lane = lax.broadcasted_iota(jnp.int32, kroped.shape, 1)
    k_lo = jnp.where(lane < QK_ROPE_DIM, kroped, 0.0).astype(BF16)
    k_hi = jnp.where(lane >= QK_ROPE_DIM, kroped, 0.0).astype(BF16)
    for h in range(MLA_HEADS):
        k_ref[:, 256 * h:256 * h + 128] = kvf[:, 128 * h:128 * h + 128].astype(BF16)
        k_ref[:, 256 * h + 128:256 * h + 256] = k_lo if h % 2 == 0 else k_hi


def _mix_residual(yf_ref, yp_ref, ya_ref, sg_ref, h_ref, w_ref):
    y = jnp.concatenate([yf_ref[...], yp_ref[...], ya_ref[...]], axis=-1)
    return h_ref[...] + _dot(y * sg_ref[...], w_ref[...])


def _embed_kernel(xw_ref, meta_ref, *refs, tm, seq_len, **kw):
    t = pl.program_id(0)
    h_ref = refs[8]
    first_pos = lax.rem(t * tm, seq_len)
    meta_group = lax.rem(seq_len - first_pos, seq_len) // N_META
    for j in range(tm // N_META):
        src = pl.multiple_of(N_META * (j - (meta_group < j).astype(jnp.int32)), N_META)
        blk = jnp.where(meta_group == j, meta_ref[...], xw_ref[pl.ds(src, N_META), :])
        h_ref[N_META * j:N_META * (j + 1), :] = blk
    _project_in(h_ref[...], *refs[:8], *refs[9:], tm=tm, seq_len=seq_len, **kw)


def _fused_kernel(yf_ref, yp_ref, ya_ref, sg_ref, h_ref, w_ref, *refs, **kw):
    h_new = _mix_residual(yf_ref, yp_ref, ya_ref, sg_ref, h_ref, w_ref)
    refs[8][...] = h_new
    _project_in(h_new, *refs[:8], *refs[9:], **kw)


def _final_kernel(yf_ref, yp_ref, ya_ref, sg_ref, h_ref, w_ref, fw_ref, o_ref):
    o_ref[0] = _rms(_mix_residual(yf_ref, yp_ref, ya_ref, sg_ref, h_ref, w_ref), fw_ref[...])


_MIX_WIDTHS = (256, 256, 512, 1024)
_IN_OUT = ((512, F32), (1024, BF16), (1024, BF16), (1024, BF16), (512, BF16))


def _token_call(src, in_w, *, n_tok, tm, seq_len):
    nw, w1, qnw, wuq, kvnw, wukv, cos_t, sin_t = in_w
    d = nw.shape[1]
    scale = float((QK_NOPE_DIM + QK_ROPE_DIM) ** -0.5 * math.log2(math.e))
    const = lambda a: pl.BlockSpec(a.shape, lambda t: (0,) * a.ndim)
    row = lambda w: pl.BlockSpec((tm, w), lambda t: (t, 0))
    kw = dict(tm=tm, seq_len=seq_len, scale=scale, d_q=qnw.shape[1], d_kv=kvnw.shape[1])
    out_specs = [row(d)] + [row(w) for w, _ in _IN_OUT]
    out_shape = [jax.ShapeDtypeStruct((n_tok, d), F32)] + [jax.ShapeDtypeStruct((n_tok, w), dt) for w, dt in _IN_OUT]
    if len(src) == 2:
        x2, meta = src
        seq = seq_len - N_META

        def window(t):
            b0, p0 = (t * tm) // seq_len, lax.rem(t * tm, seq_len)
            return pl.multiple_of(b0 * seq + jnp.maximum(p0 - N_META, 0), N_META), 0

        body, args, aliases = _embed_kernel, (x2, meta, *in_w), {}
        in_specs = [pl.BlockSpec((pl.Element(tm), pl.Element(d)), window), const(meta)]
    else:
        mix, h2, w_out = src
        body, args, aliases = _fused_kernel, (*mix, h2, w_out, *in_w), {}
        in_specs = [row(w) for w in _MIX_WIDTHS] + [row(d), const(w_out)]
    in_specs += [const(a) for a in in_w]
    return pl.pallas_call(
        functools.partial(body, **kw),
        grid=(pl.cdiv(n_tok, tm),),
        in_specs=in_specs, out_specs=out_specs, out_shape=out_shape,
        input_output_aliases=aliases,
        compiler_params=pltpu.CompilerParams(dimension_semantics=("arbitrary",),
                                             vmem_limit_bytes=VMEM_LIMIT),
        name="embed" if len(src) == 2 else "outin",
    )(*args)


def _final_call(mix, h2, w_out, fw, *, bsz, seq_len, tm):
    n_tok, d = h2.shape
    seq = seq_len - N_META
    row = lambda w: pl.BlockSpec((pl.Element(tm), pl.Element(w)),
                                 lambda b, j: (pl.multiple_of(b * seq_len + N_META + j * tm, 16), 0))
    const = lambda a: pl.BlockSpec(a.shape, lambda b, j: (0,) * a.ndim)
    return pl.pallas_call(
        _final_kernel,
        grid=(bsz, seq // tm),
        in_specs=[row(w) for w in _MIX_WIDTHS] + [row(d), const(w_out), const(fw)],
        out_specs=pl.BlockSpec((1, tm, d), lambda b, j: (b, j, 0)),
        out_shape=jax.ShapeDtypeStruct((bsz, seq, d), F32),
        compiler_params=pltpu.CompilerParams(dimension_semantics=("arbitrary", "arbitrary"),
                                             vmem_limit_bytes=VMEM_LIMIT),
        name="final",
    )(*mix, h2, w_out, fw)


def _fourier_kernel(x_ref, cs_ref, wbd_ref, mats_ref, o_ref, pq_ref, u_ref, y_ref, *, quarter):
    b = pl.program_id(0)
    qp = u_ref.shape[1]

    @pl.when(b == 0)
    def _():
        w = wbd_ref[...]
        pq_ref[:, 0:256] = jnp.dot(cs_ref[0], w, preferred_element_type=F32,
                                   precision=lax.Precision.HIGHEST).astype(BF16)
        pq_ref[:, 256:512] = jnp.dot(cs_ref[1], w, preferred_element_type=F32,
                                     precision=lax.Precision.HIGHEST).astype(BF16)
        u_ref[:, quarter:qp, :] = jnp.zeros((4, qp - quarter, u_ref.shape[2]), F32)

    x0, x1, x2, x3 = (x_ref[0, j * quarter:(j + 1) * quarter, :] for j in range(4))
    s02, s13, d02, d13 = x0 + x2, x1 + x3, x0 - x2, x1 - x3
    for j, u in enumerate((s02 + s13, s02 - s13, d02, d13)):
        u_ref[j, 0:quarter, :] = u
    pq = pq_ref[...]
    ab = [_dot(u_ref[j].astype(BF16), pq) for j in range(4)]
    re = (ab[0][:, :256], ab[2][:, :256] - ab[3][:, 256:], ab[1][:, :256], ab[2][:, :256] + ab[3][:, 256:])
    im = (ab[0][:, 256:], ab[2][:, 256:] + ab[3][:, :256], ab[1][:, 256:], ab[2][:, 256:] - ab[3][:, :256])
    for r in range(4):
        rhs = jnp.concatenate([re[r].astype(BF16), im[r].astype(BF16)], axis=0)
        y = _dot(mats_ref[r], rhs)
        for c in range(2):
            y_ref[c, pl.ds(r, quarter, stride=4), :] = y[0:quarter, c * LANES:(c + 1) * LANES]
    for c in range(2):
        o_ref[0, :, c * LANES:(c + 1) * LANES] = y_ref[c].astype(BF16)


def _fourier(fp3, cs, wbd, mats):
    bsz, seq_len, _ = fp3.shape
    quarter = seq_len // 4
    qp = mats.shape[2] // 2
    return pl.pallas_call(
        functools.partial(_fourier_kernel, quarter=quarter),
        grid=(bsz,),
        in_specs=[pl.BlockSpec((1, seq_len, 256), lambda b: (b, 0, 0)),
                  pl.BlockSpec(cs.shape, lambda b: (0, 0, 0)),
                  pl.BlockSpec(wbd.shape, lambda b: (0, 0)),
                  pl.BlockSpec(mats.shape, lambda b: (0, 0, 0), pipeline_mode=pl.Buffered(1))],
        out_specs=pl.BlockSpec((1, seq_len, 256), lambda b: (b, 0, 0)),
        out_shape=jax.ShapeDtypeStruct((bsz, seq_len, 256), BF16),
        scratch_shapes=[pltpu.VMEM((256, 512), BF16), pltpu.VMEM((4, qp, 256), F32),
                        pltpu.VMEM((2, seq_len, LANES), F32)],
        compiler_params=pltpu.CompilerParams(dimension_semantics=("arbitrary",),
                                             vmem_limit_bytes=VMEM_LIMIT),
        name="fourier",
    )(fp3, cs, wbd, mats)


def _pool_kernel(x_ref, icnt_ref, wbd_ref, ps_ref, o_ref, xp_ref, s2_ref, s4_ref, s8_ref, *, seq_len):
    pad = POOL_PAD
    tot = seq_len + 2 * pad
    zeros = jnp.zeros((pad, 256), F32)
    xp_ref[0:pad, :] = zeros
    xp_ref[pad + seq_len:tot, :] = zeros
    xp_ref[pad:pad + seq_len, :] = x_ref[0]

    def level(src, dst, lo, hi, d_lo, d_hi, cols):
        chunk = 512
        for c0 in range(lo, hi, chunk):
            c1 = min(c0 + chunk, hi)
            dst[c0:c1, cols] = src[c0 + d_lo:c1 + d_lo, cols] + src[c0 + d_hi:c1 + d_hi, cols]

    every, upper = slice(0, 2 * LANES), slice(LANES, 2 * LANES)
    level(xp_ref, s2_ref, 8, tot - 8, -1, 0, every)
    level(s2_ref, s4_ref, 16, tot - 16, -1, 1, every)
    level(s4_ref, s8_ref, 24, tot - 24, -2, 2, upper)

    chunk = 512
    for c0 in range(0, seq_len, chunk):
        c1 = min(c0 + chunk, seq_len)
        a0, a1 = pad + c0, pad + c1
        s16 = s8_ref[a0 - 4:a1 - 4, upper] + s8_ref[a0 + 4:a1 + 4, upper]
        first = lax.broadcasted_iota(jnp.int32, s16.shape, 1) < POOL_GROUP_DIM
        win = jnp.concatenate([jnp.where(first, s2_ref[a0:a1, 0:LANES], s4_ref[a0:a1, 0:LANES]),
                               jnp.where(first, s8_ref[a0:a1, upper], s16)], axis=-1)
        pooled = (win * icnt_ref[c0:c1, :] - xp_ref[a0:a1, :]).astype(BF16)
        o_ref[0, c0:c1, :] = (_dot(pooled, wbd_ref[...]) * ps_ref[...]).astype(BF16)


def _pool(fp3, icnt, wbd, ps):
    bsz, seq_len, _ = fp3.shape
    tot = seq_len + 2 * POOL_PAD
    return pl.pallas_call(
        functools.partial(_pool_kernel, seq_len=seq_len),
        grid=(bsz,),
        in_specs=[pl.BlockSpec((1, seq_len, 256), lambda b: (b, 0, 1)),
                  pl.BlockSpec(icnt.shape, lambda b: (0, 0)),
                  pl.BlockSpec(wbd.shape, lambda b: (0, 0)),
                  pl.BlockSpec(ps.shape, lambda b: (0, 0))],
        out_specs=pl.BlockSpec((1, seq_len, 256), lambda b: (b, 0, 0)),
        out_shape=jax.ShapeDtypeStruct((bsz, seq_len, 256), BF16),
        scratch_shapes=[pltpu.VMEM((tot, 256), F32)] * 4,
        compiler_params=pltpu.CompilerParams(dimension_semantics=("arbitrary",),
                                             vmem_limit_bytes=VMEM_LIMIT),
        name="pool",
    )(fp3, icnt, wbd, ps)


def _attn_kernel(q_ref, k_ref, v_ref, o_ref, vpad_ref, vt_ref, *s_refs, seq_len, ts):
    lp = vt_ref.shape[1]
    pad = lp - seq_len
    sa_refs, sb_refs = s_refs[:ATTN_STREAMS], s_refs[ATTN_STREAMS:]
    vpad_ref[0:seq_len, :] = v_ref[0].astype(F32)
    if pad:
        vpad_ref[seq_len:lp, :] = jnp.zeros((pad, V_HEAD_DIM), F32)
    vt_ref[0:V_HEAD_DIM, :] = vpad_ref[...].T.astype(BF16)
    key = lax.broadcasted_iota(jnp.int32, (ATTN_ONES, lp), 1)
    vt_ref[V_HEAD_DIM:, :] = jnp.where(key < seq_len, 1.0, 0.0).astype(BF16)

    group = ts * ATTN_STREAMS
    n_groups = seq_len // group
    tail = seq_len - n_groups * group
    half_keys = (seq_len // 2) // LANES * LANES

    def put(buf, *parts):
        lo = 0
        for st in parts:
            buf[lo:lo + st.shape[0], :] = st
            lo += st.shape[0]
        m = functools.reduce(jnp.maximum, [jnp.max(st, axis=0, keepdims=True) for st in parts])
        buf[seq_len:seq_len + 8, :] = jnp.broadcast_to(m, (8, ts))

    def scores(bufs, rows):
        for buf, r in zip(bufs, rows):
            put(buf, _dot_nt(k_ref[0], q_ref[0, pl.ds(r, ts), :]))

    def finish(ov, r):
        o = ov[:V_HEAD_DIM] * pl.reciprocal(ov[V_HEAD_DIM:V_HEAD_DIM + 1])
        o_ref[0, pl.ds(r, ts), :] = o.T.astype(BF16)

    def weighted(buf):
        m = buf[seq_len:seq_len + 1, :]
        bounds = [seq_len * i // ATTN_KEY_PARTS // LANES * LANES for i in range(ATTN_KEY_PARTS)] + [seq_len]
        acc = None
        for lo, hi in zip(bounds[:-1], bounds[1:]):
            p = jnp.exp2(buf[lo:hi, :] - m).astype(BF16)
            if hi == seq_len and pad:
                p, hi = jnp.concatenate([p, jnp.zeros((pad, ts), BF16)], axis=0), lp
            part = _dot(vt_ref[:, lo:hi], p)
            acc = part if acc is None else acc + part
        return acc

    def outputs(bufs, rows):
        for buf, r in zip(bufs, rows):
            finish(weighted(buf), r)

    def rows_of(g):
        rows = [g * group + j * ts for j in range(ATTN_STREAMS)]
        return rows if isinstance(g, int) else [pl.multiple_of(r, ts) for r in rows]

    def pair(k, _):
        scores(sb_refs, rows_of(2 * k + 1))
        outputs(sa_refs, rows_of(2 * k))
        scores(sa_refs, rows_of(2 * k + 2))
        outputs(sb_refs, rows_of(2 * k + 1))
        return 0

    scores(sa_refs, rows_of(0))
    lax.fori_loop(0, n_groups // 2 - 1, pair, 0)
    scores(sb_refs, rows_of(n_groups - 1))
    outputs(sa_refs, rows_of(n_groups - 2))
    if tail:
        r = seq_len - ts
        q = q_ref[0, r:seq_len, :]
        put(sa_refs[0], _dot_nt(k_ref[0, 0:half_keys, :], q), _dot_nt(k_ref[0, half_keys:seq_len, :], q))
    outputs(sb_refs, rows_of(n_groups - 1))
    if tail:
        finish(weighted(sa_refs[0]), r)


def _attention(q3, k3, v3, *, ts):
    bsz, seq_len, _ = q3.shape
    assert ts % LANES == 0 and (seq_len // (ts * ATTN_STREAMS)) % 2 == 0 and seq_len % (ts * ATTN_STREAMS) <= ts
    lp = _round_up(seq_len, LANES)
    return pl.pallas_call(
        functools.partial(_attn_kernel, seq_len=seq_len, ts=ts),
        grid=(bsz, MLA_HEADS),
        in_specs=[pl.BlockSpec((1, seq_len, 256), lambda b, h: (b, 0, h)),
                  pl.BlockSpec((1, seq_len, 256), lambda b, h: (b, 0, h)),
                  pl.BlockSpec((1, seq_len, V_HEAD_DIM), lambda b, h: (b, 0, h))],
        out_specs=pl.BlockSpec((1, seq_len, V_HEAD_DIM), lambda b, h: (b, 0, h)),
        out_shape=jax.ShapeDtypeStruct((bsz, seq_len, MLA_HEADS * V_HEAD_DIM), BF16),
        scratch_shapes=[pltpu.VMEM((lp, V_HEAD_DIM), F32), pltpu.VMEM((V_HEAD_DIM + ATTN_ONES, lp), BF16)]
        + [pltpu.VMEM((seq_len + 8, ts), F32)] * (2 * ATTN_STREAMS),
        compiler_params=pltpu.CompilerParams(dimension_semantics=("arbitrary", "arbitrary"),
                                             vmem_limit_bytes=VMEM_LIMIT),
        name="attention",
    )(q3, k3, v3)


def _rope_tables(seq_len, rows):
    inv = 1.0 / (ROPE_THETA ** (jnp.arange(0, QK_ROPE_DIM, 2, dtype=F32) / QK_ROPE_DIM))
    pos = (jnp.arange(rows) % seq_len).astype(F32)
    ang = pos[:, None] * inv[None, :]
    cos, sin = jnp.cos(ang), jnp.sin(ang)
    return jnp.tile(cos, (1, 4)), jnp.tile(jnp.concatenate([-sin, sin], axis=-1), (1, 2))


def _dft_tables(seq_len):
    quarter = seq_len // 4
    hp, qp = _round_up(quarter, 16), _round_up(quarter, LANES)
    step = 64
    k = jnp.arange(hp, dtype=jnp.int32)[:, None]
    a = step * jnp.arange(qp // step, dtype=jnp.int32)[None, :]
    b = jnp.arange(step, dtype=jnp.int32)[None, :]
    valid = ((k < quarter)[:, :, None] & ((a[:, :, None] + b[:, None, :]) < quarter)).reshape(hp, qp)
    unit = 2.0 * math.pi / seq_len
    mats = []
    for r in range(4):
        row = 4 * k + r
        ang_a = (row * a % seq_len).astype(F32) * unit
        ang_b = (row * b % seq_len).astype(F32) * unit
        ca, sa = jnp.cos(ang_a)[:, :, None], jnp.sin(ang_a)[:, :, None]
        cb, sb = jnp.cos(ang_b)[:, None, :], jnp.sin(ang_b)[:, None, :]
        cos = (ca * cb - sa * sb).reshape(hp, qp)
        msin = -(sa * cb + ca * sb).reshape(hp, qp)
        mats.append(jnp.concatenate([jnp.where(valid, cos, 0.0), jnp.where(valid, msin, 0.0)], axis=1))
    return jnp.stack(mats).astype(BF16)


def _channel_dft(seq_len):
    c = np.arange(FOURIER_HEAD_DIM)
    ang = 2.0 * np.pi * ((c[:, None] * c[None, :]) % FOURIER_HEAD_DIM) / FOURIER_HEAD_DIM
    norm = 1.0 / math.sqrt(seq_len * FOURIER_HEAD_DIM)
    eye = np.eye(FOURIER_HEADS)
    return jnp.asarray(np.stack([np.kron(eye, np.cos(ang) * norm), np.kron(eye, np.sin(ang) * norm)]), F32)


def _block_diag(w):
    n, g, c, d = w.shape
    out = jnp.zeros((n, g * c, g * d), w.dtype)
    for i in range(g):
        out = out.at[:, i * c:(i + 1) * c, i * d:(i + 1) * d].set(w[:, i])
    return out


def _pool_inv_count(seq_len):
    idx = np.arange(seq_len)
    cols = []
    for w in POOL_WINDOWS:
        cnt = np.clip(idx + w // 2, 0, seq_len) - np.clip(idx - w // 2, 0, seq_len)
        cols.append(np.repeat((1.0 / cnt)[:, None], POOL_GROUP_DIM, axis=1))
    return jnp.asarray(np.concatenate(cols, axis=1), F32)


def _layer_weights(w_in, w_uq, w_ukv):
    n, d_q, d_kv = w_in.shape[0], w_uq.shape[1], w_ukv.shape[1]
    sizes = (256, 256, 256, 256, d_q, d_kv, QK_ROPE_DIM, 512)
    offs = np.concatenate([[0], np.cumsum(sizes)])
    f_in, f_gate, p_in, p_gate, c_q, c_kv, k_r, a_gate = (w_in[..., offs[i]:offs[i + 1]] for i in range(8))
    half = QK_ROPE_DIM // 2
    swap = lambda a: jnp.concatenate([a[..., half:], a[..., :half]], axis=-1)
    w1 = jnp.concatenate([f_in, p_in, f_gate, p_gate, a_gate, c_q, c_kv,
                          k_r, k_r, swap(k_r), swap(k_r)], axis=-1).astype(BF16)
    uq = w_uq.reshape(n, d_q, MLA_HEADS, QK_NOPE_DIM + QK_ROPE_DIM)
    q_rope = uq[..., QK_NOPE_DIM:]
    wuq = jnp.concatenate([uq[..., :QK_NOPE_DIM].reshape(n, d_q, -1), q_rope.reshape(n, d_q, -1),
                           swap(q_rope).reshape(n, d_q, -1)], axis=-1).astype(BF16)
    ukv = w_ukv.reshape(n, d_kv, MLA_HEADS, QK_NOPE_DIM + V_HEAD_DIM)
    wukv = jnp.concatenate([ukv[..., :QK_NOPE_DIM].reshape(n, d_kv, -1),
                            ukv[..., QK_NOPE_DIM:].reshape(n, d_kv, -1)], axis=-1).astype(BF16)
    return w1, wuq, wukv


def kernel(x, meta_tokens, norm_w, w_in, fourier_w, pool_w, pool_scale, q_norm_w, w_uq, kv_norm_w, w_ukv,
           w_out, final_norm_w):
    bsz, seq, d = x.shape
    depth = norm_w.shape[0]
    seq_len = seq + N_META
    n_tok = bsz * seq_len
    assert n_tok % EMBED_TM == 0 and seq_len % 16 == 0 and EMBED_TM % N_META == 0 and seq % FINAL_TM == 0

    cos_t, sin_t = _rope_tables(seq_len, seq_len + max(EMBED_TM, TOKEN_TM))
    mats = _dft_tables(seq_len)
    cs = _channel_dft(seq_len)
    icnt = _pool_inv_count(seq_len)
    fw = final_norm_w.reshape(1, d)

    w1, wuq, wukv = _layer_weights(w_in, w_uq, w_ukv)
    w_o = w_out.astype(BF16)
    f_bd = _block_diag(fourier_w)
    p_bd = _block_diag(pool_w).astype(BF16)

    def in_weights(l):
        return (norm_w[l].reshape(1, d), w1[l], q_norm_w[l].reshape(1, -1), wuq[l], kv_norm_w[l].reshape(1, -1),
                wukv[l], cos_t, sin_t)

    call = functools.partial(_token_call, n_tok=n_tok, seq_len=seq_len)
    h2, fp, sg, q, k, v = call((x.reshape(bsz * seq, d), meta_tokens.astype(x.dtype)), in_weights(0), tm=EMBED_TM)
    for l in range(depth):
        fp3 = fp.reshape(bsz, seq_len, 512)
        yf = _fourier(fp3, cs, f_bd[l], mats).reshape(n_tok, 256)
        yp = _pool(fp3, icnt, p_bd[l], pool_scale[l].reshape(1, -1)).reshape(n_tok, 256)
        ya = _attention(q.reshape(bsz, seq_len, 1024), k.reshape(bsz, seq_len, 1024),
                        v.reshape(bsz, seq_len, 512), ts=ATTN_TS).reshape(n_tok, 512)
        mix = (yf, yp, ya, sg)
        if l == depth - 1:
            return _final_call(mix, h2, w_o[l], fw, bsz=bsz, seq_len=seq_len, tm=FINAL_TM)
        h2, fp, sg, q, k, v = call((mix, h2, w_o[l]), in_weights(l + 1), tm=TOKEN_TM)
```

```python
import functools
import math

import numpy as np
import jax
import jax.numpy as jnp
from jax import lax
from jax.experimental import pallas as pl
from jax.experimental.pallas import tpu as pltpu

N_META = 16
FOURIER_HEADS = 4
FOURIER_HEAD_DIM = 64
POOL_WINDOWS = (2, 4, 8, 16)
POOL_GROUP_DIM = 64
MLA_HEADS = 4
QK_NOPE_DIM = 128
QK_ROPE_DIM = 64
V_HEAD_DIM = 128
ROPE_THETA = 10000.0
NORM_EPS = 1e-6

LANES = 128
POOL_PAD = 32
VMEM_LIMIT = 56 * 1024 * 1024
EMBED_TM = 256
TOKEN_TM = 512
FINAL_TM = 1024
ATTN_HEADS = 2
ATTN_STREAMS = 2
ATTN_TS = 256
ATTN_KEY_PARTS = 2
ATTN_ONES = 16

F32 = jnp.float32
BF16 = jnp.bfloat16


def _round_up(a, m):
    return (a + m - 1) // m * m


def _dot(a, b):
    return jnp.dot(a, b, preferred_element_type=F32)


def _dot_nt(a, b):
    return lax.dot_general(a, b, (((1,), (1,)), ((), ())), preferred_element_type=F32)


def _rms(x, w):
    return x * lax.rsqrt(jnp.mean(x * x, axis=-1, keepdims=True) + NORM_EPS) * w


def _project_in(x, nw_ref, w1_ref, qnw_ref, wuq_ref, kvnw_ref, wukv_ref, cos_ref, sin_ref,
                fp_ref, sg_ref, q_ref, k_ref, v_ref, *, tm, seq_len, scale, d_q, d_kv):
    t = pl.program_id(0)
    inv = lax.rsqrt(jnp.mean(x * x, axis=-1, keepdims=True) + NORM_EPS)
    n = (x * nw_ref[...]).astype(BF16)

    fp_ref[...] = _dot(n, w1_ref[:, 0:512]) * inv
    g = _dot(n, w1_ref[:, 512:1536]) * inv
    sg_ref[...] = (g * pl.reciprocal(1.0 + jnp.exp(-g))).astype(BF16)

    start = pl.multiple_of(lax.rem(t * tm, seq_len), 16)
    cos = cos_ref[pl.ds(start, tm), :]
    sin = sin_ref[pl.ds(start, tm), :]

    def latent(lo, width, w_ref):
        u = _dot(n, w1_ref[:, lo:lo + width])
        ms = jnp.mean(u * u, axis=-1, keepdims=True)
        return (u * w_ref[...]).astype(BF16), inv * lax.rsqrt(inv * inv * ms + NORM_EPS)

    o = 1536
    cq, q_inv = latent(o, d_q, qnw_ref)
    o += d_q
    ckv, kv_inv = latent(o, d_kv, kvnw_ref)
    o += d_kv
    kr = _dot(n, w1_ref[:, o:o + 2 * LANES]) * inv

    qf = _dot(cq, wuq_ref[...]) * (q_inv * scale)
    nq = MLA_HEADS * QK_NOPE_DIM
    nr = MLA_HEADS * QK_ROPE_DIM
    for h in range(MLA_HEADS):
        q_ref[:, 256 * h:256 * h + 128] = qf[:, 128 * h:128 * h + 128].astype(BF16)
    for pair in range(MLA_HEADS // 2):
        r = qf[:, nq + 128 * pair:nq + 128 * pair + 128]
        rs = qf[:, nq + nr + 128 * pair:nq + nr + 128 * pair + 128]
        roped = (r * cos + rs * sin).astype(BF16)
        for h in (2 * pair, 2 * pair + 1):
            q_ref[:, 256 * h + 128:256 * h + 256] = roped

    kvf = _dot(ckv, wukv_ref[...]) * kv_inv
    v_ref[...] = kvf[:, nq:].astype(BF16)
    kroped = kr[:, :LANES] * cos + kr[:, LANES:] * sin
    lane = lax.broadcasted_iota(jnp.int32, kroped.shape, 1)
    k_lo = jnp.where(lane < QK_ROPE_DIM, kroped, 0.0).astype(BF16)
    k_hi = jnp.where(lane >= QK_ROPE_DIM, kroped, 0.0).astype(BF16)
    for h in range(MLA_HEADS):
        k_ref[:, 256 * h:256 * h + 128] = kvf[:, 128 * h:128 * h + 128].astype(BF16)
        k_ref[:, 256 * h + 128:256 * h + 256] = k_lo if h % 2 == 0 else k_hi


def _mix_residual(yf_ref, yp_ref, ya_ref, sg_ref, h_ref, w_ref):
    y = jnp.concatenate([yf_ref[...], yp_ref[...], ya_ref[...]], axis=-1)
    return h_ref[...] + _dot(y * sg_ref[...], w_ref[...])


def _embed_kernel(xw_ref, meta_ref, *refs, tm, seq_len, **kw):
    t = pl.program_id(0)
    h_ref = refs[8]
    first_pos = lax.rem(t * tm, seq_len)
    meta_group = lax.rem(seq_len - first_pos, seq_len) // N_META
    for j in range(tm // N_META):
        src = pl.multiple_of(N_META * (j - (meta_group < j).astype(jnp.int32)), N_META)
        blk = jnp.where(meta_group == j, meta_ref[...], xw_ref[pl.ds(src, N_META), :])
        h_ref[N_META * j:N_META * (j + 1), :] = blk
    _project_in(h_ref[...], *refs[:8], *refs[9:], tm=tm, seq_len=seq_len, **kw)


def _fused_kernel(yf_ref, yp_ref, ya_ref, sg_ref, h_ref, w_ref, *refs, **kw):
    h_new = _mix_residual(yf_ref, yp_ref, ya_ref, sg_ref, h_ref, w_ref)
    refs[8][...] = h_new
    _project_in(h_new, *refs[:8], *refs[9:], **kw)


def _final_kernel(yf_ref, yp_ref, ya_ref, sg_ref, h_ref, w_ref, fw_ref, o_ref):
    o_ref[0] = _rms(_mix_residual(yf_ref, yp_ref, ya_ref, sg_ref, h_ref, w_ref), fw_ref[...])


_MIX_WIDTHS = (256, 256, 512, 1024)
_IN_OUT = ((512, F32), (1024, BF16), (1024, BF16), (1024, BF16), (512, BF16))


def _token_call(src, in_w, *, n_tok, tm, seq_len):
    nw, w1, qnw, wuq, kvnw, wukv, cos_t, sin_t = in_w
    d = nw.shape[1]
    scale = float((QK_NOPE_DIM + QK_ROPE_DIM) ** -0.5 * math.log2(math.e))
    const = lambda a: pl.BlockSpec(a.shape, lambda t: (0,) * a.ndim)
    row = lambda w: pl.BlockSpec((tm, w), lambda t: (t, 0))
    kw = dict(tm=tm, seq_len=seq_len, scale=scale, d_q=qnw.shape[1], d_kv=kvnw.shape[1])
    out_specs = [row(d)] + [row(w) for w, _ in _IN_OUT]
    out_shape = [jax.ShapeDtypeStruct((n_tok, d), F32)] + [jax.ShapeDtypeStruct((n_tok, w), dt) for w, dt in _IN_OUT]
    if len(src) == 2:
        x2, meta = src
        seq = seq_len - N_META

        def window(t):
            b0, p0 = (t * tm) // seq_len, lax.rem(t * tm, seq_len)
            return pl.multiple_of(b0 * seq + jnp.maximum(p0 - N_META, 0), N_META), 0

        body, args, aliases = _embed_kernel, (x2, meta, *in_w), {}
        in_specs = [pl.BlockSpec((pl.Element(tm), pl.Element(d)), window), const(meta)]
    else:
        mix, h2, w_out = src
        body, args, aliases = _fused_kernel, (*mix, h2, w_out, *in_w), {}
        in_specs = [row(w) for w in _MIX_WIDTHS] + [row(d), const(w_out)]
    in_specs += [const(a) for a in in_w]
    return pl.pallas_call(
        functools.partial(body, **kw),
        grid=(pl.cdiv(n_tok, tm),),
        in_specs=in_specs, out_specs=out_specs, out_shape=out_shape,
        input_output_aliases=aliases,
        compiler_params=pltpu.CompilerParams(dimension_semantics=("arbitrary",),
                                             vmem_limit_bytes=VMEM_LIMIT),
        name="embed" if len(src) == 2 else "outin",
    )(*args)


def _final_call(mix, h2, w_out, fw, *, bsz, seq_len, tm):
    n_tok, d = h2.shape
    seq = seq_len - N_META
    row = lambda w: pl.BlockSpec((pl.Element(tm), pl.Element(w)),
                                 lambda b, j: (pl.multiple_of(b * seq_len + N_META + j * tm, 16), 0))
    const = lambda a: pl.BlockSpec(a.shape, lambda b, j: (0,) * a.ndim)
    return pl.pallas_call(
        _final_kernel,
        grid=(bsz, seq // tm),
        in_specs=[row(w) for w in _MIX_WIDTHS] + [row(d), const(w_out), const(fw)],
        out_specs=pl.BlockSpec((1, tm, d), lambda b, j: (b, j, 0)),
        out_shape=jax.ShapeDtypeStruct((bsz, seq, d), F32),
        compiler_params=pltpu.CompilerParams(dimension_semantics=("arbitrary", "arbitrary"),
                                             vmem_limit_bytes=VMEM_LIMIT),
        name="final",
    )(*mix, h2, w_out, fw)


def _fourier_kernel(x_ref, cs_ref, wbd_ref, mats_ref, o_ref, pq_ref, u_ref, y_ref, *, quarter):
    b = pl.program_id(0)
    qp = u_ref.shape[1]

    @pl.when(b == 0)
    def _():
        w = wbd_ref[...]
        pq_ref[:, 0:256] = jnp.dot(cs_ref[0], w, preferred_element_type=F32,
                                   precision=lax.Precision.HIGHEST).astype(BF16)
        pq_ref[:, 256:512] = jnp.dot(cs_ref[1], w, preferred_element_type=F32,
                                     precision=lax.Precision.HIGHEST).astype(BF16)
        u_ref[:, quarter:qp, :] = jnp.zeros((4, qp - quarter, u_ref.shape[2]), F32)

    x0, x1, x2, x3 = (x_ref[0, j * quarter:(j + 1) * quarter, :] for j in range(4))
    s02, s13, d02, d13 = x0 + x2, x1 + x3, x0 - x2, x1 - x3
    for j, u in enumerate((s02 + s13, s02 - s13, d02, d13)):
        u_ref[j, 0:quarter, :] = u
    pq = pq_ref[...]
    ab = [_dot(u_ref[j].astype(BF16), pq) for j in range(4)]
    re = (ab[0][:, :256], ab[2][:, :256] - ab[3][:, 256:], ab[1][:, :256], ab[2][:, :256] + ab[3][:, 256:])
    im = (ab[0][:, 256:], ab[2][:, 256:] + ab[3][:, :256], ab[1][:, 256:], ab[2][:, 256:] - ab[3][:, :256])
    for r in range(4):
        rhs = jnp.concatenate([re[r].astype(BF16), im[r].astype(BF16)], axis=0)
        y = _dot(mats_ref[r], rhs)
        for c in range(2):
            y_ref[c, pl.ds(r, quarter, stride=4), :] = y[0:quarter, c * LANES:(c + 1) * LANES]
    for c in range(2):
        o_ref[0, :, c * LANES:(c + 1) * LANES] = y_ref[c].astype(BF16)


def _fourier(fp3, cs, wbd, mats):
    bsz, seq_len, _ = fp3.shape
    quarter = seq_len // 4
    qp = mats.shape[2] // 2
    return pl.pallas_call(
        functools.partial(_fourier_kernel, quarter=quarter),
        grid=(bsz,),
        in_specs=[pl.BlockSpec((1, seq_len, 256), lambda b: (b, 0, 0)),
                  pl.BlockSpec(cs.shape, lambda b: (0, 0, 0)),
                  pl.BlockSpec(wbd.shape, lambda b: (0, 0)),
                  pl.BlockSpec(mats.shape, lambda b: (0, 0, 0), pipeline_mode=pl.Buffered(1))],
        out_specs=pl.BlockSpec((1, seq_len, 256), lambda b: (b, 0, 0)),
        out_shape=jax.ShapeDtypeStruct((bsz, seq_len, 256), BF16),
        scratch_shapes=[pltpu.VMEM((256, 512), BF16), pltpu.VMEM((4, qp, 256), F32),
                        pltpu.VMEM((2, seq_len, LANES), F32)],
        compiler_params=pltpu.CompilerParams(dimension_semantics=("arbitrary",),
                                             vmem_limit_bytes=VMEM_LIMIT),
        name="fourier",
    )(fp3, cs, wbd, mats)


def _pool_kernel(x_ref, icnt_ref, wbd_ref, ps_ref, o_ref, xp_ref, s2_ref, s4_ref, s8_ref, *, seq_len):
    pad = POOL_PAD
    tot = seq_len + 2 * pad
    zeros = jnp.zeros((pad, 256), F32)
    xp_ref[0:pad, :] = zeros
    xp_ref[pad + seq_len:tot, :] = zeros
    xp_ref[pad:pad + seq_len, :] = x_ref[0]

    def level(src, dst, lo, hi, d_lo, d_hi, cols):
        chunk = 512
        for c0 in range(lo, hi, chunk):
            c1 = min(c0 + chunk, hi)
            dst[c0:c1, cols] = src[c0 + d_lo:c1 + d_lo, cols] + src[c0 + d_hi:c1 + d_hi, cols]

    every, upper = slice(0, 2 * LANES), slice(LANES, 2 * LANES)
    level(xp_ref, s2_ref, 8, tot - 8, -1, 0, every)
    level(s2_ref, s4_ref, 16, tot - 16, -1, 1, every)
    level(s4_ref, s8_ref, 24, tot - 24, -2, 2, upper)

    chunk = 512
    for c0 in range(0, seq_len, chunk):
        c1 = min(c0 + chunk, seq_len)
        a0, a1 = pad + c0, pad + c1
        s16 = s8_ref[a0 - 4:a1 - 4, upper] + s8_ref[a0 + 4:a1 + 4, upper]
        first = lax.broadcasted_iota(jnp.int32, s16.shape, 1) < POOL_GROUP_DIM
        win = jnp.concatenate([jnp.where(first, s2_ref[a0:a1, 0:LANES], s4_ref[a0:a1, 0:LANES]),
                               jnp.where(first, s8_ref[a0:a1, upper], s16)], axis=-1)
        pooled = (win * icnt_ref[c0:c1, :] - xp_ref[a0:a1, :]).astype(BF16)
        o_ref[0, c0:c1, :] = (_dot(pooled, wbd_ref[...]) * ps_ref[...]).astype(BF16)


def _pool(fp3, icnt, wbd, ps):
    bsz, seq_len, _ = fp3.shape
    tot = seq_len + 2 * POOL_PAD
    return pl.pallas_call(
        functools.partial(_pool_kernel, seq_len=seq_len),
        grid=(bsz,),
        in_specs=[pl.BlockSpec((1, seq_len, 256), lambda b: (b, 0, 1)),
                  pl.BlockSpec(icnt.shape, lambda b: (0, 0)),
                  pl.BlockSpec(wbd.shape, lambda b: (0, 0)),
                  pl.BlockSpec(ps.shape, lambda b: (0, 0))],
        out_specs=pl.BlockSpec((1, seq_len, 256), lambda b: (b, 0, 0)),
        out_shape=jax.ShapeDtypeStruct((bsz, seq_len, 256), BF16),
        scratch_shapes=[pltpu.VMEM((tot, 256), F32)] * 4,
        compiler_params=pltpu.CompilerParams(dimension_semantics=("arbitrary",),
                                             vmem_limit_bytes=VMEM_LIMIT),
        name="pool",
    )(fp3, icnt, wbd, ps)


def _attn_kernel(q_ref, k_ref, v_ref, o_ref, vpad_ref, vt_ref, *s_refs, seq_len, ts):
    lp = vt_ref.shape[2]
    pad = lp - seq_len
    dk = q_ref.shape[2] // ATTN_HEADS
    key = lax.broadcasted_iota(jnp.int32, (ATTN_ONES, lp), 1)
    if pad:
        vpad_ref[seq_len:lp, :] = jnp.zeros((pad, V_HEAD_DIM), F32)
    for hh in range(ATTN_HEADS):
        vpad_ref[0:seq_len, :] = v_ref[0, :, hh * V_HEAD_DIM:(hh + 1) * V_HEAD_DIM].astype(F32)
        vt_ref[hh, 0:V_HEAD_DIM, :] = vpad_ref[...].T.astype(BF16)
        vt_ref[hh, V_HEAD_DIM:, :] = jnp.where(key < seq_len, 1.0, 0.0).astype(BF16)

    group = ts * ATTN_STREAMS
    n_groups = seq_len // group
    tail = seq_len - n_groups * group
    tail_row = seq_len - ts
    half_keys = (seq_len // 2) // LANES * LANES

    def put(buf, *parts):
        lo = 0
        for st in parts:
            buf[lo:lo + st.shape[0], :] = st
            lo += st.shape[0]
        m = functools.reduce(jnp.maximum, [jnp.max(st, axis=0, keepdims=True) for st in parts])
        buf[seq_len:seq_len + 8, :] = jnp.broadcast_to(m, (8, ts))

    def rows_of(g):
        rows = [g * group + j * ts for j in range(ATTN_STREAMS)]
        return rows if isinstance(g, int) else [pl.multiple_of(r, ts) for r in rows]

    def scores(bufs, hh, g):
        cols = slice(hh * dk, (hh + 1) * dk)
        for buf, r in zip(bufs, rows_of(g)):
            put(buf, _dot_nt(k_ref[0, :, cols], q_ref[0, pl.ds(r, ts), cols]))

    def tail_scores(buf, hh):
        cols = slice(hh * dk, (hh + 1) * dk)
        q = q_ref[0, tail_row:seq_len, cols]
        put(buf, _dot_nt(k_ref[0, 0:half_keys, cols], q), _dot_nt(k_ref[0, half_keys:seq_len, cols], q))

    def weighted(buf, hh):
        m = buf[seq_len:seq_len + 1, :]
        bounds = [seq_len * i // ATTN_KEY_PARTS // LANES * LANES for i in range(ATTN_KEY_PARTS)] + [seq_len]
        acc = None
        for lo, hi in zip(bounds[:-1], bounds[1:]):
            p = jnp.exp2(buf[lo:hi, :] - m).astype(BF16)
            if hi == seq_len and pad:
                p, hi = jnp.concatenate([p, jnp.zeros((pad, ts), BF16)], axis=0), lp
            part = _dot(vt_ref[hh, :, lo:hi], p)
            acc = part if acc is None else acc + part
        return acc

    def finish(buf, hh, r):
        ov = weighted(buf, hh)
        o = ov[:V_HEAD_DIM] * pl.reciprocal(ov[V_HEAD_DIM:V_HEAD_DIM + 1])
        o_ref[0, pl.ds(r, ts), hh * V_HEAD_DIM:(hh + 1) * V_HEAD_DIM] = o.T.astype(BF16)

    def outputs(bufs, hh, g):
        for buf, r in zip(bufs, rows_of(g)):
            finish(buf, hh, r)

    first, second = s_refs[:ATTN_STREAMS], s_refs[ATTN_STREAMS:]
    scores(first, 0, 0)
    for hh in range(ATTN_HEADS):
        def pair(k, _, hh=hh, first=first, second=second):
            scores(second, hh, 2 * k + 1)
            outputs(first, hh, 2 * k)
            scores(first, hh, 2 * k + 2)
            outputs(second, hh, 2 * k + 1)
            return 0

        lax.fori_loop(0, n_groups // 2 - 1, pair, 0)
        scores(second, hh, n_groups - 1)
        outputs(first, hh, n_groups - 2)
        if tail:
            tail_scores(first[0], hh)
        outputs(second, hh, n_groups - 1)
        if hh + 1 < ATTN_HEADS:
            scores(second, hh + 1, 0)
        if tail:
            finish(first[0], hh, tail_row)
        first, second = second, first


def _attention(q3, k3, v3, *, ts):
    bsz, seq_len, _ = q3.shape
    assert ts % LANES == 0 and (seq_len // (ts * ATTN_STREAMS)) % 2 == 0 and seq_len % (ts * ATTN_STREAMS) <= ts
    lp = _round_up(seq_len, LANES)
    nh = ATTN_HEADS
    return pl.pallas_call(
        functools.partial(_attn_kernel, seq_len=seq_len, ts=ts),
        grid=(bsz, MLA_HEADS // nh),
        in_specs=[pl.BlockSpec((1, seq_len, 256 * nh), lambda b, h: (b, 0, h)),
                  pl.BlockSpec((1, seq_len, 256 * nh), lambda b, h: (b, 0, h)),
                  pl.BlockSpec((1, seq_len, V_HEAD_DIM * nh), lambda b, h: (b, 0, h))],
        out_specs=pl.BlockSpec((1, seq_len, V_HEAD_DIM * nh), lambda b, h: (b, 0, h)),
        out_shape=jax.ShapeDtypeStruct((bsz, seq_len, MLA_HEADS * V_HEAD_DIM), BF16),
        scratch_shapes=[pltpu.VMEM((lp, V_HEAD_DIM), F32), pltpu.VMEM((nh, V_HEAD_DIM + ATTN_ONES, lp), BF16)]
        + [pltpu.VMEM((seq_len + 8, ts), F32)] * (2 * ATTN_STREAMS),
        compiler_params=pltpu.CompilerParams(dimension_semantics=("arbitrary", "arbitrary"),
                                             vmem_limit_bytes=VMEM_LIMIT),
        name="attention",
    )(q3, k3, v3)


def _rope_tables(seq_len, rows):
    inv = 1.0 / (ROPE_THETA ** (jnp.arange(0, QK_ROPE_DIM, 2, dtype=F32) / QK_ROPE_DIM))
    pos = (jnp.arange(rows) % seq_len).astype(F32)
    ang = pos[:, None] * inv[None, :]
    cos, sin = jnp.cos(ang), jnp.sin(ang)
    return jnp.tile(cos, (1, 4)), jnp.tile(jnp.concatenate([-sin, sin], axis=-1), (1, 2))


def _dft_tables(seq_len):
    quarter = seq_len // 4
    hp, qp = _round_up(quarter, 16), _round_up(quarter, LANES)
    step = 64
    k = jnp.arange(hp, dtype=jnp.int32)[:, None]
    a = step * jnp.arange(qp // step, dtype=jnp.int32)[None, :]
    b = jnp.arange(step, dtype=jnp.int32)[None, :]
    valid = ((k < quarter)[:, :, None] & ((a[:, :, None] + b[:, None, :]) < quarter)).reshape(hp, qp)
    unit = 2.0 * math.pi / seq_len
    mats = []
    for r in range(4):
        row = 4 * k + r
        ang_a = (row * a % seq_len).astype(F32) * unit
        ang_b = (row * b % seq_len).astype(F32) * unit
        ca, sa = jnp.cos(ang_a)[:, :, None], jnp.sin(ang_a)[:, :, None]
        cb, sb = jnp.cos(ang_b)[:, None, :], jnp.sin(ang_b)[:, None, :]
        cos = (ca * cb - sa * sb).reshape(hp, qp)
        msin = -(sa * cb + ca * sb).reshape(hp, qp)
        mats.append(jnp.concatenate([jnp.where(valid, cos, 0.0), jnp.where(valid, msin, 0.0)], axis=1))
    return jnp.stack(mats).astype(BF16)


def _channel_dft(seq_len):
    c = np.arange(FOURIER_HEAD_DIM)
    ang = 2.0 * np.pi * ((c[:, None] * c[None, :]) % FOURIER_HEAD_DIM) / FOURIER_HEAD_DIM
    norm = 1.0 / math.sqrt(seq_len * FOURIER_HEAD_DIM)
    eye = np.eye(FOURIER_HEADS)
    return jnp.asarray(np.stack([np.kron(eye, np.cos(ang) * norm), np.kron(eye, np.sin(ang) * norm)]), F32)


def _block_diag(w):
    n, g, c, d = w.shape
    out = jnp.zeros((n, g * c, g * d), w.dtype)
    for i in range(g):
        out = out.at[:, i * c:(i + 1) * c, i * d:(i + 1) * d].set(w[:, i])
    return out


def _pool_inv_count(seq_len):
    idx = np.arange(seq_len)
    cols = []
    for w in POOL_WINDOWS:
        cnt = np.clip(idx + w // 2, 0, seq_len) - np.clip(idx - w // 2, 0, seq_len)
        cols.append(np.repeat((1.0 / cnt)[:, None], POOL_GROUP_DIM, axis=1))
    return jnp.asarray(np.concatenate(cols, axis=1), F32)


def _layer_weights(w_in, w_uq, w_ukv):
    n, d_q, d_kv = w_in.shape[0], w_uq.shape[1], w_ukv.shape[1]
    sizes = (256, 256, 256, 256, d_q, d_kv, QK_ROPE_DIM, 512)
    offs = np.concatenate([[0], np.cumsum(sizes)])
    f_in, f_gate, p_in, p_gate, c_q, c_kv, k_r, a_gate = (w_in[..., offs[i]:offs[i + 1]] for i in range(8))
    half = QK_ROPE_DIM // 2
    swap = lambda a: jnp.concatenate([a[..., half:], a[..., :half]], axis=-1)
    w1 = jnp.concatenate([f_in, p_in, f_gate, p_gate, a_gate, c_q, c_kv,
                          k_r, k_r, swap(k_r), swap(k_r)], axis=-1).astype(BF16)
    uq = w_uq.reshape(n, d_q, MLA_HEADS, QK_NOPE_DIM + QK_ROPE_DIM)
    q_rope = uq[..., QK_NOPE_DIM:]
    wuq = jnp.concatenate([uq[..., :QK_NOPE_DIM].reshape(n, d_q, -1), q_rope.reshape(n, d_q, -1),
                           swap(q_rope).reshape(n, d_q, -1)], axis=-1).astype(BF16)
    ukv = w_ukv.reshape(n, d_kv, MLA_HEADS, QK_NOPE_DIM + V_HEAD_DIM)
    wukv = jnp.concatenate([ukv[..., :QK_NOPE_DIM].reshape(n, d_kv, -1),
                            ukv[..., QK_NOPE_DIM:].reshape(n, d_kv, -1)], axis=-1).astype(BF16)
    return w1, wuq, wukv


def kernel(x, meta_tokens, norm_w, w_in, fourier_w, pool_w, pool_scale, q_norm_w, w_uq, kv_norm_w, w_ukv,
           w_out, final_norm_w):
    bsz, seq, d = x.shape
    depth = norm_w.shape[0]
    seq_len = seq + N_META
    n_tok = bsz * seq_len
    assert n_tok % EMBED_TM == 0 and seq_len % 16 == 0 and EMBED_TM % N_META == 0 and seq % FINAL_TM == 0

    cos_t, sin_t = _rope_tables(seq_len, seq_len + max(EMBED_TM, TOKEN_TM))
    mats = _dft_tables(seq_len)
    cs = _channel_dft(seq_len)
    icnt = _pool_inv_count(seq_len)
    fw = final_norm_w.reshape(1, d)

    w1, wuq, wukv = _layer_weights(w_in, w_uq, w_ukv)
    w_o = w_out.astype(BF16)
    f_bd = _block_diag(fourier_w)
    p_bd = _block_diag(pool_w).astype(BF16)

    def in_weights(l):
        return (norm_w[l].reshape(1, d), w1[l], q_norm_w[l].reshape(1, -1), wuq[l], kv_norm_w[l].reshape(1, -1),
                wukv[l], cos_t, sin_t)

    call = functools.partial(_token_call, n_tok=n_tok, seq_len=seq_len)
    h2, fp, sg, q, k, v = call((x.reshape(bsz * seq, d), meta_tokens.astype(x.dtype)), in_weights(0), tm=EMBED_TM)
    for l in range(depth):
        fp3 = fp.reshape(bsz, seq_len, 512)
        yf = _fourier(fp3, cs, f_bd[l], mats).reshape(n_tok, 256)
        yp = _pool(fp3, icnt, p_bd[l], pool_scale[l].reshape(1, -1)).reshape(n_tok, 256)
        ya = _attention(q.reshape(bsz, seq_len, 1024), k.reshape(bsz, seq_len, 1024),
                        v.reshape(bsz, seq_len, 512), ts=ATTN_TS).reshape(n_tok, 512)
        mix = (yf, yp, ya, sg)
        if l == depth - 1:
            return _final_call(mix, h2, w_o[l], fw, bsz=bsz, seq_len=seq_len, tm=FINAL_TM)
        h2, fp, sg, q, k, v = call((mix, h2, w_o[l]), in_weights(l + 1), tm=TOKEN_TM)
```

```python
import functools
import math
from typing import NamedTuple

import numpy as np
import jax
import jax.numpy as jnp
from jax import lax
from jax.experimental import pallas as pl
from jax.experimental.pallas import tpu as pltpu

N_META = 16
FOURIER_HEADS = 4
FOURIER_HEAD_DIM = 64
POOL_WINDOWS = (2, 4, 8, 16)
POOL_GROUP_DIM = 64
MLA_HEADS = 4
QK_NOPE_DIM = 128
QK_ROPE_DIM = 64
V_HEAD_DIM = 128
ROPE_THETA = 10000.0
NORM_EPS = 1e-6

LANES = 128
POOL_PAD = 32
VMEM_LIMIT = 56 * 1024 * 1024
EMBED_TM = 256
TOKEN_TM = 512
FINAL_TM = 1024
ATTN_HEADS = 2
ATTN_STREAMS = 2
ATTN_TS = 256
ATTN_KEY_PARTS = 2
ATTN_ONES = 16

F32 = jnp.float32
BF16 = jnp.bfloat16


def _round_up(a, m):
    return (a + m - 1) // m * m


def _dot(a, b):
    return jnp.dot(a, b, preferred_element_type=F32)


def _dot_nt(a, b):
    return lax.dot_general(a, b, (((1,), (1,)), ((), ())), preferred_element_type=F32)


def _rms(x, w):
    return x * lax.rsqrt(jnp.mean(x * x, axis=-1, keepdims=True) + NORM_EPS) * w


class _Layer(NamedTuple):
    stack: jax.Array
    index: int

    @property
    def shape(self):
        return self.stack.shape[1:]


def _whole(a):
    if isinstance(a, _Layer):
        return pl.BlockSpec((None,) + a.shape, lambda *g: (a.index,) + (0,) * len(a.shape))
    return pl.BlockSpec(a.shape, lambda *g: (0,) * a.ndim)


def _operands(*xs):
    return [x.stack if isinstance(x, _Layer) else x for x in xs]


def _project_in(x, nw_ref, w1_ref, qnw_ref, wuq_ref, kvnw_ref, wukv_ref, cos_ref, sin_ref,
                fp_ref, sg_ref, q_ref, k_ref, v_ref, *, tm, seq_len, scale, d_q, d_kv):
    t = pl.program_id(0)
    inv = lax.rsqrt(jnp.mean(x * x, axis=-1, keepdims=True) + NORM_EPS)
    n = (x * nw_ref[...]).astype(BF16)

    fp_ref[...] = _dot(n, w1_ref[:, 0:512]) * inv
    g = _dot(n, w1_ref[:, 512:1536]) * inv
    sg_ref[...] = (g * pl.reciprocal(1.0 + jnp.exp(-g))).astype(BF16)

    start = pl.multiple_of(lax.rem(t * tm, seq_len), 16)
    cos = cos_ref[pl.ds(start, tm), :]
    sin = sin_ref[pl.ds(start, tm), :]

    def latent(lo, width, w_ref):
        u = _dot(n, w1_ref[:, lo:lo + width])
        ms = jnp.mean(u * u, axis=-1, keepdims=True)
        return (u * w_ref[...]).astype(BF16), inv * lax.rsqrt(inv * inv * ms + NORM_EPS)

    o = 1536
    cq, q_inv = latent(o, d_q, qnw_ref)
    o += d_q
    ckv, kv_inv = latent(o, d_kv, kvnw_ref)
    o += d_kv
    kr = _dot(n, w1_ref[:, o:o + 2 * LANES]) * inv

    qf = _dot(cq, wuq_ref[...]) * (q_inv * scale)
    nq = MLA_HEADS * QK_NOPE_DIM
    nr = MLA_HEADS * QK_ROPE_DIM
    for h in range(MLA_HEADS):
        q_ref[:, 256 * h:256 * h + 128] = qf[:, 128 * h:128 * h + 128].astype(BF16)
    for pair in range(MLA_HEADS // 2):
        r = qf[:, nq + 128 * pair:nq + 128 * pair + 128]
        rs = qf[:, nq + nr + 128 * pair:nq + nr + 128 * pair + 128]
        roped = (r * cos + rs * sin).astype(BF16)
        for h in (2 * pair, 2 * pair + 1):
            q_ref[:, 256 * h + 128:256 * h + 256] = roped

    kvf = _dot(ckv, wukv_ref[...]) * kv_inv
    v_ref[...] = kvf[:, nq:].astype(BF16)
    kroped = kr[:, :LANES] * cos + kr[:, LANES:] * sin
    lane = lax.broadcasted_iota(jnp.int32, kroped.shape, 1)
    k_lo = jnp.where(lane < QK_ROPE_DIM, kroped, 0.0).astype(BF16)
    k_hi = jnp.where(lane >= QK_ROPE_DIM, kroped, 0.0).astype(BF16)
    for h in range(MLA_HEADS):
        k_ref[:, 256 * h:256 * h + 128] = kvf[:, 128 * h:128 * h + 128].astype(BF16)
        k_ref[:, 256 * h + 128:256 * h + 256] = k_lo if h % 2 == 0 else k_hi


def _mix_residual(yf_ref, yp_ref, ya_ref, sg_ref, h_ref, w_ref):
    y = jnp.concatenate([yf_ref[...], yp_ref[...], ya_ref[...]], axis=-1)
    return h_ref[...] + _dot(y * sg_ref[...], w_ref[...])


def _embed_kernel(xw_ref, meta_ref, *refs, tm, seq_len, **kw):
    t = pl.program_id(0)
    h_ref = refs[8]
    first_pos = lax.rem(t * tm, seq_len)
    meta_group = lax.rem(seq_len - first_pos, seq_len) // N_META
    for j in range(tm // N_META):
        src = pl.multiple_of(N_META * (j - (meta_group < j).astype(jnp.int32)), N_META)
        blk = jnp.where(meta_group == j, meta_ref[...], xw_ref[pl.ds(src, N_META), :])
        h_ref[N_META * j:N_META * (j + 1), :] = blk
    _project_in(h_ref[...], *refs[:8], *refs[9:], tm=tm, seq_len=seq_len, **kw)


def _fused_kernel(yf_ref, yp_ref, ya_ref, sg_ref, h_ref, w_ref, *refs, **kw):
    h_new = _mix_residual(yf_ref, yp_ref, ya_ref, sg_ref, h_ref, w_ref)
    refs[8][...] = h_new
    _project_in(h_new, *refs[:8], *refs[9:], **kw)


def _final_kernel(yf_ref, yp_ref, ya_ref, sg_ref, h_ref, w_ref, fw_ref, o_ref):
    o_ref[0] = _rms(_mix_residual(yf_ref, yp_ref, ya_ref, sg_ref, h_ref, w_ref), fw_ref[...])


_MIX_WIDTHS = (256, 256, 512, 1024)
_IN_OUT = ((512, F32), (1024, BF16), (1024, BF16), (1024, BF16), (512, BF16))


def _token_call(src, in_w, *, n_tok, tm, seq_len):
    nw, w1, qnw, wuq, kvnw, wukv, cos_t, sin_t = in_w
    d = nw.shape[1]
    scale = float((QK_NOPE_DIM + QK_ROPE_DIM) ** -0.5 * math.log2(math.e))
    row = lambda w: pl.BlockSpec((tm, w), lambda t: (t, 0))
    kw = dict(tm=tm, seq_len=seq_len, scale=scale, d_q=qnw.shape[1], d_kv=kvnw.shape[1])
    out_specs = [row(d)] + [row(w) for w, _ in _IN_OUT]
    out_shape = [jax.ShapeDtypeStruct((n_tok, d), F32)] + [jax.ShapeDtypeStruct((n_tok, w), dt) for w, dt in _IN_OUT]
    if len(src) == 2:
        x2, meta = src
        seq = seq_len - N_META

        def window(t):
            b0, p0 = (t * tm) // seq_len, lax.rem(t * tm, seq_len)
            return pl.multiple_of(b0 * seq + jnp.maximum(p0 - N_META, 0), N_META), 0

        body, args, aliases = _embed_kernel, (x2, meta, *in_w), {}
        in_specs = [pl.BlockSpec((pl.Element(tm), pl.Element(d)), window), _whole(meta)]
    else:
        mix, h2, w_out = src
        body, args, aliases = _fused_kernel, (*mix, h2, w_out, *in_w), {}
        in_specs = [row(w) for w in _MIX_WIDTHS] + [row(d), _whole(w_out)]
    in_specs += [_whole(a) for a in in_w]
    return pl.pallas_call(
        functools.partial(body, **kw),
        grid=(pl.cdiv(n_tok, tm),),
        in_specs=in_specs, out_specs=out_specs, out_shape=out_shape,
        input_output_aliases=aliases,
        compiler_params=pltpu.CompilerParams(dimension_semantics=("arbitrary",),
                                             vmem_limit_bytes=VMEM_LIMIT),
        name="embed" if len(src) == 2 else "outin",
    )(*_operands(*args))


def _final_call(mix, h2, w_out, fw, *, bsz, seq_len, tm):
    n_tok, d = h2.shape
    seq = seq_len - N_META
    row = lambda w: pl.BlockSpec((pl.Element(tm), pl.Element(w)),
                                 lambda b, j: (pl.multiple_of(b * seq_len + N_META + j * tm, 16), 0))
    return pl.pallas_call(
        _final_kernel,
        grid=(bsz, seq // tm),
        in_specs=[row(w) for w in _MIX_WIDTHS] + [row(d), _whole(w_out), _whole(fw)],
        out_specs=pl.BlockSpec((1, tm, d), lambda b, j: (b, j, 0)),
        out_shape=jax.ShapeDtypeStruct((bsz, seq, d), F32),
        compiler_params=pltpu.CompilerParams(dimension_semantics=("arbitrary", "arbitrary"),
                                             vmem_limit_bytes=VMEM_LIMIT),
        name="final",
    )(*_operands(*mix, h2, w_out, fw))


def _fourier_kernel(x_ref, cs_ref, wbd_ref, mats_ref, o_ref, pq_ref, u_ref, y_ref, *, quarter):
    b = pl.program_id(0)
    qp = u_ref.shape[1]

    @pl.when(b == 0)
    def _():
        w = wbd_ref[...]
        pq_ref[:, 0:256] = jnp.dot(cs_ref[0], w, preferred_element_type=F32,
                                   precision=lax.Precision.HIGHEST).astype(BF16)
        pq_ref[:, 256:512] = jnp.dot(cs_ref[1], w, preferred_element_type=F32,
                                     precision=lax.Precision.HIGHEST).astype(BF16)
        u_ref[:, quarter:qp, :] = jnp.zeros((4, qp - quarter, u_ref.shape[2]), F32)

    x0, x1, x2, x3 = (x_ref[0, j * quarter:(j + 1) * quarter, :] for j in range(4))
    s02, s13, d02, d13 = x0 + x2, x1 + x3, x0 - x2, x1 - x3
    for j, u in enumerate((s02 + s13, s02 - s13, d02, d13)):
        u_ref[j, 0:quarter, :] = u
    pq = pq_ref[...]
    ab = [_dot(u_ref[j].astype(BF16), pq) for j in range(4)]
    re = (ab[0][:, :256], ab[2][:, :256] - ab[3][:, 256:], ab[1][:, :256], ab[2][:, :256] + ab[3][:, 256:])
    im = (ab[0][:, 256:], ab[2][:, 256:] + ab[3][:, :256], ab[1][:, 256:], ab[2][:, 256:] - ab[3][:, :256])
    for r in range(4):
        rhs = jnp.concatenate([re[r].astype(BF16), im[r].astype(BF16)], axis=0)
        y = _dot(mats_ref[r], rhs)
        for c in range(2):
            y_ref[c, pl.ds(r, quarter, stride=4), :] = y[0:quarter, c * LANES:(c + 1) * LANES]
    for c in range(2):
        o_ref[0, :, c * LANES:(c + 1) * LANES] = y_ref[c].astype(BF16)


def _fourier(fp3, cs, wbd, mats):
    bsz, seq_len, _ = fp3.shape
    quarter = seq_len // 4
    qp = mats.shape[2] // 2
    return pl.pallas_call(
        functools.partial(_fourier_kernel, quarter=quarter),
        grid=(bsz,),
        in_specs=[pl.BlockSpec((1, seq_len, 256), lambda b: (b, 0, 0)),
                  _whole(cs), _whole(wbd),
                  pl.BlockSpec(mats.shape, lambda b: (0, 0, 0), pipeline_mode=pl.Buffered(1))],
        out_specs=pl.BlockSpec((1, seq_len, 256), lambda b: (b, 0, 0)),
        out_shape=jax.ShapeDtypeStruct((bsz, seq_len, 256), BF16),
        scratch_shapes=[pltpu.VMEM((256, 512), BF16), pltpu.VMEM((4, qp, 256), F32),
                        pltpu.VMEM((2, seq_len, LANES), F32)],
        compiler_params=pltpu.CompilerParams(dimension_semantics=("arbitrary",),
                                             vmem_limit_bytes=VMEM_LIMIT),
        name="fourier",
    )(*_operands(fp3, cs, wbd, mats))


def _pool_kernel(x_ref, icnt_ref, wbd_ref, ps_ref, o_ref, xp_ref, s2_ref, s4_ref, s8_ref, *, seq_len):
    pad = POOL_PAD
    tot = seq_len + 2 * pad
    zeros = jnp.zeros((pad, 256), F32)
    xp_ref[0:pad, :] = zeros
    xp_ref[pad + seq_len:tot, :] = zeros
    xp_ref[pad:pad + seq_len, :] = x_ref[0]

    def level(src, dst, lo, hi, d_lo, d_hi, cols):
        chunk = 512
        for c0 in range(lo, hi, chunk):
            c1 = min(c0 + chunk, hi)
            dst[c0:c1, cols] = src[c0 + d_lo:c1 + d_lo, cols] + src[c0 + d_hi:c1 + d_hi, cols]

    every, upper = slice(0, 2 * LANES), slice(LANES, 2 * LANES)
    level(xp_ref, s2_ref, 8, tot - 8, -1, 0, every)
    level(s2_ref, s4_ref, 16, tot - 16, -1, 1, every)
    level(s4_ref, s8_ref, 24, tot - 24, -2, 2, upper)

    chunk = 512
    for c0 in range(0, seq_len, chunk):
        c1 = min(c0 + chunk, seq_len)
        a0, a1 = pad + c0, pad + c1
        s16 = s8_ref[a0 - 4:a1 - 4, upper] + s8_ref[a0 + 4:a1 + 4, upper]
        first = lax.broadcasted_iota(jnp.int32, s16.shape, 1) < POOL_GROUP_DIM
        win = jnp.concatenate([jnp.where(first, s2_ref[a0:a1, 0:LANES], s4_ref[a0:a1, 0:LANES]),
                               jnp.where(first, s8_ref[a0:a1, upper], s16)], axis=-1)
        pooled = (win * icnt_ref[c0:c1, :] - xp_ref[a0:a1, :]).astype(BF16)
        o_ref[0, c0:c1, :] = (_dot(pooled, wbd_ref[...]) * ps_ref[...]).astype(BF16)


def _pool(fp3, icnt, wbd, ps):
    bsz, seq_len, _ = fp3.shape
    tot = seq_len + 2 * POOL_PAD
    return pl.pallas_call(
        functools.partial(_pool_kernel, seq_len=seq_len),
        grid=(bsz,),
        in_specs=[pl.BlockSpec((1, seq_len, 256), lambda b: (b, 0, 1)),
                  _whole(icnt), _whole(wbd), _whole(ps)],
        out_specs=pl.BlockSpec((1, seq_len, 256), lambda b: (b, 0, 0)),
        out_shape=jax.ShapeDtypeStruct((bsz, seq_len, 256), BF16),
        scratch_shapes=[pltpu.VMEM((tot, 256), F32)] * 4,
        compiler_params=pltpu.CompilerParams(dimension_semantics=("arbitrary",),
                                             vmem_limit_bytes=VMEM_LIMIT),
        name="pool",
    )(*_operands(fp3, icnt, wbd, ps))


def _attn_kernel(q_ref, k_ref, v_ref, o_ref, vpad_ref, vt_ref, *s_refs, seq_len, ts):
    lp = vt_ref.shape[2]
    pad = lp - seq_len
    dk = q_ref.shape[2] // ATTN_HEADS
    key = lax.broadcasted_iota(jnp.int32, (ATTN_ONES, lp), 1)
    if pad:
        vpad_ref[seq_len:lp, :] = jnp.zeros((pad, V_HEAD_DIM), F32)
    for hh in range(ATTN_HEADS):
        vpad_ref[0:seq_len, :] = v_ref[0, :, hh * V_HEAD_DIM:(hh + 1) * V_HEAD_DIM].astype(F32)
        vt_ref[hh, 0:V_HEAD_DIM, :] = vpad_ref[...].T.astype(BF16)
        vt_ref[hh, V_HEAD_DIM:, :] = jnp.where(key < seq_len, 1.0, 0.0).astype(BF16)

    group = ts * ATTN_STREAMS
    n_groups = seq_len // group
    tail = seq_len - n_groups * group
    tail_row = seq_len - ts
    half_keys = (seq_len // 2) // LANES * LANES

    def put(buf, *parts):
        lo = 0
        for st in parts:
            buf[lo:lo + st.shape[0], :] = st
            lo += st.shape[0]
        m = functools.reduce(jnp.maximum, [jnp.max(st, axis=0, keepdims=True) for st in parts])
        buf[seq_len:seq_len + 8, :] = jnp.broadcast_to(m, (8, ts))

    def rows_of(g):
        rows = [g * group + j * ts for j in range(ATTN_STREAMS)]
        return rows if isinstance(g, int) else [pl.multiple_of(r, ts) for r in rows]

    def scores(bufs, hh, g):
        cols = slice(hh * dk, (hh + 1) * dk)
        for buf, r in zip(bufs, rows_of(g)):
            put(buf, _dot_nt(k_ref[0, :, cols], q_ref[0, pl.ds(r, ts), cols]))

    def tail_scores(buf, hh):
        cols = slice(hh * dk, (hh + 1) * dk)
        q = q_ref[0, tail_row:seq_len, cols]
        put(buf, _dot_nt(k_ref[0, 0:half_keys, cols], q), _dot_nt(k_ref[0, half_keys:seq_len, cols], q))

    def weighted(buf, hh):
        m = buf[seq_len:seq_len + 1, :]
        bounds = [seq_len * i // ATTN_KEY_PARTS // LANES * LANES for i in range(ATTN_KEY_PARTS)] + [seq_len]
        acc = None
        for lo, hi in zip(bounds[:-1], bounds[1:]):
            p = jnp.exp2(buf[lo:hi, :] - m).astype(BF16)
            if hi == seq_len and pad:
                p, hi = jnp.concatenate([p, jnp.zeros((pad, ts), BF16)], axis=0), lp
            part = _dot(vt_ref[hh, :, lo:hi], p)
            acc = part if acc is None else acc + part
        return acc

    def finish(buf, hh, r):
        ov = weighted(buf, hh)
        o = ov[:V_HEAD_DIM] * pl.reciprocal(ov[V_HEAD_DIM:V_HEAD_DIM + 1])
        o_ref[0, pl.ds(r, ts), hh * V_HEAD_DIM:(hh + 1) * V_HEAD_DIM] = o.T.astype(BF16)

    def outputs(bufs, hh, g):
        for buf, r in zip(bufs, rows_of(g)):
            finish(buf, hh, r)

    first, second = s_refs[:ATTN_STREAMS], s_refs[ATTN_STREAMS:]
    scores(first, 0, 0)
    for hh in range(ATTN_HEADS):
        def pair(k, _, hh=hh, first=first, second=second):
            scores(second, hh, 2 * k + 1)
            outputs(first, hh, 2 * k)
            scores(first, hh, 2 * k + 2)
            outputs(second, hh, 2 * k + 1)
            return 0

        lax.fori_loop(0, n_groups // 2 - 1, pair, 0)
        scores(second, hh, n_groups - 1)
        outputs(first, hh, n_groups - 2)
        if tail:
            tail_scores(first[0], hh)
        outputs(second, hh, n_groups - 1)
        if hh + 1 < ATTN_HEADS:
            scores(second, hh + 1, 0)
        if tail:
            finish(first[0], hh, tail_row)
        first, second = second, first


def _attention(q3, k3, v3, *, ts):
    bsz, seq_len, _ = q3.shape
    assert ts % LANES == 0 and (seq_len // (ts * ATTN_STREAMS)) % 2 == 0 and seq_len % (ts * ATTN_STREAMS) <= ts
    lp = _round_up(seq_len, LANES)
    nh = ATTN_HEADS
    return pl.pallas_call(
        functools.partial(_attn_kernel, seq_len=seq_len, ts=ts),
        grid=(bsz, MLA_HEADS // nh),
        in_specs=[pl.BlockSpec((1, seq_len, 256 * nh), lambda b, h: (b, 0, h)),
                  pl.BlockSpec((1, seq_len, 256 * nh), lambda b, h: (b, 0, h)),
                  pl.BlockSpec((1, seq_len, V_HEAD_DIM * nh), lambda b, h: (b, 0, h))],
        out_specs=pl.BlockSpec((1, seq_len, V_HEAD_DIM * nh), lambda b, h: (b, 0, h)),
        out_shape=jax.ShapeDtypeStruct((bsz, seq_len, MLA_HEADS * V_HEAD_DIM), BF16),
        scratch_shapes=[pltpu.VMEM((lp, V_HEAD_DIM), F32), pltpu.VMEM((nh, V_HEAD_DIM + ATTN_ONES, lp), BF16)]
        + [pltpu.VMEM((seq_len + 8, ts), F32)] * (2 * ATTN_STREAMS),
        compiler_params=pltpu.CompilerParams(dimension_semantics=("arbitrary", "arbitrary"),
                                             vmem_limit_bytes=VMEM_LIMIT),
        name="attention",
    )(q3, k3, v3)


def _rope_tables(seq_len, rows):
    inv = 1.0 / (ROPE_THETA ** (jnp.arange(0, QK_ROPE_DIM, 2, dtype=F32) / QK_ROPE_DIM))
    pos = (jnp.arange(rows) % seq_len).astype(F32)
    ang = pos[:, None] * inv[None, :]
    cos, sin = jnp.cos(ang), jnp.sin(ang)
    return jnp.tile(cos, (1, 4)), jnp.tile(jnp.concatenate([-sin, sin], axis=-1), (1, 2))


def _dft_tables(seq_len):
    quarter = seq_len // 4
    hp, qp = _round_up(quarter, 16), _round_up(quarter, LANES)
    step = 64
    k = jnp.arange(hp, dtype=jnp.int32)[:, None]
    a = step * jnp.arange(qp // step, dtype=jnp.int32)[None, :]
    b = jnp.arange(step, dtype=jnp.int32)[None, :]
    valid = ((k < quarter)[:, :, None] & ((a[:, :, None] + b[:, None, :]) < quarter)).reshape(hp, qp)
    unit = 2.0 * math.pi / seq_len
    mats = []
    for r in range(4):
        row = 4 * k + r
        ang_a = (row * a % seq_len).astype(F32) * unit
        ang_b = (row * b % seq_len).astype(F32) * unit
        ca, sa = jnp.cos(ang_a)[:, :, None], jnp.sin(ang_a)[:, :, None]
        cb, sb = jnp.cos(ang_b)[:, None, :], jnp.sin(ang_b)[:, None, :]
        cos = (ca * cb - sa * sb).reshape(hp, qp)
        msin = -(sa * cb + ca * sb).reshape(hp, qp)
        mats.append(jnp.concatenate([jnp.where(valid, cos, 0.0), jnp.where(valid, msin, 0.0)], axis=1))
    return jnp.stack(mats).astype(BF16)


def _channel_dft(seq_len):
    c = np.arange(FOURIER_HEAD_DIM)
    ang = 2.0 * np.pi * ((c[:, None] * c[None, :]) % FOURIER_HEAD_DIM) / FOURIER_HEAD_DIM
    norm = 1.0 / math.sqrt(seq_len * FOURIER_HEAD_DIM)
    eye = np.eye(FOURIER_HEADS)
    return jnp.asarray(np.stack([np.kron(eye, np.cos(ang) * norm), np.kron(eye, np.sin(ang) * norm)]), F32)


def _block_diag(w):
    n, g, c, d = w.shape
    out = jnp.zeros((n, g * c, g * d), w.dtype)
    for i in range(g):
        out = out.at[:, i * c:(i + 1) * c, i * d:(i + 1) * d].set(w[:, i])
    return out


def _pool_inv_count(seq_len):
    idx = np.arange(seq_len)
    cols = []
    for w in POOL_WINDOWS:
        cnt = np.clip(idx + w // 2, 0, seq_len) - np.clip(idx - w // 2, 0, seq_len)
        cols.append(np.repeat((1.0 / cnt)[:, None], POOL_GROUP_DIM, axis=1))
    return jnp.asarray(np.concatenate(cols, axis=1), F32)


def _layer_weights(w_in, w_uq, w_ukv):
    n, d_q, d_kv = w_in.shape[0], w_uq.shape[1], w_ukv.shape[1]
    sizes = (256, 256, 256, 256, d_q, d_kv, QK_ROPE_DIM, 512)
    offs = np.concatenate([[0], np.cumsum(sizes)])
    f_in, f_gate, p_in, p_gate, c_q, c_kv, k_r, a_gate = (w_in[..., offs[i]:offs[i + 1]] for i in range(8))
    half = QK_ROPE_DIM // 2
    swap = lambda a: jnp.concatenate([a[..., half:], a[..., :half]], axis=-1)
    w1 = jnp.concatenate([f_in, p_in, f_gate, p_gate, a_gate, c_q, c_kv,
                          k_r, k_r, swap(k_r), swap(k_r)], axis=-1).astype(BF16)
    uq = w_uq.reshape(n, d_q, MLA_HEADS, QK_NOPE_DIM + QK_ROPE_DIM)
    q_rope = uq[..., QK_NOPE_DIM:]
    wuq = jnp.concatenate([uq[..., :QK_NOPE_DIM].reshape(n, d_q, -1), q_rope.reshape(n, d_q, -1),
                           swap(q_rope).reshape(n, d_q, -1)], axis=-1).astype(BF16)
    ukv = w_ukv.reshape(n, d_kv, MLA_HEADS, QK_NOPE_DIM + V_HEAD_DIM)
    wukv = jnp.concatenate([ukv[..., :QK_NOPE_DIM].reshape(n, d_kv, -1),
                            ukv[..., QK_NOPE_DIM:].reshape(n, d_kv, -1)], axis=-1).astype(BF16)
    return w1, wuq, wukv


def kernel(x, meta_tokens, norm_w, w_in, fourier_w, pool_w, pool_scale, q_norm_w, w_uq, kv_norm_w, w_ukv,
           w_out, final_norm_w):
    bsz, seq, d = x.shape
    depth = norm_w.shape[0]
    seq_len = seq + N_META
    n_tok = bsz * seq_len
    assert n_tok % EMBED_TM == 0 and seq_len % 16 == 0 and EMBED_TM % N_META == 0 and seq % FINAL_TM == 0

    cos_t, sin_t = _rope_tables(seq_len, seq_len + max(EMBED_TM, TOKEN_TM))
    mats = _dft_tables(seq_len)
    cs = _channel_dft(seq_len)
    icnt = _pool_inv_count(seq_len)
    fw = final_norm_w.reshape(1, d)

    w1, wuq, wukv = _layer_weights(w_in, w_uq, w_ukv)
    w_o = w_out.astype(BF16)
    f_bd = _block_diag(fourier_w)
    p_bd = _block_diag(pool_w).astype(BF16)

    rows = lambda a: a.reshape(depth, 1, -1)
    nw, qnw, kvnw, ps = rows(norm_w), rows(q_norm_w), rows(kv_norm_w), rows(pool_scale)

    def in_weights(l):
        return (*(_Layer(a, l) for a in (nw, w1, qnw, wuq, kvnw, wukv)), cos_t, sin_t)

    call = functools.partial(_token_call, n_tok=n_tok, seq_len=seq_len)
    h2, fp, sg, q, k, v = call((x.reshape(bsz * seq, d), meta_tokens.astype(x.dtype)), in_weights(0), tm=EMBED_TM)
    for l in range(depth):
        fp3 = fp.reshape(bsz, seq_len, 512)
        yf = _fourier(fp3, cs, _Layer(f_bd, l), mats).reshape(n_tok, 256)
        yp = _pool(fp3, icnt, _Layer(p_bd, l), _Layer(ps, l)).reshape(n_tok, 256)
        ya = _attention(q.reshape(bsz, seq_len, 1024), k.reshape(bsz, seq_len, 1024),
                        v.reshape(bsz, seq_len, 512), ts=ATTN_TS).reshape(n_tok, 512)
        mix = (yf, yp, ya, sg)
        if l == depth - 1:
            return _final_call(mix, h2, _Layer(w_o, l), fw, bsz=bsz, seq_len=seq_len, tm=FINAL_TM)
        h2, fp, sg, q, k, v = call((mix, h2, _Layer(w_o, l)), in_weights(l + 1), tm=TOKEN_TM)
```

```python
import functools
import math
from typing import NamedTuple

import numpy as np
import jax
import jax.numpy as jnp
from jax import lax
from jax.experimental import pallas as pl
from jax.experimental.pallas import tpu as pltpu

N_META = 16
FOURIER_HEADS = 4
FOURIER_HEAD_DIM = 64
POOL_WINDOWS = (2, 4, 8, 16)
POOL_GROUP_DIM = 64
MLA_HEADS = 4
QK_NOPE_DIM = 128
QK_ROPE_DIM = 64
V_HEAD_DIM = 128
ROPE_THETA = 10000.0
NORM_EPS = 1e-6

LANES = 128
POOL_PAD = 32
VMEM_LIMIT = 56 * 1024 * 1024
TOKEN_TM = 512
FINAL_TM = 1024
ATTN_HEADS = 2
ATTN_STREAMS = 2
ATTN_TS = 256
ATTN_KEY_PARTS = 2
ATTN_ONES = 16

F32 = jnp.float32
BF16 = jnp.bfloat16


def _round_up(a, m):
    return (a + m - 1) // m * m


def _dot(a, b):
    return jnp.dot(a, b, preferred_element_type=F32)


def _dot_nt(a, b):
    return lax.dot_general(a, b, (((1,), (1,)), ((), ())), preferred_element_type=F32)


def _rms(x, w):
    return x * lax.rsqrt(jnp.mean(x * x, axis=-1, keepdims=True) + NORM_EPS) * w


class _Layer(NamedTuple):
    stack: jax.Array
    index: int

    @property
    def shape(self):
        return self.stack.shape[1:]


def _whole(a):
    if isinstance(a, _Layer):
        return pl.BlockSpec((None,) + a.shape, lambda *g: (a.index,) + (0,) * len(a.shape))
    return pl.BlockSpec(a.shape, lambda *g: (0,) * a.ndim)


def _operands(*xs):
    return [x.stack if isinstance(x, _Layer) else x for x in xs]


def _project_in(x, nw_ref, w1_ref, qnw_ref, wuq_ref, kvnw_ref, wukv_ref, cos_ref, sin_ref,
                fp_ref, sg_ref, q_ref, k_ref, v_ref, *, tm, seq_len, scale, d_q, d_kv):
    t = pl.program_id(0)
    inv = lax.rsqrt(jnp.mean(x * x, axis=-1, keepdims=True) + NORM_EPS)
    n = (x * nw_ref[...]).astype(BF16)

    fp_ref[...] = _dot(n, w1_ref[:, 0:512]) * inv
    g = _dot(n, w1_ref[:, 512:1536]) * inv
    sg_ref[...] = (g * pl.reciprocal(1.0 + jnp.exp(-g))).astype(BF16)

    start = pl.multiple_of(lax.rem(t * tm, seq_len), 16)
    cos = cos_ref[pl.ds(start, tm), :]
    sin = sin_ref[pl.ds(start, tm), :]

    def latent(lo, width, w_ref):
        u = _dot(n, w1_ref[:, lo:lo + width])
        ms = jnp.mean(u * u, axis=-1, keepdims=True)
        return (u * w_ref[...]).astype(BF16), inv * lax.rsqrt(inv * inv * ms + NORM_EPS)

    o = 1536
    cq, q_inv = latent(o, d_q, qnw_ref)
    o += d_q
    ckv, kv_inv = latent(o, d_kv, kvnw_ref)
    o += d_kv
    kr = _dot(n, w1_ref[:, o:o + 2 * LANES]) * inv

    qf = _dot(cq, wuq_ref[...]) * (q_inv * scale)
    nq = MLA_HEADS * QK_NOPE_DIM
    nr = MLA_HEADS * QK_ROPE_DIM
    for h in range(MLA_HEADS):
        q_ref[:, 256 * h:256 * h + 128] = qf[:, 128 * h:128 * h + 128].astype(BF16)
    for pair in range(MLA_HEADS // 2):
        r = qf[:, nq + 128 * pair:nq + 128 * pair + 128]
        rs = qf[:, nq + nr + 128 * pair:nq + nr + 128 * pair + 128]
        roped = (r * cos + rs * sin).astype(BF16)
        for h in (2 * pair, 2 * pair + 1):
            q_ref[:, 256 * h + 128:256 * h + 256] = roped

    kvf = _dot(ckv, wukv_ref[...]) * kv_inv
    v_ref[...] = kvf[:, nq:].astype(BF16)
    kroped = kr[:, :LANES] * cos + kr[:, LANES:] * sin
    lane = lax.broadcasted_iota(jnp.int32, kroped.shape, 1)
    k_lo = jnp.where(lane < QK_ROPE_DIM, kroped, 0.0).astype(BF16)
    k_hi = jnp.where(lane >= QK_ROPE_DIM, kroped, 0.0).astype(BF16)
    for h in range(MLA_HEADS):
        k_ref[:, 256 * h:256 * h + 128] = kvf[:, 128 * h:128 * h + 128].astype(BF16)
        k_ref[:, 256 * h + 128:256 * h + 256] = k_lo if h % 2 == 0 else k_hi


def _mix_residual(yf_ref, yp_ref, ya_ref, sg_ref, h_ref, w_ref):
    y = jnp.concatenate([yf_ref[...], yp_ref[...], ya_ref[...]], axis=-1)
    return h_ref[...] + _dot(y * sg_ref[...], w_ref[...])


def _window_start(t, tm, seq_len):
    b0, p0 = (t * tm) // seq_len, lax.rem(t * tm, seq_len)
    return b0 * (seq_len - N_META) + jnp.maximum(p0 - N_META, 0)


def _embed_kernel(xw_ref, meta_ref, *refs, tm, seq_len, n_x, **kw):
    t = pl.program_id(0)
    h_ref = refs[8]
    first_pos = lax.rem(t * tm, seq_len)
    meta_group = lax.rem(seq_len - first_pos, seq_len) // N_META
    start = _window_start(t, tm, seq_len)
    shift = start - jnp.minimum(start, n_x - tm)
    for j in range(tm // N_META):
        src = shift + N_META * (j - (meta_group < j).astype(jnp.int32))
        src = pl.multiple_of(jnp.minimum(src, tm - N_META), N_META)
        blk = jnp.where(meta_group == j, meta_ref[...], xw_ref[pl.ds(src, N_META), :])
        h_ref[N_META * j:N_META * (j + 1), :] = blk
    _project_in(h_ref[...], *refs[:8], *refs[9:], tm=tm, seq_len=seq_len, **kw)


def _fused_kernel(yf_ref, yp_ref, ya_ref, sg_ref, h_ref, w_ref, *refs, **kw):
    h_new = _mix_residual(yf_ref, yp_ref, ya_ref, sg_ref, h_ref, w_ref)
    refs[8][...] = h_new
    _project_in(h_new, *refs[:8], *refs[9:], **kw)


def _final_kernel(yf_ref, yp_ref, ya_ref, sg_ref, h_ref, w_ref, fw_ref, o_ref):
    o_ref[0] = _rms(_mix_residual(yf_ref, yp_ref, ya_ref, sg_ref, h_ref, w_ref), fw_ref[...])


_MIX_WIDTHS = (256, 256, 512, 1024)
_IN_OUT = ((512, F32), (1024, BF16), (1024, BF16), (1024, BF16), (512, BF16))


def _token_call(src, in_w, *, n_tok, tm, seq_len):
    nw, w1, qnw, wuq, kvnw, wukv, cos_t, sin_t = in_w
    d = nw.shape[1]
    scale = float((QK_NOPE_DIM + QK_ROPE_DIM) ** -0.5 * math.log2(math.e))
    row = lambda w: pl.BlockSpec((tm, w), lambda t: (t, 0))
    kw = dict(tm=tm, seq_len=seq_len, scale=scale, d_q=qnw.shape[1], d_kv=kvnw.shape[1])
    out_specs = [row(d)] + [row(w) for w, _ in _IN_OUT]
    out_shape = [jax.ShapeDtypeStruct((n_tok, d), F32)] + [jax.ShapeDtypeStruct((n_tok, w), dt) for w, dt in _IN_OUT]
    if len(src) == 2:
        x2, meta = src
        kw["n_x"] = x2.shape[0]

        def window(t):
            return pl.multiple_of(jnp.minimum(_window_start(t, tm, seq_len), x2.shape[0] - tm), N_META), 0

        body, args, aliases = _embed_kernel, (x2, meta, *in_w), {}
        in_specs = [pl.BlockSpec((pl.Element(tm), pl.Element(d)), window), _whole(meta)]
    else:
        mix, h2, w_out = src
        body, args, aliases = _fused_kernel, (*mix, h2, w_out, *in_w), {}
        in_specs = [row(w) for w in _MIX_WIDTHS] + [row(d), _whole(w_out)]
    in_specs += [_whole(a) for a in in_w]
    return pl.pallas_call(
        functools.partial(body, **kw),
        grid=(pl.cdiv(n_tok, tm),),
        in_specs=in_specs, out_specs=out_specs, out_shape=out_shape,
        input_output_aliases=aliases,
        compiler_params=pltpu.CompilerParams(dimension_semantics=("arbitrary",),
                                             vmem_limit_bytes=VMEM_LIMIT),
        name="embed" if len(src) == 2 else "outin",
    )(*_operands(*args))


def _final_call(mix, h2, w_out, fw, *, bsz, seq_len, tm):
    n_tok, d = h2.shape
    seq = seq_len - N_META
    row = lambda w: pl.BlockSpec((pl.Element(tm), pl.Element(w)),
                                 lambda b, j: (pl.multiple_of(b * seq_len + N_META + j * tm, 16), 0))
    return pl.pallas_call(
        _final_kernel,
        grid=(bsz, seq // tm),
        in_specs=[row(w) for w in _MIX_WIDTHS] + [row(d), _whole(w_out), _whole(fw)],
        out_specs=pl.BlockSpec((1, tm, d), lambda b, j: (b, j, 0)),
        out_shape=jax.ShapeDtypeStruct((bsz, seq, d), F32),
        compiler_params=pltpu.CompilerParams(dimension_semantics=("arbitrary", "arbitrary"),
                                             vmem_limit_bytes=VMEM_LIMIT),
        name="final",
    )(*_operands(*mix, h2, w_out, fw))


def _fourier_kernel(x_ref, cs_ref, wbd_ref, mats_ref, o_ref, pq_ref, u_ref, y_ref, *, quarter):
    b = pl.program_id(0)
    qp = u_ref.shape[1]

    @pl.when(b == 0)
    def _():
        w = wbd_ref[...]
        pq_ref[:, 0:256] = jnp.dot(cs_ref[0], w, preferred_element_type=F32,
                                   precision=lax.Precision.HIGHEST).astype(BF16)
        pq_ref[:, 256:512] = jnp.dot(cs_ref[1], w, preferred_element_type=F32,
                                     precision=lax.Precision.HIGHEST).astype(BF16)
        u_ref[:, quarter:qp, :] = jnp.zeros((4, qp - quarter, u_ref.shape[2]), F32)

    x0, x1, x2, x3 = (x_ref[0, j * quarter:(j + 1) * quarter, :] for j in range(4))
    s02, s13, d02, d13 = x0 + x2, x1 + x3, x0 - x2, x1 - x3
    for j, u in enumerate((s02 + s13, s02 - s13, d02, d13)):
        u_ref[j, 0:quarter, :] = u
    pq = pq_ref[...]
    ab = [_dot(u_ref[j].astype(BF16), pq) for j in range(4)]
    re = (ab[0][:, :256], ab[2][:, :256] - ab[3][:, 256:], ab[1][:, :256], ab[2][:, :256] + ab[3][:, 256:])
    im = (ab[0][:, 256:], ab[2][:, 256:] + ab[3][:, :256], ab[1][:, 256:], ab[2][:, 256:] - ab[3][:, :256])
    for r in range(4):
        rhs = jnp.concatenate([re[r].astype(BF16), im[r].astype(BF16)], axis=0)
        y = _dot(mats_ref[r], rhs)
        for c in range(2):
            y_ref[c, pl.ds(r, quarter, stride=4), :] = y[0:quarter, c * LANES:(c + 1) * LANES]
    for c in range(2):
        o_ref[0, :, c * LANES:(c + 1) * LANES] = y_ref[c].astype(BF16)


def _fourier(fp3, cs, wbd, mats):
    bsz, seq_len, _ = fp3.shape
    quarter = seq_len // 4
    qp = mats.shape[2] // 2
    return pl.pallas_call(
        functools.partial(_fourier_kernel, quarter=quarter),
        grid=(bsz,),
        in_specs=[pl.BlockSpec((1, seq_len, 256), lambda b: (b, 0, 0)),
                  _whole(cs), _whole(wbd),
                  pl.BlockSpec(mats.shape, lambda b: (0, 0, 0), pipeline_mode=pl.Buffered(1))],
        out_specs=pl.BlockSpec((1, seq_len, 256), lambda b: (b, 0, 0)),
        out_shape=jax.ShapeDtypeStruct((bsz, seq_len, 256), BF16),
        scratch_shapes=[pltpu.VMEM((256, 512), BF16), pltpu.VMEM((4, qp, 256), F32),
                        pltpu.VMEM((2, seq_len, LANES), F32)],
        compiler_params=pltpu.CompilerParams(dimension_semantics=("arbitrary",),
                                             vmem_limit_bytes=VMEM_LIMIT),
        name="fourier",
    )(*_operands(fp3, cs, wbd, mats))


def _pool_kernel(x_ref, icnt_ref, wbd_ref, ps_ref, o_ref, xp_ref, s2_ref, s4_ref, s8_ref, *, seq_len):
    pad = POOL_PAD
    tot = seq_len + 2 * pad
    zeros = jnp.zeros((pad, 256), F32)
    xp_ref[0:pad, :] = zeros
    xp_ref[pad + seq_len:tot, :] = zeros
    xp_ref[pad:pad + seq_len, :] = x_ref[0]

    def level(src, dst, lo, hi, d_lo, d_hi, cols):
        chunk = 512
        for c0 in range(lo, hi, chunk):
            c1 = min(c0 + chunk, hi)
            dst[c0:c1, cols] = src[c0 + d_lo:c1 + d_lo, cols] + src[c0 + d_hi:c1 + d_hi, cols]

    lower, upper = slice(0, LANES), slice(LANES, 2 * LANES)
    level(xp_ref, s2_ref, 8, tot - 8, -1, 0, lower)
    level(s2_ref, s4_ref, 16, tot - 16, -1, 1, lower)
    level(xp_ref, s2_ref, 8, tot - 8, 0, 1, upper)
    level(s2_ref, s4_ref, 8, tot - 16, 0, 2, upper)
    level(s4_ref, s8_ref, 8, tot - 24, 0, 4, upper)

    chunk = 512
    for c0 in range(0, seq_len, chunk):
        c1 = min(c0 + chunk, seq_len)
        a0, a1 = pad + c0, pad + c1
        s8 = s8_ref[a0 - 4:a1 - 4, upper]
        s16 = s8_ref[a0 - 8:a1 - 8, upper] + s8_ref[a0:a1, upper]
        first = lax.broadcasted_iota(jnp.int32, s16.shape, 1) < POOL_GROUP_DIM
        win = jnp.concatenate([jnp.where(first, s2_ref[a0:a1, lower], s4_ref[a0:a1, lower]),
                               jnp.where(first, s8, s16)], axis=-1)
        pooled = (win * icnt_ref[c0:c1, :] - xp_ref[a0:a1, :]).astype(BF16)
        o_ref[0, c0:c1, :] = (_dot(pooled, wbd_ref[...]) * ps_ref[...]).astype(BF16)


def _pool(fp3, icnt, wbd, ps):
    bsz, seq_len, _ = fp3.shape
    tot = seq_len + 2 * POOL_PAD
    return pl.pallas_call(
        functools.partial(_pool_kernel, seq_len=seq_len),
        grid=(bsz,),
        in_specs=[pl.BlockSpec((1, seq_len, 256), lambda b: (b, 0, 1)),
                  _whole(icnt), _whole(wbd), _whole(ps)],
        out_specs=pl.BlockSpec((1, seq_len, 256), lambda b: (b, 0, 0)),
        out_shape=jax.ShapeDtypeStruct((bsz, seq_len, 256), BF16),
        scratch_shapes=[pltpu.VMEM((tot, 256), F32)] * 4,
        compiler_params=pltpu.CompilerParams(dimension_semantics=("arbitrary",),
                                             vmem_limit_bytes=VMEM_LIMIT),
        name="pool",
    )(*_operands(fp3, icnt, wbd, ps))


def _attn_kernel(q_ref, k_ref, v_ref, o_ref, vpad_ref, vt_ref, *s_refs, seq_len, ts):
    lp = vt_ref.shape[2]
    pad = lp - seq_len
    dk = q_ref.shape[2] // ATTN_HEADS
    key = lax.broadcasted_iota(jnp.int32, (ATTN_ONES, lp), 1)
    if pad:
        vpad_ref[seq_len:lp, :] = jnp.zeros((pad, V_HEAD_DIM), F32)
    for hh in range(ATTN_HEADS):
        vpad_ref[0:seq_len, :] = v_ref[0, :, hh * V_HEAD_DIM:(hh + 1) * V_HEAD_DIM].astype(F32)
        vt_ref[hh, 0:V_HEAD_DIM, :] = vpad_ref[...].T.astype(BF16)
        vt_ref[hh, V_HEAD_DIM:, :] = jnp.where(key < seq_len, 1.0, 0.0).astype(BF16)

    group = ts * ATTN_STREAMS
    n_groups = seq_len // group
    tail = seq_len - n_groups * group
    tail_row = seq_len - ts
    half_keys = (seq_len // 2) // LANES * LANES

    def put(buf, *parts):
        lo = 0
        for st in parts:
            buf[lo:lo + st.shape[0], :] = st
            lo += st.shape[0]
        m = functools.reduce(jnp.maximum, [jnp.max(st, axis=0, keepdims=True) for st in parts])
        buf[seq_len:seq_len + 8, :] = jnp.broadcast_to(m, (8, ts))

    def rows_of(g):
        rows = [g * group + j * ts for j in range(ATTN_STREAMS)]
        return rows if isinstance(g, int) else [pl.multiple_of(r, ts) for r in rows]

    def scores(bufs, hh, g):
        cols = slice(hh * dk, (hh + 1) * dk)
        for buf, r in zip(bufs, rows_of(g)):
            put(buf, _dot_nt(k_ref[0, :, cols], q_ref[0, pl.ds(r, ts), cols]))

    def tail_scores(buf, hh):
        cols = slice(hh * dk, (hh + 1) * dk)
        q = q_ref[0, tail_row:seq_len, cols]
        put(buf, _dot_nt(k_ref[0, 0:half_keys, cols], q), _dot_nt(k_ref[0, half_keys:seq_len, cols], q))

    def weighted(buf, hh):
        m = buf[seq_len:seq_len + 1, :]
        bounds = [seq_len * i // ATTN_KEY_PARTS // LANES * LANES for i in range(ATTN_KEY_PARTS)] + [seq_len]
        acc = None
        for lo, hi in zip(bounds[:-1], bounds[1:]):
            p = jnp.exp2(buf[lo:hi, :] - m).astype(BF16)
            if hi == seq_len and pad:
                p, hi = jnp.concatenate([p, jnp.zeros((pad, ts), BF16)], axis=0), lp
            part = _dot(vt_ref[hh, :, lo:hi], p)
            acc = part if acc is None else acc + part
        return acc

    def finish(buf, hh, r):
        ov = weighted(buf, hh)
        o = ov[:V_HEAD_DIM] * pl.reciprocal(ov[V_HEAD_DIM:V_HEAD_DIM + 1])
        o_ref[0, pl.ds(r, ts), hh * V_HEAD_DIM:(hh + 1) * V_HEAD_DIM] = o.T.astype(BF16)

    def outputs(bufs, hh, g):
        for buf, r in zip(bufs, rows_of(g)):
            finish(buf, hh, r)

    first, second = s_refs[:ATTN_STREAMS], s_refs[ATTN_STREAMS:]
    scores(first, 0, 0)
    for hh in range(ATTN_HEADS):
        def pair(k, _, hh=hh, first=first, second=second):
            scores(second, hh, 2 * k + 1)
            outputs(first, hh, 2 * k)
            scores(first, hh, 2 * k + 2)
            outputs(second, hh, 2 * k + 1)
            return 0

        lax.fori_loop(0, n_groups // 2 - 1, pair, 0)
        scores(second, hh, n_groups - 1)
        outputs(first, hh, n_groups - 2)
        if tail:
            tail_scores(first[0], hh)
        outputs(second, hh, n_groups - 1)
        if hh + 1 < ATTN_HEADS:
            scores(second, hh + 1, 0)
        if tail:
            finish(first[0], hh, tail_row)
        first, second = second, first


def _attention(q3, k3, v3, *, ts):
    bsz, seq_len, _ = q3.shape
    assert ts % LANES == 0 and (seq_len // (ts * ATTN_STREAMS)) % 2 == 0 and seq_len % (ts * ATTN_STREAMS) <= ts
    lp = _round_up(seq_len, LANES)
    nh = ATTN_HEADS
    return pl.pallas_call(
        functools.partial(_attn_kernel, seq_len=seq_len, ts=ts),
        grid=(bsz, MLA_HEADS // nh),
        in_specs=[pl.BlockSpec((1, seq_len, 256 * nh), lambda b, h: (b, 0, h)),
                  pl.BlockSpec((1, seq_len, 256 * nh), lambda b, h: (b, 0, h)),
                  pl.BlockSpec((1, seq_len, V_HEAD_DIM * nh), lambda b, h: (b, 0, h))],
        out_specs=pl.BlockSpec((1, seq_len, V_HEAD_DIM * nh), lambda b, h: (b, 0, h)),
        out_shape=jax.ShapeDtypeStruct((bsz, seq_len, MLA_HEADS * V_HEAD_DIM), BF16),
        scratch_shapes=[pltpu.VMEM((lp, V_HEAD_DIM), F32), pltpu.VMEM((nh, V_HEAD_DIM + ATTN_ONES, lp), BF16)]
        + [pltpu.VMEM((seq_len + 8, ts), F32)] * (2 * ATTN_STREAMS),
        compiler_params=pltpu.CompilerParams(dimension_semantics=("arbitrary", "arbitrary"),
                                             vmem_limit_bytes=VMEM_LIMIT),
        name="attention",
    )(q3, k3, v3)


def _rope_tables(seq_len, rows):
    inv = 1.0 / (ROPE_THETA ** (jnp.arange(0, QK_ROPE_DIM, 2, dtype=F32) / QK_ROPE_DIM))
    pos = (jnp.arange(rows) % seq_len).astype(F32)
    ang = pos[:, None] * inv[None, :]
    cos, sin = jnp.cos(ang), jnp.sin(ang)
    return jnp.tile(cos, (1, 4)), jnp.tile(jnp.concatenate([-sin, sin], axis=-1), (1, 2))


def _dft_tables(seq_len):
    quarter = seq_len // 4
    hp, qp = _round_up(quarter, 16), _round_up(quarter, LANES)
    step = 64
    k = jnp.arange(hp, dtype=jnp.int32)[:, None]
    a = step * jnp.arange(qp // step, dtype=jnp.int32)[None, :]
    b = jnp.arange(step, dtype=jnp.int32)[None, :]
    valid = ((k < quarter)[:, :, None] & ((a[:, :, None] + b[:, None, :]) < quarter)).reshape(hp, qp)
    unit = 2.0 * math.pi / seq_len
    mats = []
    for r in range(4):
        row = 4 * k + r
        ang_a = (row * a % seq_len).astype(F32) * unit
        ang_b = (row * b % seq_len).astype(F32) * unit
        ca, sa = jnp.cos(ang_a)[:, :, None], jnp.sin(ang_a)[:, :, None]
        cb, sb = jnp.cos(ang_b)[:, None, :], jnp.sin(ang_b)[:, None, :]
        cos = (ca * cb - sa * sb).reshape(hp, qp)
        msin = -(sa * cb + ca * sb).reshape(hp, qp)
        mats.append(jnp.concatenate([jnp.where(valid, cos, 0.0), jnp.where(valid, msin, 0.0)], axis=1))
    return jnp.stack(mats).astype(BF16)


def _channel_dft(seq_len):
    c = np.arange(FOURIER_HEAD_DIM)
    ang = 2.0 * np.pi * ((c[:, None] * c[None, :]) % FOURIER_HEAD_DIM) / FOURIER_HEAD_DIM
    norm = 1.0 / math.sqrt(seq_len * FOURIER_HEAD_DIM)
    eye = np.eye(FOURIER_HEADS)
    return jnp.asarray(np.stack([np.kron(eye, np.cos(ang) * norm), np.kron(eye, np.sin(ang) * norm)]), F32)


def _block_diag(w):
    n, g, c, d = w.shape
    out = jnp.zeros((n, g * c, g * d), w.dtype)
    for i in range(g):
        out = out.at[:, i * c:(i + 1) * c, i * d:(i + 1) * d].set(w[:, i])
    return out


def _pool_inv_count(seq_len):
    idx = np.arange(seq_len)
    cols = []
    for w in POOL_WINDOWS:
        cnt = np.clip(idx + w // 2, 0, seq_len) - np.clip(idx - w // 2, 0, seq_len)
        cols.append(np.repeat((1.0 / cnt)[:, None], POOL_GROUP_DIM, axis=1))
    return jnp.asarray(np.concatenate(cols, axis=1), F32)


def _layer_weights(w_in, w_uq, w_ukv):
    n, d_q, d_kv = w_in.shape[0], w_uq.shape[1], w_ukv.shape[1]
    sizes = (256, 256, 256, 256, d_q, d_kv, QK_ROPE_DIM, 512)
    offs = np.concatenate([[0], np.cumsum(sizes)])
    f_in, f_gate, p_in, p_gate, c_q, c_kv, k_r, a_gate = (w_in[..., offs[i]:offs[i + 1]] for i in range(8))
    half = QK_ROPE_DIM // 2
    swap = lambda a: jnp.concatenate([a[..., half:], a[..., :half]], axis=-1)
    w1 = jnp.concatenate([f_in, p_in, f_gate, p_gate, a_gate, c_q, c_kv,
                          k_r, k_r, swap(k_r), swap(k_r)], axis=-1).astype(BF16)
    uq = w_uq.reshape(n, d_q, MLA_HEADS, QK_NOPE_DIM + QK_ROPE_DIM)
    q_rope = uq[..., QK_NOPE_DIM:]
    wuq = jnp.concatenate([uq[..., :QK_NOPE_DIM].reshape(n, d_q, -1), q_rope.reshape(n, d_q, -1),
                           swap(q_rope).reshape(n, d_q, -1)], axis=-1).astype(BF16)
    ukv = w_ukv.reshape(n, d_kv, MLA_HEADS, QK_NOPE_DIM + V_HEAD_DIM)
    wukv = jnp.concatenate([ukv[..., :QK_NOPE_DIM].reshape(n, d_kv, -1),
                            ukv[..., QK_NOPE_DIM:].reshape(n, d_kv, -1)], axis=-1).astype(BF16)
    return w1, wuq, wukv


def kernel(x, meta_tokens, norm_w, w_in, fourier_w, pool_w, pool_scale, q_norm_w, w_uq, kv_norm_w, w_ukv,
           w_out, final_norm_w):
    bsz, seq, d = x.shape
    depth = norm_w.shape[0]
    seq_len = seq + N_META
    n_tok = bsz * seq_len
    assert seq_len % 16 == 0 and TOKEN_TM % N_META == 0 and TOKEN_TM <= seq and seq % FINAL_TM == 0

    cos_t, sin_t = _rope_tables(seq_len, seq_len + TOKEN_TM)
    mats = _dft_tables(seq_len)
    cs = _channel_dft(seq_len)
    icnt = _pool_inv_count(seq_len)
    fw = final_norm_w.reshape(1, d)

    w1, wuq, wukv = _layer_weights(w_in, w_uq, w_ukv)
    w_o = w_out.astype(BF16)
    f_bd = _block_diag(fourier_w)
    p_bd = _block_diag(pool_w).astype(BF16)

    rows = lambda a: a.reshape(depth, 1, -1)
    nw, qnw, kvnw, ps = rows(norm_w), rows(q_norm_w), rows(kv_norm_w), rows(pool_scale)

    def in_weights(l):
        return (*(_Layer(a, l) for a in (nw, w1, qnw, wuq, kvnw, wukv)), cos_t, sin_t)

    call = functools.partial(_token_call, n_tok=n_tok, tm=TOKEN_TM, seq_len=seq_len)
    h2, fp, sg, q, k, v = call((x.reshape(bsz * seq, d), meta_tokens.astype(x.dtype)), in_weights(0))
    for l in range(depth):
        fp3 = fp.reshape(bsz, seq_len, 512)
        yf = _fourier(fp3, cs, _Layer(f_bd, l), mats).reshape(n_tok, 256)
        yp = _pool(fp3, icnt, _Layer(p_bd, l), _Layer(ps, l)).reshape(n_tok, 256)
        ya = _attention(q.reshape(bsz, seq_len, 1024), k.reshape(bsz, seq_len, 1024),
                        v.reshape(bsz, seq_len, 512), ts=ATTN_TS).reshape(n_tok, 512)
        mix = (yf, yp, ya, sg)
        if l == depth - 1:
            return _final_call(mix, h2, _Layer(w_o, l), fw, bsz=bsz, seq_len=seq_len, tm=FINAL_TM)
        h2, fp, sg, q, k, v = call((mix, h2, _Layer(w_o, l)), in_weights(l + 1))
```

```python
import functools
import math
from typing import NamedTuple

import numpy as np
import jax
import jax.numpy as jnp
from jax import lax
from jax.experimental import pallas as pl
from jax.experimental.pallas import tpu as pltpu

N_META = 16
FOURIER_HEADS = 4
FOURIER_HEAD_DIM = 64
POOL_WINDOWS = (2, 4, 8, 16)
POOL_GROUP_DIM = 64
MLA_HEADS = 4
QK_NOPE_DIM = 128
QK_ROPE_DIM = 64
V_HEAD_DIM = 128
ROPE_THETA = 10000.0
NORM_EPS = 1e-6

LANES = 128
POOL_PAD = 32
VMEM_LIMIT = 56 * 1024 * 1024
TOKEN_TM = 512
FINAL_TM = 1024
ATTN_HEADS = 2
ATTN_STREAMS = 2
ATTN_TS = 256
ATTN_KEY_PARTS = 2
ATTN_ONES = 16

F32 = jnp.float32
BF16 = jnp.bfloat16


def _round_up(a, m):
    return (a + m - 1) // m * m


def _dot(a, b):
    return jnp.dot(a, b, preferred_element_type=F32)


def _dot_nt(a, b):
    return lax.dot_general(a, b, (((1,), (1,)), ((), ())), preferred_element_type=F32)


def _rms(x, w):
    return x * lax.rsqrt(jnp.mean(x * x, axis=-1, keepdims=True) + NORM_EPS) * w


class _Layer(NamedTuple):
    stack: jax.Array
    index: int

    @property
    def shape(self):
        return self.stack.shape[1:]


def _whole(a):
    if isinstance(a, _Layer):
        return pl.BlockSpec((None,) + a.shape, lambda *g: (a.index,) + (0,) * len(a.shape))
    return pl.BlockSpec(a.shape, lambda *g: (0,) * a.ndim)


def _operands(*xs):
    return [x.stack if isinstance(x, _Layer) else x for x in xs]


def _project_in(x, nw_ref, w1_ref, qnw_ref, wuq_ref, kvnw_ref, wukv_ref, cos_ref, sin_ref,
                fp_ref, sg_ref, q_ref, k_ref, v_ref, *, tm, seq_len, scale, d_q, d_kv):
    t = pl.program_id(0)
    inv = lax.rsqrt(jnp.mean(x * x, axis=-1, keepdims=True) + NORM_EPS)
    n = (x * nw_ref[...]).astype(BF16)

    fp_ref[...] = _dot(n, w1_ref[:, 0:512]) * inv
    g = _dot(n, w1_ref[:, 512:1536]) * inv
    sg_ref[...] = (g * pl.reciprocal(1.0 + jnp.exp(-g))).astype(BF16)

    start = pl.multiple_of(lax.rem(t * tm, seq_len), 16)
    cos = cos_ref[pl.ds(start, tm), :]
    sin = sin_ref[pl.ds(start, tm), :]

    def latent(lo, width, w_ref):
        u = _dot(n, w1_ref[:, lo:lo + width])
        ms = jnp.mean(u * u, axis=-1, keepdims=True)
        return (u * w_ref[...]).astype(BF16), inv * lax.rsqrt(inv * inv * ms + NORM_EPS)

    o = 1536
    cq, q_inv = latent(o, d_q, qnw_ref)
    o += d_q
    ckv, kv_inv = latent(o, d_kv, kvnw_ref)
    o += d_kv
    kr = _dot(n, w1_ref[:, o:o + 2 * LANES]) * inv

    qf = _dot(cq, wuq_ref[...]) * (q_inv * scale)
    nq = MLA_HEADS * QK_NOPE_DIM
    nr = MLA_HEADS * QK_ROPE_DIM
    for h in range(MLA_HEADS):
        q_ref[:, 256 * h:256 * h + 128] = qf[:, 128 * h:128 * h + 128].astype(BF16)
    for pair in range(MLA_HEADS // 2):
        r = qf[:, nq + 128 * pair:nq + 128 * pair + 128]
        rs = qf[:, nq + nr + 128 * pair:nq + nr + 128 * pair + 128]
        roped = (r * cos + rs * sin).astype(BF16)
        for h in (2 * pair, 2 * pair + 1):
            q_ref[:, 256 * h + 128:256 * h + 256] = roped

    kvf = _dot(ckv, wukv_ref[...]) * kv_inv
    v_ref[...] = kvf[:, nq:].astype(BF16)
    kroped = kr[:, :LANES] * cos + kr[:, LANES:] * sin
    lane = lax.broadcasted_iota(jnp.int32, kroped.shape, 1)
    k_lo = jnp.where(lane < QK_ROPE_DIM, kroped, 0.0).astype(BF16)
    k_hi = jnp.where(lane >= QK_ROPE_DIM, kroped, 0.0).astype(BF16)
    for h in range(MLA_HEADS):
        k_ref[:, 256 * h:256 * h + 128] = kvf[:, 128 * h:128 * h + 128].astype(BF16)
        k_ref[:, 256 * h + 128:256 * h + 256] = k_lo if h % 2 == 0 else k_hi


def _mix_residual(yf_ref, yp_ref, ya_ref, sg_ref, h_ref, w_ref):
    y = jnp.concatenate([yf_ref[...], yp_ref[...], ya_ref[...]], axis=-1)
    return h_ref[...] + _dot(y * sg_ref[...], w_ref[...])


def _window_start(t, tm, seq_len):
    b0, p0 = (t * tm) // seq_len, lax.rem(t * tm, seq_len)
    return b0 * (seq_len - N_META) + jnp.maximum(p0 - N_META, 0)


def _embed_kernel(xw_ref, meta_ref, *refs, tm, seq_len, n_x, **kw):
    t = pl.program_id(0)
    h_ref = refs[8]
    first_pos = lax.rem(t * tm, seq_len)
    meta_group = lax.rem(seq_len - first_pos, seq_len) // N_META
    start = _window_start(t, tm, seq_len)
    shift = start - jnp.minimum(start, n_x - tm)
    for j in range(tm // N_META):
        src = shift + N_META * (j - (meta_group < j).astype(jnp.int32))
        src = pl.multiple_of(jnp.minimum(src, tm - N_META), N_META)
        blk = jnp.where(meta_group == j, meta_ref[...], xw_ref[pl.ds(src, N_META), :])
        h_ref[N_META * j:N_META * (j + 1), :] = blk
    _project_in(h_ref[...], *refs[:8], *refs[9:], tm=tm, seq_len=seq_len, **kw)


def _fused_kernel(yf_ref, yp_ref, ya_ref, sg_ref, h_ref, w_ref, *refs, **kw):
    h_new = _mix_residual(yf_ref, yp_ref, ya_ref, sg_ref, h_ref, w_ref)
    refs[8][...] = h_new
    _project_in(h_new, *refs[:8], *refs[9:], **kw)


def _final_kernel(yf_ref, yp_ref, ya_ref, sg_ref, h_ref, w_ref, fw_ref, o_ref):
    o_ref[0] = _rms(_mix_residual(yf_ref, yp_ref, ya_ref, sg_ref, h_ref, w_ref), fw_ref[...])


_MIX_WIDTHS = (256, 256, 512, 1024)
_IN_OUT = ((512, F32), (1024, BF16), (1024, BF16), (1024, BF16), (512, BF16))


def _token_call(src, in_w, *, n_tok, tm, seq_len):
    nw, w1, qnw, wuq, kvnw, wukv, cos_t, sin_t = in_w
    d = nw.shape[1]
    scale = float((QK_NOPE_DIM + QK_ROPE_DIM) ** -0.5 * math.log2(math.e))
    row = lambda w: pl.BlockSpec((tm, w), lambda t: (t, 0))
    kw = dict(tm=tm, seq_len=seq_len, scale=scale, d_q=qnw.shape[1], d_kv=kvnw.shape[1])
    out_specs = [row(d)] + [row(w) for w, _ in _IN_OUT]
    out_shape = [jax.ShapeDtypeStruct((n_tok, d), F32)] + [jax.ShapeDtypeStruct((n_tok, w), dt) for w, dt in _IN_OUT]
    if len(src) == 2:
        x2, meta = src
        kw["n_x"] = x2.shape[0]

        def window(t):
            return pl.multiple_of(jnp.minimum(_window_start(t, tm, seq_len), x2.shape[0] - tm), N_META), 0

        body, args, aliases = _embed_kernel, (x2, meta, *in_w), {}
        in_specs = [pl.BlockSpec((pl.Element(tm), pl.Element(d)), window), _whole(meta)]
    else:
        mix, h2, w_out = src
        body, args, aliases = _fused_kernel, (*mix, h2, w_out, *in_w), {}
        in_specs = [row(w) for w in _MIX_WIDTHS] + [row(d), _whole(w_out)]
    in_specs += [_whole(a) for a in in_w]
    return pl.pallas_call(
        functools.partial(body, **kw),
        grid=(pl.cdiv(n_tok, tm),),
        in_specs=in_specs, out_specs=out_specs, out_shape=out_shape,
        input_output_aliases=aliases,
        compiler_params=pltpu.CompilerParams(dimension_semantics=("parallel",),
                                             vmem_limit_bytes=VMEM_LIMIT),
        name="embed" if len(src) == 2 else "outin",
    )(*_operands(*args))


def _final_call(mix, h2, w_out, fw, *, bsz, seq_len, tm):
    n_tok, d = h2.shape
    seq = seq_len - N_META
    row = lambda w: pl.BlockSpec((pl.Element(tm), pl.Element(w)),
                                 lambda b, j: (pl.multiple_of(b * seq_len + N_META + j * tm, 16), 0))
    return pl.pallas_call(
        _final_kernel,
        grid=(bsz, seq // tm),
        in_specs=[row(w) for w in _MIX_WIDTHS] + [row(d), _whole(w_out), _whole(fw)],
        out_specs=pl.BlockSpec((1, tm, d), lambda b, j: (b, j, 0)),
        out_shape=jax.ShapeDtypeStruct((bsz, seq, d), F32),
        compiler_params=pltpu.CompilerParams(dimension_semantics=("parallel", "parallel"),
                                             vmem_limit_bytes=VMEM_LIMIT),
        name="final",
    )(*_operands(*mix, h2, w_out, fw))


def _fourier_kernel(x_ref, cs_ref, wbd_ref, mats_ref, o_ref, pq_ref, u_ref, y_ref, *, quarter):
    b = pl.program_id(0)
    qp = u_ref.shape[1]

    @pl.when(b == 0)
    def _():
        w = wbd_ref[...]
        pq_ref[:, 0:256] = jnp.dot(cs_ref[0], w, preferred_element_type=F32,
                                   precision=lax.Precision.HIGHEST).astype(BF16)
        pq_ref[:, 256:512] = jnp.dot(cs_ref[1], w, preferred_element_type=F32,
                                     precision=lax.Precision.HIGHEST).astype(BF16)
        u_ref[:, quarter:qp, :] = jnp.zeros((4, qp - quarter, u_ref.shape[2]), F32)

    x0, x1, x2, x3 = (x_ref[0, j * quarter:(j + 1) * quarter, :] for j in range(4))
    s02, s13, d02, d13 = x0 + x2, x1 + x3, x0 - x2, x1 - x3
    for j, u in enumerate((s02 + s13, s02 - s13, d02, d13)):
        u_ref[j, 0:quarter, :] = u
    pq = pq_ref[...]
    ab = [_dot(u_ref[j].astype(BF16), pq) for j in range(4)]
    re = (ab[0][:, :256], ab[2][:, :256] - ab[3][:, 256:], ab[1][:, :256], ab[2][:, :256] + ab[3][:, 256:])
    im = (ab[0][:, 256:], ab[2][:, 256:] + ab[3][:, :256], ab[1][:, 256:], ab[2][:, 256:] - ab[3][:, :256])
    for r in range(4):
        rhs = jnp.concatenate([re[r].astype(BF16), im[r].astype(BF16)], axis=0)
        y = _dot(mats_ref[r], rhs)
        for c in range(2):
            y_ref[c, pl.ds(r, quarter, stride=4), :] = y[0:quarter, c * LANES:(c + 1) * LANES]
    for c in range(2):
        o_ref[0, :, c * LANES:(c + 1) * LANES] = y_ref[c].astype(BF16)


def _fourier(fp3, cs, wbd, mats):
    bsz, seq_len, _ = fp3.shape
    quarter = seq_len // 4
    qp = mats.shape[2] // 2
    return pl.pallas_call(
        functools.partial(_fourier_kernel, quarter=quarter),
        grid=(bsz,),
        in_specs=[pl.BlockSpec((1, seq_len, 256), lambda b: (b, 0, 0)),
                  _whole(cs), _whole(wbd),
                  pl.BlockSpec(mats.shape, lambda b: (0, 0, 0), pipeline_mode=pl.Buffered(1))],
        out_specs=pl.BlockSpec((1, seq_len, 256), lambda b: (b, 0, 0)),
        out_shape=jax.ShapeDtypeStruct((bsz, seq_len, 256), BF16),
        scratch_shapes=[pltpu.VMEM((256, 512), BF16), pltpu.VMEM((4, qp, 256), F32),
                        pltpu.VMEM((2, seq_len, LANES), F32)],
        compiler_params=pltpu.CompilerParams(dimension_semantics=("arbitrary",),
                                             vmem_limit_bytes=VMEM_LIMIT),
        name="fourier",
    )(*_operands(fp3, cs, wbd, mats))


def _pool_kernel(x_ref, icnt_ref, wbd_ref, ps_ref, o_ref, xp_ref, s2_ref, s4_ref, s8_ref, *, seq_len):
    pad = POOL_PAD
    tot = seq_len + 2 * pad
    zeros = jnp.zeros((pad, 256), F32)
    xp_ref[0:pad, :] = zeros
    xp_ref[pad + seq_len:tot, :] = zeros
    xp_ref[pad:pad + seq_len, :] = x_ref[0]

    def level(src, dst, lo, hi, d_lo, d_hi, cols):
        chunk = 512
        for c0 in range(lo, hi, chunk):
            c1 = min(c0 + chunk, hi)
            dst[c0:c1, cols] = src[c0 + d_lo:c1 + d_lo, cols] + src[c0 + d_hi:c1 + d_hi, cols]

    lower, upper = slice(0, LANES), slice(LANES, 2 * LANES)
    level(xp_ref, s2_ref, 8, tot - 8, -1, 0, lower)
    level(s2_ref, s4_ref, 16, tot - 16, -1, 1, lower)
    level(xp_ref, s2_ref, 8, tot - 8, 0, 1, upper)
    level(s2_ref, s4_ref, 8, tot - 16, 0, 2, upper)
    level(s4_ref, s8_ref, 8, tot - 24, 0, 4, upper)

    chunk = 512
    for c0 in range(0, seq_len, chunk):
        c1 = min(c0 + chunk, seq_len)
        a0, a1 = pad + c0, pad + c1
        s8 = s8_ref[a0 - 4:a1 - 4, upper]
        s16 = s8_ref[a0 - 8:a1 - 8, upper] + s8_ref[a0:a1, upper]
        first = lax.broadcasted_iota(jnp.int32, s16.shape, 1) < POOL_GROUP_DIM
        win = jnp.concatenate([jnp.where(first, s2_ref[a0:a1, lower], s4_ref[a0:a1, lower]),
                               jnp.where(first, s8, s16)], axis=-1)
        pooled = (win * icnt_ref[c0:c1, :] - xp_ref[a0:a1, :]).astype(BF16)
        o_ref[0, c0:c1, :] = (_dot(pooled, wbd_ref[...]) * ps_ref[...]).astype(BF16)


def _pool(fp3, icnt, wbd, ps):
    bsz, seq_len, _ = fp3.shape
    tot = seq_len + 2 * POOL_PAD
    return pl.pallas_call(
        functools.partial(_pool_kernel, seq_len=seq_len),
        grid=(bsz,),
        in_specs=[pl.BlockSpec((1, seq_len, 256), lambda b: (b, 0, 1)),
                  _whole(icnt), _whole(wbd), _whole(ps)],
        out_specs=pl.BlockSpec((1, seq_len, 256), lambda b: (b, 0, 0)),
        out_shape=jax.ShapeDtypeStruct((bsz, seq_len, 256), BF16),
        scratch_shapes=[pltpu.VMEM((tot, 256), F32)] * 4,
        compiler_params=pltpu.CompilerParams(dimension_semantics=("parallel",),
                                             vmem_limit_bytes=VMEM_LIMIT),
        name="pool",
    )(*_operands(fp3, icnt, wbd, ps))


def _attn_kernel(q_ref, k_ref, v_ref, o_ref, vpad_ref, vt_ref, *s_refs, seq_len, ts):
    lp = vt_ref.shape[2]
    pad = lp - seq_len
    dk = q_ref.shape[2] // ATTN_HEADS
    key = lax.broadcasted_iota(jnp.int32, (ATTN_ONES, lp), 1)
    if pad:
        vpad_ref[seq_len:lp, :] = jnp.zeros((pad, V_HEAD_DIM), F32)
    for hh in range(ATTN_HEADS):
        vpad_ref[0:seq_len, :] = v_ref[0, :, hh * V_HEAD_DIM:(hh + 1) * V_HEAD_DIM].astype(F32)
        vt_ref[hh, 0:V_HEAD_DIM, :] = vpad_ref[...].T.astype(BF16)
        vt_ref[hh, V_HEAD_DIM:, :] = jnp.where(key < seq_len, 1.0, 0.0).astype(BF16)

    group = ts * ATTN_STREAMS
    n_groups = seq_len // group
    tail = seq_len - n_groups * group
    tail_row = seq_len - ts
    half_keys = (seq_len // 2) // LANES * LANES

    def put(buf, *parts):
        lo = 0
        for st in parts:
            buf[lo:lo + st.shape[0], :] = st
            lo += st.shape[0]
        m = functools.reduce(jnp.maximum, [jnp.max(st, axis=0, keepdims=True) for st in parts])
        buf[seq_len:seq_len + 8, :] = jnp.broadcast_to(m, (8, ts))

    def rows_of(g):
        rows = [g * group + j * ts for j in range(ATTN_STREAMS)]
        return rows if isinstance(g, int) else [pl.multiple_of(r, ts) for r in rows]

    def scores(bufs, hh, g):
        cols = slice(hh * dk, (hh + 1) * dk)
        for buf, r in zip(bufs, rows_of(g)):
            put(buf, _dot_nt(k_ref[0, :, cols], q_ref[0, pl.ds(r, ts), cols]))

    def tail_scores(buf, hh):
        cols = slice(hh * dk, (hh + 1) * dk)
        q = q_ref[0, tail_row:seq_len, cols]
        put(buf, _dot_nt(k_ref[0, 0:half_keys, cols], q), _dot_nt(k_ref[0, half_keys:seq_len, cols], q))

    def weighted(buf, hh):
        m = buf[seq_len:seq_len + 1, :]
        bounds = [seq_len * i // ATTN_KEY_PARTS // LANES * LANES for i in range(ATTN_KEY_PARTS)] + [seq_len]
        acc = None
        for lo, hi in zip(bounds[:-1], bounds[1:]):
            p = jnp.exp2(buf[lo:hi, :] - m).astype(BF16)
            if hi == seq_len and pad:
                p, hi = jnp.concatenate([p, jnp.zeros((pad, ts), BF16)], axis=0), lp
            part = _dot(vt_ref[hh, :, lo:hi], p)
            acc = part if acc is None else acc + part
        return acc

    def finish(buf, hh, r):
        ov = weighted(buf, hh)
        o = ov[:V_HEAD_DIM] * pl.reciprocal(ov[V_HEAD_DIM:V_HEAD_DIM + 1])
        o_ref[0, pl.ds(r, ts), hh * V_HEAD_DIM:(hh + 1) * V_HEAD_DIM] = o.T.astype(BF16)

    def outputs(bufs, hh, g):
        for buf, r in zip(bufs, rows_of(g)):
            finish(buf, hh, r)

    first, second = s_refs[:ATTN_STREAMS], s_refs[ATTN_STREAMS:]
    scores(first, 0, 0)
    for hh in range(ATTN_HEADS):
        def pair(k, _, hh=hh, first=first, second=second):
            scores(second, hh, 2 * k + 1)
            outputs(first, hh, 2 * k)
            scores(first, hh, 2 * k + 2)
            outputs(second, hh, 2 * k + 1)
            return 0

        lax.fori_loop(0, n_groups // 2 - 1, pair, 0)
        scores(second, hh, n_groups - 1)
        outputs(first, hh, n_groups - 2)
        if tail:
            tail_scores(first[0], hh)
        outputs(second, hh, n_groups - 1)
        if hh + 1 < ATTN_HEADS:
            scores(second, hh + 1, 0)
        if tail:
            finish(first[0], hh, tail_row)
        first, second = second, first


def _attention(q3, k3, v3, *, ts):
    bsz, seq_len, _ = q3.shape
    assert ts % LANES == 0 and (seq_len // (ts * ATTN_STREAMS)) % 2 == 0 and seq_len % (ts * ATTN_STREAMS) <= ts
    lp = _round_up(seq_len, LANES)
    nh = ATTN_HEADS
    return pl.pallas_call(
        functools.partial(_attn_kernel, seq_len=seq_len, ts=ts),
        grid=(bsz, MLA_HEADS // nh),
        in_specs=[pl.BlockSpec((1, seq_len, 256 * nh), lambda b, h: (b, 0, h)),
                  pl.BlockSpec((1, seq_len, 256 * nh), lambda b, h: (b, 0, h)),
                  pl.BlockSpec((1, seq_len, V_HEAD_DIM * nh), lambda b, h: (b, 0, h))],
        out_specs=pl.BlockSpec((1, seq_len, V_HEAD_DIM * nh), lambda b, h: (b, 0, h)),
        out_shape=jax.ShapeDtypeStruct((bsz, seq_len, MLA_HEADS * V_HEAD_DIM), BF16),
        scratch_shapes=[pltpu.VMEM((lp, V_HEAD_DIM), F32), pltpu.VMEM((nh, V_HEAD_DIM + ATTN_ONES, lp), BF16)]
        + [pltpu.VMEM((seq_len + 8, ts), F32)] * (2 * ATTN_STREAMS),
        compiler_params=pltpu.CompilerParams(dimension_semantics=("parallel", "parallel"),
                                             vmem_limit_bytes=VMEM_LIMIT),
        name="attention",
    )(q3, k3, v3)


def _rope_tables(seq_len, rows):
    inv = 1.0 / (ROPE_THETA ** (jnp.arange(0, QK_ROPE_DIM, 2, dtype=F32) / QK_ROPE_DIM))
    pos = (jnp.arange(rows) % seq_len).astype(F32)
    ang = pos[:, None] * inv[None, :]
    cos, sin = jnp.cos(ang), jnp.sin(ang)
    return jnp.tile(cos, (1, 4)), jnp.tile(jnp.concatenate([-sin, sin], axis=-1), (1, 2))


def _dft_tables(seq_len):
    quarter = seq_len // 4
    hp, qp = _round_up(quarter, 16), _round_up(quarter, LANES)
    step = 64
    k = jnp.arange(hp, dtype=jnp.int32)[:, None]
    a = step * jnp.arange(qp // step, dtype=jnp.int32)[None, :]
    b = jnp.arange(step, dtype=jnp.int32)[None, :]
    valid = ((k < quarter)[:, :, None] & ((a[:, :, None] + b[:, None, :]) < quarter)).reshape(hp, qp)
    unit = 2.0 * math.pi / seq_len
    mats = []
    for r in range(4):
        row = 4 * k + r
        ang_a = (row * a % seq_len).astype(F32) * unit
        ang_b = (row * b % seq_len).astype(F32) * unit
        ca, sa = jnp.cos(ang_a)[:, :, None], jnp.sin(ang_a)[:, :, None]
        cb, sb = jnp.cos(ang_b)[:, None, :], jnp.sin(ang_b)[:, None, :]
        cos = (ca * cb - sa * sb).reshape(hp, qp)
        msin = -(sa * cb + ca * sb).reshape(hp, qp)
        mats.append(jnp.concatenate([jnp.where(valid, cos, 0.0), jnp.where(valid, msin, 0.0)], axis=1))
    return jnp.stack(mats).astype(BF16)


def _channel_dft(seq_len):
    c = np.arange(FOURIER_HEAD_DIM)
    ang = 2.0 * np.pi * ((c[:, None] * c[None, :]) % FOURIER_HEAD_DIM) / FOURIER_HEAD_DIM
    norm = 1.0 / math.sqrt(seq_len * FOURIER_HEAD_DIM)
    eye = np.eye(FOURIER_HEADS)
    return jnp.asarray(np.stack([np.kron(eye, np.cos(ang) * norm), np.kron(eye, np.sin(ang) * norm)]), F32)


def _block_diag(w):
    n, g, c, d = w.shape
    out = jnp.zeros((n, g * c, g * d), w.dtype)
    for i in range(g):
        out = out.at[:, i * c:(i + 1) * c, i * d:(i + 1) * d].set(w[:, i])
    return out


def _pool_inv_count(seq_len):
    idx = np.arange(seq_len)
    cols = []
    for w in POOL_WINDOWS:
        cnt = np.clip(idx + w // 2, 0, seq_len) - np.clip(idx - w // 2, 0, seq_len)
        cols.append(np.repeat((1.0 / cnt)[:, None], POOL_GROUP_DIM, axis=1))
    return jnp.asarray(np.concatenate(cols, axis=1), F32)


def _layer_weights(w_in, w_uq, w_ukv):
    n, d_q, d_kv = w_in.shape[0], w_uq.shape[1], w_ukv.shape[1]
    sizes = (256, 256, 256, 256, d_q, d_kv, QK_ROPE_DIM, 512)
    offs = np.concatenate([[0], np.cumsum(sizes)])
    f_in, f_gate, p_in, p_gate, c_q, c_kv, k_r, a_gate = (w_in[..., offs[i]:offs[i + 1]] for i in range(8))
    half = QK_ROPE_DIM // 2
    swap = lambda a: jnp.concatenate([a[..., half:], a[..., :half]], axis=-1)
    w1 = jnp.concatenate([f_in, p_in, f_gate, p_gate, a_gate, c_q, c_kv,
                          k_r, k_r, swap(k_r), swap(k_r)], axis=-1).astype(BF16)
    uq = w_uq.reshape(n, d_q, MLA_HEADS, QK_NOPE_DIM + QK_ROPE_DIM)
    q_rope = uq[..., QK_NOPE_DIM:]
    wuq = jnp.concatenate([uq[..., :QK_NOPE_DIM].reshape(n, d_q, -1), q_rope.reshape(n, d_q, -1),
                           swap(q_rope).reshape(n, d_q, -1)], axis=-1).astype(BF16)
    ukv = w_ukv.reshape(n, d_kv, MLA_HEADS, QK_NOPE_DIM + V_HEAD_DIM)
    wukv = jnp.concatenate([ukv[..., :QK_NOPE_DIM].reshape(n, d_kv, -1),
                            ukv[..., QK_NOPE_DIM:].reshape(n, d_kv, -1)], axis=-1).astype(BF16)
    return w1, wuq, wukv


def kernel(x, meta_tokens, norm_w, w_in, fourier_w, pool_w, pool_scale, q_norm_w, w_uq, kv_norm_w, w_ukv,
           w_out, final_norm_w):
    bsz, seq, d = x.shape
    depth = norm_w.shape[0]
    seq_len = seq + N_META
    n_tok = bsz * seq_len
    assert seq_len % 16 == 0 and TOKEN_TM % N_META == 0 and TOKEN_TM <= seq and seq % FINAL_TM == 0

    cos_t, sin_t = _rope_tables(seq_len, seq_len + TOKEN_TM)
    mats = _dft_tables(seq_len)
    cs = _channel_dft(seq_len)
    icnt = _pool_inv_count(seq_len)
    fw = final_norm_w.reshape(1, d)

    w1, wuq, wukv = _layer_weights(w_in, w_uq, w_ukv)
    w_o = w_out.astype(BF16)
    f_bd = _block_diag(fourier_w)
    p_bd = _block_diag(pool_w).astype(BF16)

    rows = lambda a: a.reshape(depth, 1, -1)
    nw, qnw, kvnw, ps = rows(norm_w), rows(q_norm_w), rows(kv_norm_w), rows(pool_scale)

    def in_weights(l):
        return (*(_Layer(a, l) for a in (nw, w1, qnw, wuq, kvnw, wukv)), cos_t, sin_t)

    call = functools.partial(_token_call, n_tok=n_tok, tm=TOKEN_TM, seq_len=seq_len)
    h2, fp, sg, q, k, v = call((x.reshape(bsz * seq, d), meta_tokens.astype(x.dtype)), in_weights(0))
    for l in range(depth):
        fp3 = fp.reshape(bsz, seq_len, 512)
        yf = _fourier(fp3, cs, _Layer(f_bd, l), mats).reshape(n_tok, 256)
        yp = _pool(fp3, icnt, _Layer(p_bd, l), _Layer(ps, l)).reshape(n_tok, 256)
        ya = _attention(q.reshape(bsz, seq_len, 1024), k.reshape(bsz, seq_len, 1024),
                        v.reshape(bsz, seq_len, 512), ts=ATTN_TS).reshape(n_tok, 512)
        mix = (yf, yp, ya, sg)
        if l == depth - 1:
            return _final_call(mix, h2, _Layer(w_o, l), fw, bsz=bsz, seq_len=seq_len, tm=FINAL_TM)
        h2, fp, sg, q, k, v = call((mix, h2, _Layer(w_o, l)), in_weights(l + 1))
```

```python
import functools
import math
from typing import NamedTuple

import numpy as np
import jax
import jax.numpy as jnp
from jax import lax
from jax.experimental import pallas as pl
from jax.experimental.pallas import tpu as pltpu

N_META = 16
FOURIER_HEADS = 4
FOURIER_HEAD_DIM = 64
POOL_WINDOWS = (2, 4, 8, 16)
POOL_GROUP_DIM = 64
MLA_HEADS = 4
QK_NOPE_DIM = 128
QK_ROPE_DIM = 64
V_HEAD_DIM = 128
ROPE_THETA = 10000.0
NORM_EPS = 1e-6

LANES = 128
POOL_PAD = 32
VMEM_LIMIT = 56 * 1024 * 1024
TOKEN_TM = 512
FINAL_TM = 1024
ATTN_HEADS = 2
ATTN_STREAMS = 2
ATTN_TS = 256
ATTN_KEY_PARTS = 1
ATTN_ONES = 16

F32 = jnp.float32
BF16 = jnp.bfloat16


def _round_up(a, m):
    return (a + m - 1) // m * m


def _dot(a, b):
    return jnp.dot(a, b, preferred_element_type=F32)


def _dot_nt(a, b):
    return lax.dot_general(a, b, (((1,), (1,)), ((), ())), preferred_element_type=F32)


def _rms(x, w):
    return x * lax.rsqrt(jnp.mean(x * x, axis=-1, keepdims=True) + NORM_EPS) * w


class _Layer(NamedTuple):
    stack: jax.Array
    index: int

    @property
    def shape(self):
        return self.stack.shape[1:]


def _whole(a):
    if isinstance(a, _Layer):
        return pl.BlockSpec((None,) + a.shape, lambda *g: (a.index,) + (0,) * len(a.shape))
    return pl.BlockSpec(a.shape, lambda *g: (0,) * a.ndim)


def _operands(*xs):
    return [x.stack if isinstance(x, _Layer) else x for x in xs]


def _project_in(x, nw_ref, w1_ref, qnw_ref, wuq_ref, kvnw_ref, wukv_ref, cos_ref, sin_ref,
                fp_ref, sg_ref, q_ref, k_ref, v_ref, *, tm, seq_len, scale, d_q, d_kv):
    t = pl.program_id(0)
    inv = lax.rsqrt(jnp.mean(x * x, axis=-1, keepdims=True) + NORM_EPS)
    n = (x * nw_ref[...]).astype(BF16)

    fp_ref[...] = _dot(n, w1_ref[:, 0:512]) * inv
    g = _dot(n, w1_ref[:, 512:1536]) * inv
    sg_ref[...] = (g * pl.reciprocal(1.0 + jnp.exp(-g))).astype(BF16)

    start = pl.multiple_of(lax.rem(t * tm, seq_len), 16)
    cos = cos_ref[pl.ds(start, tm), :]
    sin = sin_ref[pl.ds(start, tm), :]

    def latent(lo, width, w_ref):
        u = _dot(n, w1_ref[:, lo:lo + width])
        ms = jnp.mean(u * u, axis=-1, keepdims=True)
        return (u * w_ref[...]).astype(BF16), inv * lax.rsqrt(inv * inv * ms + NORM_EPS)

    o = 1536
    cq, q_inv = latent(o, d_q, qnw_ref)
    o += d_q
    ckv, kv_inv = latent(o, d_kv, kvnw_ref)
    o += d_kv
    kr = _dot(n, w1_ref[:, o:o + 2 * LANES]) * inv

    qf = _dot(cq, wuq_ref[...]) * (q_inv * scale)
    nq = MLA_HEADS * QK_NOPE_DIM
    nr = MLA_HEADS * QK_ROPE_DIM
    for h in range(MLA_HEADS):
        q_ref[:, 256 * h:256 * h + 128] = qf[:, 128 * h:128 * h + 128].astype(BF16)
    for pair in range(MLA_HEADS // 2):
        r = qf[:, nq + 128 * pair:nq + 128 * pair + 128]
        rs = qf[:, nq + nr + 128 * pair:nq + nr + 128 * pair + 128]
        roped = (r * cos + rs * sin).astype(BF16)
        for h in (2 * pair, 2 * pair + 1):
            q_ref[:, 256 * h + 128:256 * h + 256] = roped

    kvf = _dot(ckv, wukv_ref[...]) * kv_inv
    v_ref[...] = kvf[:, nq:].astype(BF16)
    kroped = kr[:, :LANES] * cos + kr[:, LANES:] * sin
    lane = lax.broadcasted_iota(jnp.int32, kroped.shape, 1)
    k_lo = jnp.where(lane < QK_ROPE_DIM, kroped, 0.0).astype(BF16)
    k_hi = jnp.where(lane >= QK_ROPE_DIM, kroped, 0.0).astype(BF16)
    for h in range(MLA_HEADS):
        k_ref[:, 256 * h:256 * h + 128] = kvf[:, 128 * h:128 * h + 128].astype(BF16)
        k_ref[:, 256 * h + 128:256 * h + 256] = k_lo if h % 2 == 0 else k_hi


def _mix_residual(yf_ref, yp_ref, ya_ref, sg_ref, h_ref, w_ref):
    y = jnp.concatenate([yf_ref[...], yp_ref[...], ya_ref[...]], axis=-1)
    return h_ref[...] + _dot(y * sg_ref[...], w_ref[...])


def _window_start(t, tm, seq_len):
    b0, p0 = (t * tm) // seq_len, lax.rem(t * tm, seq_len)
    return b0 * (seq_len - N_META) + jnp.maximum(p0 - N_META, 0)


def _embed_kernel(xw_ref, meta_ref, *refs, tm, seq_len, n_x, **kw):
    t = pl.program_id(0)
    h_ref = refs[8]
    first_pos = lax.rem(t * tm, seq_len)
    meta_group = lax.rem(seq_len - first_pos, seq_len) // N_META
    start = _window_start(t, tm, seq_len)
    shift = start - jnp.minimum(start, n_x - tm)
    for j in range(tm // N_META):
        src = shift + N_META * (j - (meta_group < j).astype(jnp.int32))
        src = pl.multiple_of(jnp.minimum(src, tm - N_META), N_META)
        blk = jnp.where(meta_group == j, meta_ref[...], xw_ref[pl.ds(src, N_META), :])
        h_ref[N_META * j:N_META * (j + 1), :] = blk
    _project_in(h_ref[...], *refs[:8], *refs[9:], tm=tm, seq_len=seq_len, **kw)


def _fused_kernel(yf_ref, yp_ref, ya_ref, sg_ref, h_ref, w_ref, *refs, **kw):
    h_new = _mix_residual(yf_ref, yp_ref, ya_ref, sg_ref, h_ref, w_ref)
    refs[8][...] = h_new
    _project_in(h_new, *refs[:8], *refs[9:], **kw)


def _final_kernel(yf_ref, yp_ref, ya_ref, sg_ref, h_ref, w_ref, fw_ref, o_ref):
    o_ref[0] = _rms(_mix_residual(yf_ref, yp_ref, ya_ref, sg_ref, h_ref, w_ref), fw_ref[...])


_MIX_WIDTHS = (256, 256, 512, 1024)
_IN_OUT = ((512, F32), (1024, BF16), (1024, BF16), (1024, BF16), (512, BF16))


def _token_call(src, in_w, *, n_tok, tm, seq_len):
    nw, w1, qnw, wuq, kvnw, wukv, cos_t, sin_t = in_w
    d = nw.shape[1]
    scale = float((QK_NOPE_DIM + QK_ROPE_DIM) ** -0.5 * math.log2(math.e))
    row = lambda w: pl.BlockSpec((tm, w), lambda t: (t, 0))
    kw = dict(tm=tm, seq_len=seq_len, scale=scale, d_q=qnw.shape[1], d_kv=kvnw.shape[1])
    out_specs = [row(d)] + [row(w) for w, _ in _IN_OUT]
    out_shape = [jax.ShapeDtypeStruct((n_tok, d), F32)] + [jax.ShapeDtypeStruct((n_tok, w), dt) for w, dt in _IN_OUT]
    if len(src) == 2:
        x2, meta = src
        kw["n_x"] = x2.shape[0]

        def window(t):
            return pl.multiple_of(jnp.minimum(_window_start(t, tm, seq_len), x2.shape[0] - tm), N_META), 0

        body, args, aliases = _embed_kernel, (x2, meta, *in_w), {}
        in_specs = [pl.BlockSpec((pl.Element(tm), pl.Element(d)), window), _whole(meta)]
    else:
        mix, h2, w_out = src
        body, args, aliases = _fused_kernel, (*mix, h2, w_out, *in_w), {}
        in_specs = [row(w) for w in _MIX_WIDTHS] + [row(d), _whole(w_out)]
    in_specs += [_whole(a) for a in in_w]
    return pl.pallas_call(
        functools.partial(body, **kw),
        grid=(pl.cdiv(n_tok, tm),),
        in_specs=in_specs, out_specs=out_specs, out_shape=out_shape,
        input_output_aliases=aliases,
        compiler_params=pltpu.CompilerParams(dimension_semantics=("arbitrary",),
                                             vmem_limit_bytes=VMEM_LIMIT),
        name="embed" if len(src) == 2 else "outin",
    )(*_operands(*args))


def _final_call(mix, h2, w_out, fw, *, bsz, seq_len, tm):
    n_tok, d = h2.shape
    seq = seq_len - N_META
    row = lambda w: pl.BlockSpec((pl.Element(tm), pl.Element(w)),
                                 lambda b, j: (pl.multiple_of(b * seq_len + N_META + j * tm, 16), 0))
    return pl.pallas_call(
        _final_kernel,
        grid=(bsz, seq // tm),
        in_specs=[row(w) for w in _MIX_WIDTHS] + [row(d), _whole(w_out), _whole(fw)],
        out_specs=pl.BlockSpec((1, tm, d), lambda b, j: (b, j, 0)),
        out_shape=jax.ShapeDtypeStruct((bsz, seq, d), F32),
        compiler_params=pltpu.CompilerParams(dimension_semantics=("arbitrary", "arbitrary"),
                                             vmem_limit_bytes=VMEM_LIMIT),
        name="final",
    )(*_operands(*mix, h2, w_out, fw))


def _fourier_kernel(x_ref, cs_ref, wbd_ref, mats_ref, o_ref, pq_ref, u_ref, y_ref, *, quarter):
    b = pl.program_id(0)
    qp = u_ref.shape[1]

    @pl.when(b == 0)
    def _():
        w = wbd_ref[...]
        pq_ref[:, 0:256] = jnp.dot(cs_ref[0], w, preferred_element_type=F32,
                                   precision=lax.Precision.HIGHEST).astype(BF16)
        pq_ref[:, 256:512] = jnp.dot(cs_ref[1], w, preferred_element_type=F32,
                                     precision=lax.Precision.HIGHEST).astype(BF16)
        u_ref[:, quarter:qp, :] = jnp.zeros((4, qp - quarter, u_ref.shape[2]), F32)

    x0, x1, x2, x3 = (x_ref[0, j * quarter:(j + 1) * quarter, :] for j in range(4))
    s02, s13, d02, d13 = x0 + x2, x1 + x3, x0 - x2, x1 - x3
    for j, u in enumerate((s02 + s13, s02 - s13, d02, d13)):
        u_ref[j, 0:quarter, :] = u
    pq = pq_ref[...]
    ab = [_dot(u_ref[j].astype(BF16), pq) for j in range(4)]
    re = (ab[0][:, :256], ab[2][:, :256] - ab[3][:, 256:], ab[1][:, :256], ab[2][:, :256] + ab[3][:, 256:])
    im = (ab[0][:, 256:], ab[2][:, 256:] + ab[3][:, :256], ab[1][:, 256:], ab[2][:, 256:] - ab[3][:, :256])
    for r in range(4):
        rhs = jnp.concatenate([re[r].astype(BF16), im[r].astype(BF16)], axis=0)
        y = _dot(mats_ref[r], rhs)
        for c in range(2):
            y_ref[c, pl.ds(r, quarter, stride=4), :] = y[0:quarter, c * LANES:(c + 1) * LANES]
    for c in range(2):
        o_ref[0, :, c * LANES:(c + 1) * LANES] = y_ref[c].astype(BF16)


def _fourier(fp3, cs, wbd, mats):
    bsz, seq_len, _ = fp3.shape
    quarter = seq_len // 4
    qp = mats.shape[2] // 2
    return pl.pallas_call(
        functools.partial(_fourier_kernel, quarter=quarter),
        grid=(bsz,),
        in_specs=[pl.BlockSpec((1, seq_len, 256), lambda b: (b, 0, 0)),
                  _whole(cs), _whole(wbd),
                  pl.BlockSpec(mats.shape, lambda b: (0, 0, 0), pipeline_mode=pl.Buffered(1))],
        out_specs=pl.BlockSpec((1, seq_len, 256), lambda b: (b, 0, 0)),
        out_shape=jax.ShapeDtypeStruct((bsz, seq_len, 256), BF16),
        scratch_shapes=[pltpu.VMEM((256, 512), BF16), pltpu.VMEM((4, qp, 256), F32),
                        pltpu.VMEM((2, seq_len, LANES), F32)],
        compiler_params=pltpu.CompilerParams(dimension_semantics=("arbitrary",),
                                             vmem_limit_bytes=VMEM_LIMIT),
        name="fourier",
    )(*_operands(fp3, cs, wbd, mats))


def _pool_kernel(x_ref, icnt_ref, wbd_ref, ps_ref, o_ref, xp_ref, s2_ref, s4_ref, s8_ref, *, seq_len):
    pad = POOL_PAD
    tot = seq_len + 2 * pad
    zeros = jnp.zeros((pad, 256), F32)
    xp_ref[0:pad, :] = zeros
    xp_ref[pad + seq_len:tot, :] = zeros
    xp_ref[pad:pad + seq_len, :] = x_ref[0]

    def level(src, dst, lo, hi, d_lo, d_hi, cols):
        chunk = 512
        for c0 in range(lo, hi, chunk):
            c1 = min(c0 + chunk, hi)
            dst[c0:c1, cols] = src[c0 + d_lo:c1 + d_lo, cols] + src[c0 + d_hi:c1 + d_hi, cols]

    lower, upper = slice(0, LANES), slice(LANES, 2 * LANES)
    level(xp_ref, s2_ref, 8, tot - 8, -1, 0, lower)
    level(s2_ref, s4_ref, 16, tot - 16, -1, 1, lower)
    level(xp_ref, s2_ref, 8, tot - 8, 0, 1, upper)
    level(s2_ref, s4_ref, 8, tot - 16, 0, 2, upper)
    level(s4_ref, s8_ref, 8, tot - 24, 0, 4, upper)

    chunk = 512
    for c0 in range(0, seq_len, chunk):
        c1 = min(c0 + chunk, seq_len)
        a0, a1 = pad + c0, pad + c1
        s8 = s8_ref[a0 - 4:a1 - 4, upper]
        s16 = s8_ref[a0 - 8:a1 - 8, upper] + s8_ref[a0:a1, upper]
        first = lax.broadcasted_iota(jnp.int32, s16.shape, 1) < POOL_GROUP_DIM
        win = jnp.concatenate([jnp.where(first, s2_ref[a0:a1, lower], s4_ref[a0:a1, lower]),
                               jnp.where(first, s8, s16)], axis=-1)
        pooled = (win * icnt_ref[c0:c1, :] - xp_ref[a0:a1, :]).astype(BF16)
        o_ref[0, c0:c1, :] = (_dot(pooled, wbd_ref[...]) * ps_ref[...]).astype(BF16)


def _pool(fp3, icnt, wbd, ps):
    bsz, seq_len, _ = fp3.shape
    tot = seq_len + 2 * POOL_PAD
    return pl.pallas_call(
        functools.partial(_pool_kernel, seq_len=seq_len),
        grid=(bsz,),
        in_specs=[pl.BlockSpec((1, seq_len, 256), lambda b: (b, 0, 1)),
                  _whole(icnt), _whole(wbd), _whole(ps)],
        out_specs=pl.BlockSpec((1, seq_len, 256), lambda b: (b, 0, 0)),
        out_shape=jax.ShapeDtypeStruct((bsz, seq_len, 256), BF16),
        scratch_shapes=[pltpu.VMEM((tot, 256), F32)] * 4,
        compiler_params=pltpu.CompilerParams(dimension_semantics=("arbitrary",),
                                             vmem_limit_bytes=VMEM_LIMIT),
        name="pool",
    )(*_operands(fp3, icnt, wbd, ps))


def _attn_kernel(q_ref, k_ref, v_ref, o_ref, vpad_ref, vt_ref, *s_refs, seq_len, ts):
    lp = vt_ref.shape[2]
    pad = lp - seq_len
    dk = q_ref.shape[2] // ATTN_HEADS
    key = lax.broadcasted_iota(jnp.int32, (ATTN_ONES, lp), 1)
    if pad:
        vpad_ref[seq_len:lp, :] = jnp.zeros((pad, V_HEAD_DIM), F32)
    for hh in range(ATTN_HEADS):
        vpad_ref[0:seq_len, :] = v_ref[0, :, hh * V_HEAD_DIM:(hh + 1) * V_HEAD_DIM].astype(F32)
        vt_ref[hh, 0:V_HEAD_DIM, :] = vpad_ref[...].T.astype(BF16)
        vt_ref[hh, V_HEAD_DIM:, :] = jnp.where(key < seq_len, 1.0, 0.0).astype(BF16)

    group = ts * ATTN_STREAMS
    n_groups = seq_len // group
    tail = seq_len - n_groups * group
    tail_row = seq_len - ts
    half_keys = (seq_len // 2) // LANES * LANES

    def put(buf, *parts):
        lo = 0
        for st in parts:
            buf[lo:lo + st.shape[0], :] = st
            lo += st.shape[0]
        m = functools.reduce(jnp.maximum, [jnp.max(st, axis=0, keepdims=True) for st in parts])
        buf[seq_len:seq_len + 8, :] = jnp.broadcast_to(m, (8, ts))

    def rows_of(g):
        rows = [g * group + j * ts for j in range(ATTN_STREAMS)]
        return rows if isinstance(g, int) else [pl.multiple_of(r, ts) for r in rows]

    def scores(bufs, hh, g):
        cols = slice(hh * dk, (hh + 1) * dk)
        for buf, r in zip(bufs, rows_of(g)):
            put(buf, _dot_nt(k_ref[0, :, cols], q_ref[0, pl.ds(r, ts), cols]))

    def tail_scores(buf, hh):
        cols = slice(hh * dk, (hh + 1) * dk)
        q = q_ref[0, tail_row:seq_len, cols]
        put(buf, _dot_nt(k_ref[0, 0:half_keys, cols], q), _dot_nt(k_ref[0, half_keys:seq_len, cols], q))

    def weighted(buf, hh):
        m = buf[seq_len:seq_len + 1, :]
        bounds = [seq_len * i // ATTN_KEY_PARTS // LANES * LANES for i in range(ATTN_KEY_PARTS)] + [seq_len]
        acc = None
        for lo, hi in zip(bounds[:-1], bounds[1:]):
            p = jnp.exp2(buf[lo:hi, :] - m).astype(BF16)
            if hi == seq_len and pad:
                p, hi = jnp.concatenate([p, jnp.zeros((pad, ts), BF16)], axis=0), lp
            part = _dot(vt_ref[hh, :, lo:hi], p)
            acc = part if acc is None else acc + part
        return acc

    def finish(buf, hh, r):
        ov = weighted(buf, hh)
        o = ov[:V_HEAD_DIM] * pl.reciprocal(ov[V_HEAD_DIM:V_HEAD_DIM + 1])
        o_ref[0, pl.ds(r, ts), hh * V_HEAD_DIM:(hh + 1) * V_HEAD_DIM] = o.T.astype(BF16)

    def outputs(bufs, hh, g):
        for buf, r in zip(bufs, rows_of(g)):
            finish(buf, hh, r)

    first, second = s_refs[:ATTN_STREAMS], s_refs[ATTN_STREAMS:]
    scores(first, 0, 0)
    for hh in range(ATTN_HEADS):
        def pair(k, _, hh=hh, first=first, second=second):
            scores(second, hh, 2 * k + 1)
            outputs(first, hh, 2 * k)
            scores(first, hh, 2 * k + 2)
            outputs(second, hh, 2 * k + 1)
            return 0

        lax.fori_loop(0, n_groups // 2 - 1, pair, 0)
        scores(second, hh, n_groups - 1)
        outputs(first, hh, n_groups - 2)
        if tail:
            tail_scores(first[0], hh)
        outputs(second, hh, n_groups - 1)
        if hh + 1 < ATTN_HEADS:
            scores(second, hh + 1, 0)
        if tail:
            finish(first[0], hh, tail_row)
        first, second = second, first


def _attention(q3, k3, v3, *, ts):
    bsz, seq_len, _ = q3.shape
    assert ts % LANES == 0 and (seq_len // (ts * ATTN_STREAMS)) % 2 == 0 and seq_len % (ts * ATTN_STREAMS) <= ts
    lp = _round_up(seq_len, LANES)
    nh = ATTN_HEADS
    return pl.pallas_call(
        functools.partial(_attn_kernel, seq_len=seq_len, ts=ts),
        grid=(bsz, MLA_HEADS // nh),
        in_specs=[pl.BlockSpec((1, seq_len, 256 * nh), lambda b, h: (b, 0, h)),
                  pl.BlockSpec((1, seq_len, 256 * nh), lambda b, h: (b, 0, h)),
                  pl.BlockSpec((1, seq_len, V_HEAD_DIM * nh), lambda b, h: (b, 0, h))],
        out_specs=pl.BlockSpec((1, seq_len, V_HEAD_DIM * nh), lambda b, h: (b, 0, h)),
        out_shape=jax.ShapeDtypeStruct((bsz, seq_len, MLA_HEADS * V_HEAD_DIM), BF16),
        scratch_shapes=[pltpu.VMEM((lp, V_HEAD_DIM), F32), pltpu.VMEM((nh, V_HEAD_DIM + ATTN_ONES, lp), BF16)]
        + [pltpu.VMEM((seq_len + 8, ts), F32)] * (2 * ATTN_STREAMS),
        compiler_params=pltpu.CompilerParams(dimension_semantics=("arbitrary", "arbitrary"),
                                             vmem_limit_bytes=VMEM_LIMIT),
        name="attention",
    )(q3, k3, v3)


def _rope_tables(seq_len, rows):
    inv = 1.0 / (ROPE_THETA ** (jnp.arange(0, QK_ROPE_DIM, 2, dtype=F32) / QK_ROPE_DIM))
    pos = (jnp.arange(rows) % seq_len).astype(F32)
    ang = pos[:, None] * inv[None, :]
    cos, sin = jnp.cos(ang), jnp.sin(ang)
    return jnp.tile(cos, (1, 4)), jnp.tile(jnp.concatenate([-sin, sin], axis=-1), (1, 2))


def _dft_tables(seq_len):
    quarter = seq_len // 4
    hp, qp = _round_up(quarter, 16), _round_up(quarter, LANES)
    step = 64
    k = jnp.arange(hp, dtype=jnp.int32)[:, None]
    a = step * jnp.arange(qp // step, dtype=jnp.int32)[None, :]
    b = jnp.arange(step, dtype=jnp.int32)[None, :]
    valid = ((k < quarter)[:, :, None] & ((a[:, :, None] + b[:, None, :]) < quarter)).reshape(hp, qp)
    unit = 2.0 * math.pi / seq_len
    mats = []
    for r in range(4):
        row = 4 * k + r
        ang_a = (row * a % seq_len).astype(F32) * unit
        ang_b = (row * b % seq_len).astype(F32) * unit
        ca, sa = jnp.cos(ang_a)[:, :, None], jnp.sin(ang_a)[:, :, None]
        cb, sb = jnp.cos(ang_b)[:, None, :], jnp.sin(ang_b)[:, None, :]
        cos = (ca * cb - sa * sb).reshape(hp, qp)
        msin = -(sa * cb + ca * sb).reshape(hp, qp)
        mats.append(jnp.concatenate([jnp.where(valid, cos, 0.0), jnp.where(valid, msin, 0.0)], axis=1))
    return jnp.stack(mats).astype(BF16)


def _channel_dft(seq_len):
    c = np.arange(FOURIER_HEAD_DIM)
    ang = 2.0 * np.pi * ((c[:, None] * c[None, :]) % FOURIER_HEAD_DIM) / FOURIER_HEAD_DIM
    norm = 1.0 / math.sqrt(seq_len * FOURIER_HEAD_DIM)
    eye = np.eye(FOURIER_HEADS)
    return jnp.asarray(np.stack([np.kron(eye, np.cos(ang) * norm), np.kron(eye, np.sin(ang) * norm)]), F32)


def _block_diag(w):
    n, g, c, d = w.shape
    out = jnp.zeros((n, g * c, g * d), w.dtype)
    for i in range(g):
        out = out.at[:, i * c:(i + 1) * c, i * d:(i + 1) * d].set(w[:, i])
    return out


def _pool_inv_count(seq_len):
    idx = np.arange(seq_len)
    cols = []
    for w in POOL_WINDOWS:
        cnt = np.clip(idx + w // 2, 0, seq_len) - np.clip(idx - w // 2, 0, seq_len)
        cols.append(np.repeat((1.0 / cnt)[:, None], POOL_GROUP_DIM, axis=1))
    return jnp.asarray(np.concatenate(cols, axis=1), F32)


def _layer_weights(w_in, w_uq, w_ukv):
    n, d_q, d_kv = w_in.shape[0], w_uq.shape[1], w_ukv.shape[1]
    sizes = (256, 256, 256, 256, d_q, d_kv, QK_ROPE_DIM, 512)
    offs = np.concatenate([[0], np.cumsum(sizes)])
    f_in, f_gate, p_in, p_gate, c_q, c_kv, k_r, a_gate = (w_in[..., offs[i]:offs[i + 1]] for i in range(8))
    half = QK_ROPE_DIM // 2
    swap = lambda a: jnp.concatenate([a[..., half:], a[..., :half]], axis=-1)
    w1 = jnp.concatenate([f_in, p_in, f_gate, p_gate, a_gate, c_q, c_kv,
                          k_r, k_r, swap(k_r), swap(k_r)], axis=-1).astype(BF16)
    uq = w_uq.reshape(n, d_q, MLA_HEADS, QK_NOPE_DIM + QK_ROPE_DIM)
    q_rope = uq[..., QK_NOPE_DIM:]
    wuq = jnp.concatenate([uq[..., :QK_NOPE_DIM].reshape(n, d_q, -1), q_rope.reshape(n, d_q, -1),
                           swap(q_rope).reshape(n, d_q, -1)], axis=-1).astype(BF16)
    ukv = w_ukv.reshape(n, d_kv, MLA_HEADS, QK_NOPE_DIM + V_HEAD_DIM)
    wukv = jnp.concatenate([ukv[..., :QK_NOPE_DIM].reshape(n, d_kv, -1),
                            ukv[..., QK_NOPE_DIM:].reshape(n, d_kv, -1)], axis=-1).astype(BF16)
    return w1, wuq, wukv


def kernel(x, meta_tokens, norm_w, w_in, fourier_w, pool_w, pool_scale, q_norm_w, w_uq, kv_norm_w, w_ukv,
           w_out, final_norm_w):
    bsz, seq, d = x.shape
    depth = norm_w.shape[0]
    seq_len = seq + N_META
    n_tok = bsz * seq_len
    assert seq_len % 16 == 0 and TOKEN_TM % N_META == 0 and TOKEN_TM <= seq and seq % FINAL_TM == 0

    cos_t, sin_t = _rope_tables(seq_len, seq_len + TOKEN_TM)
    mats = _dft_tables(seq_len)
    cs = _channel_dft(seq_len)
    icnt = _pool_inv_count(seq_len)
    fw = final_norm_w.reshape(1, d)

    w1, wuq, wukv = _layer_weights(w_in, w_uq, w_ukv)
    w_o = w_out.astype(BF16)
    f_bd = _block_diag(fourier_w)
    p_bd = _block_diag(pool_w).astype(BF16)

    rows = lambda a: a.reshape(depth, 1, -1)
    nw, qnw, kvnw, ps = rows(norm_w), rows(q_norm_w), rows(kv_norm_w), rows(pool_scale)

    def in_weights(l):
        return (*(_Layer(a, l) for a in (nw, w1, qnw, wuq, kvnw, wukv)), cos_t, sin_t)

    call = functools.partial(_token_call, n_tok=n_tok, tm=TOKEN_TM, seq_len=seq_len)
    h2, fp, sg, q, k, v = call((x.reshape(bsz * seq, d), meta_tokens.astype(x.dtype)), in_weights(0))
    for l in range(depth):
        fp3 = fp.reshape(bsz, seq_len, 512)
        yf = _fourier(fp3, cs, _Layer(f_bd, l), mats).reshape(n_tok, 256)
        yp = _pool(fp3, icnt, _Layer(p_bd, l), _Layer(ps, l)).reshape(n_tok, 256)
        ya = _attention(q.reshape(bsz, seq_len, 1024), k.reshape(bsz, seq_len, 1024),
                        v.reshape(bsz, seq_len, 512), ts=ATTN_TS).reshape(n_tok, 512)
        mix = (yf, yp, ya, sg)
        if l == depth - 1:
            return _final_call(mix, h2, _Layer(w_o, l), fw, bsz=bsz, seq_len=seq_len, tm=FINAL_TM)
        h2, fp, sg, q, k, v = call((mix, h2, _Layer(w_o, l)), in_weights(l + 1))
```
